```python
import jax
import jax.numpy as jnp
from jax import lax
import numpy as np

D_MODEL = 1024
BATCH = 8
SEQ = 2048
DEPTH = 4
DEC_BATCH = 128
DEC_SEQ = 8
PAST_LEN = 2048
PAGE_SIZE = 128

N_EVEN = (DEPTH + 1) // 2
N_ODD = DEPTH // 2
A_HEADS = 8
A_KV_HEADS = 2
A_HEAD_DIM = D_MODEL // 16
A_WIDTH = A_HEADS * A_HEAD_DIM
ROT_DIM = A_HEAD_DIM // 4
ROPE_THETA = 500000.0
IDX_HEADS = 4
IDX_DIM = A_HEAD_DIM
TOPK_MAX = 256
Q_BLOCK = 128
RET_HEADS = 4
RET_HEAD_DIM = D_MODEL // 8
RET_WIDTH = RET_HEADS * RET_HEAD_DIM
RET_THETA = 10000.0
RET_CHUNK = 128
D_RNN = D_MODEL
RG_BLOCKS = 8
RG_BLOCK_W = D_RNN // RG_BLOCKS
CONV_W = 4
RG_C = 8.0
D_FF = 128 * ((8 * D_MODEL // 3 + 127) // 128)
EPS = 1e-6
N_MOD = 9
EVEN_SIZES = (A_WIDTH, A_KV_HEADS * A_HEAD_DIM, A_KV_HEADS * A_HEAD_DIM, IDX_HEADS * IDX_DIM, IDX_HEADS, IDX_DIM,
              RET_WIDTH, RET_WIDTH, RET_WIDTH, RET_WIDTH)
P_EVEN = sum(EVEN_SIZES)

kernel_name = 'hybrid_dsa_retention_rglru_step'


def rmsnorm(x, g=None):
    xf = x.astype(jnp.float32)
    y = xf * lax.rsqrt(jnp.mean(xf * xf, axis=-1, keepdims=True) + EPS)
    if g is not None:
        y = y * g.astype(jnp.float32)
    return y.astype(x.dtype)


def modulate(h, shift, scale):
    return h * (1 + scale) + shift


def rope(x, pos, inv_freq):
    half = inv_freq.shape[0]
    rot = 2 * half
    ang = pos.astype(jnp.float32)[:, None] * inv_freq[None, :]
    cos = jnp.cos(ang)[None, :, None, :]
    sin = jnp.sin(ang)[None, :, None, :]
    xf = x.astype(jnp.float32)
    x1 = xf[..., :half]
    x2 = xf[..., half:rot]
    out = jnp.concatenate([x1 * cos - x2 * sin, x2 * cos + x1 * sin, xf[..., rot:]], axis=-1)
    return out.astype(x.dtype)


def partial_inv_freq():
    return jnp.power(jnp.float32(ROPE_THETA), -jnp.arange(0, ROT_DIM, 2, dtype=jnp.float32) / ROT_DIM)


def retention_inv_freq():
    return jnp.power(jnp.float32(RET_THETA), -jnp.linspace(0.0, 1.0, RET_HEAD_DIM // 2, dtype=jnp.float32))


def swiglu(h, wg, wu, wd):
    return (jax.nn.silu(h @ wg) * (h @ wu)) @ wd


def dsa_block(q, qi, wi, qpos, k, v, ki, kpos, topk):
    B, T, H, Dh = q.shape
    hkv = k.shape[2]
    raw = jnp.einsum('bthd,bsd->bths', qi, ki).astype(jnp.float32) * (IDX_DIM ** -0.5)
    score = jnp.einsum('bths,bth->bts', jax.nn.relu(raw), wi.astype(jnp.float32)) * (IDX_HEADS ** -0.5)
    visible = kpos[None, :] <= qpos[:, None]
    score = jnp.where(visible[None], score, -jnp.inf)
    _, idx = lax.top_k(score, topk)
    valid = kpos[idx] <= qpos[None, :, None]
    gather = jax.vmap(lambda a, i: a[i])
    k_sel = gather(k, idx)
    v_sel = gather(v, idx)
    qg = q.reshape(B, T, hkv, H // hkv, Dh)
    logits = jnp.einsum('btgrd,btkgd->btgrk', qg, k_sel).astype(jnp.float32) * (Dh ** -0.5)
    logits = jnp.where(valid[:, :, None, None, :], logits, -jnp.inf)
    prob = jax.nn.softmax(logits, axis=-1).astype(v.dtype)
    o = jnp.einsum('btgrk,btkgd->btgrd', prob, v_sel)
    return o.reshape(B, T, H * Dh)


def dsa_attention(q, qi, wi, qpos, k, v, ki, kpos, topk):
    B, T = q.shape[:2]
    if T <= Q_BLOCK or T % Q_BLOCK != 0:
        return dsa_block(q, qi, wi, qpos, k, v, ki, kpos, topk)
    n = T // Q_BLOCK

    def split(a):
        return jnp.swapaxes(a.reshape(B, n, Q_BLOCK, *a.shape[2:]), 0, 1)

    def one_block(args):
        qb, qib, wb, pb = args
        return dsa_block(qb, qib, wb, pb, k, v, ki, kpos, topk)

    out = lax.map(one_block, (split(q), split(qi), split(wi), qpos.reshape(n, Q_BLOCK)))
    return jnp.swapaxes(out, 0, 1).reshape(B, T, -1)


def retention(q, k, v, s0):
    B, L, H, DK = q.shape
    DV = v.shape[-1]
    C = RET_CHUNK if L % RET_CHUNK == 0 else L
    n = L // C
    log_g = jnp.log1p(-jnp.power(2.0, -5.0 - jnp.arange(H, dtype=jnp.float32)))
    idx = jnp.arange(C, dtype=jnp.float32)
    diff = idx[:, None] - idx[None, :]
    decay = jnp.where(diff >= 0, jnp.exp(jnp.maximum(diff, 0.0)[None] * log_g[:, None, None]), 0.0)
    q_dec = jnp.exp((idx[:, None] + 1.0) * log_g[None, :])
    k_dec = jnp.exp((C - 1.0 - idx[:, None]) * log_g[None, :])
    c_dec = jnp.exp(C * log_g)

    def blocks(a):
        return jnp.moveaxis(a.astype(jnp.float32).reshape(B, n, C, H, a.shape[-1]), 1, 0)

    def step(s, inp):
        qb, kb, vb = inp
        att = jnp.einsum('bihd,bjhd->bhij', qb, kb) * decay[None]
        inner = jnp.einsum('bhij,bjhv->bihv', att, vb)
        cross = jnp.einsum('bihd,bhdv->bihv', qb, s) * q_dec[None, :, :, None]
        s = s * c_dec[None, :, None, None] + jnp.einsum('bjhd,bjhv->bhdv', kb * k_dec[None, :, :, None], vb)
        return s, inner + cross

    s, out = lax.scan(step, s0.astype(jnp.float32), (blocks(q), blocks(k), blocks(v)))
    out = jnp.moveaxis(out, 0, 1).reshape(B, L, H, DV)
    return out, s


def linear_scan(a, b, h0):
    b = b.at[:, 0].add(a[:, 0] * h0)

    def comb(x, y):
        a1, b1 = x
        a2, b2 = y
        return a1 * a2, a2 * b1 + b2

    _, h = lax.associative_scan(comb, (a, b), axis=1)
    return h


def even_mixer(h, pos, e, past, p):
    B, L, _ = h.shape
    proj = h @ p['w_in_even'][e]
    points = [int(s) for s in np.cumsum(EVEN_SIZES)[:-1]]
    q, k, v, qi, wi, ki, rq, rk, rv, rg = jnp.split(proj, points, axis=-1)
    inv_a = partial_inv_freq()
    q = rope(q.reshape(B, L, A_HEADS, A_HEAD_DIM), pos, inv_a)
    k = rope(k.reshape(B, L, A_KV_HEADS, A_HEAD_DIM), pos, inv_a)
    v = v.reshape(B, L, A_KV_HEADS, A_HEAD_DIM)
    qi = rope(qi.reshape(B, L, IDX_HEADS, IDX_DIM), pos, inv_a)
    ki = rope(ki.reshape(B, L, 1, IDX_DIM), pos, inv_a)[:, :, 0]
    if past is None:
        k_all, v_all, ki_all, kpos = k, v, ki, pos
        s0 = jnp.zeros((B, RET_HEADS, RET_HEAD_DIM, RET_HEAD_DIM), jnp.float32)
    else:
        page_table = past['page_table']

        def gather_pages(pool):
            g = pool[e][page_table]
            return g.reshape(B, g.shape[1] * g.shape[2], *g.shape[3:])

        k_all = jnp.concatenate([gather_pages(past['cache_k']).astype(k.dtype), k], axis=1)
        v_all = jnp.concatenate([gather_pages(past['cache_v']).astype(v.dtype), v], axis=1)
        ki_all = jnp.concatenate([gather_pages(past['cache_kidx']).astype(ki.dtype), ki], axis=1)
        kpos = jnp.arange(k_all.shape[1], dtype=jnp.int32)
        s0 = past['state_ret'][e]
    topk = min(TOPK_MAX, k_all.shape[1] // 4)
    o_a = dsa_attention(q, qi, wi, pos, k_all, v_all, ki_all, kpos, topk)
    inv_r = retention_inv_freq()
    rq = rope(rq.reshape(B, L, RET_HEADS, RET_HEAD_DIM), pos, inv_r)
    rk = rope(rk.reshape(B, L, RET_HEADS, RET_HEAD_DIM) * (RET_HEAD_DIM ** -0.5), pos, inv_r)
    rv = rv.reshape(B, L, RET_HEADS, RET_HEAD_DIM)
    o_r, s_new = retention(rq, rk, rv, s0)
    o_r = rmsnorm(o_r).reshape(B, L, RET_WIDTH).astype(h.dtype) * jax.nn.silu(rg)
    out = jnp.concatenate([o_a, o_r], axis=-1) @ p['w_out_even'][e]
    return out, (k, v, ki, s_new.astype(h.dtype))


def odd_mixer(h, pos, o, past, p):
    B, L, _ = h.shape
    proj = h @ p['w_in_odd'][o]
    gate_br, xb = jnp.split(proj, 2, axis=-1)
    if past is None:
        buf = jnp.zeros((B, CONV_W - 1, D_RNN), xb.dtype)
        h0 = jnp.zeros((B, D_RNN), jnp.float32)
    else:
        buf = past['state_conv'][o].astype(xb.dtype)
        h0 = past['state_rglru'][o].astype(jnp.float32)
    xc = jnp.concatenate([buf, xb], axis=1)
    cw = p['conv_w'][o]
    conv = sum(xc[:, j:j + L] * cw[j] for j in range(CONV_W)) + p['conv_b'][o]
    new_buf = xc[:, L:]
    xg = conv.reshape(B, L, RG_BLOCKS, RG_BLOCK_W)
    r = jax.nn.sigmoid(jnp.einsum('blnd,nde->blne', xg, p['w_rg_a'][o]).reshape(B, L, D_RNN) + p['b_rg_a'][o])
    i = jax.nn.sigmoid(jnp.einsum('blnd,nde->blne', xg, p['w_rg_x'][o]).reshape(B, L, D_RNN) + p['b_rg_x'][o])
    log_a = -RG_C * r.astype(jnp.float32) * jax.nn.softplus(-p['rg_lambda'][o].astype(jnp.float32))
    a = jnp.exp(log_a)
    mult = jnp.sqrt(-jnp.expm1(2.0 * log_a))
    mult = jnp.where((pos == 0)[None, :, None], 1.0, mult)
    b = mult * (i * conv).astype(jnp.float32)
    hs = linear_scan(a, b, h0)
    y = hs.astype(h.dtype) * jax.nn.gelu(gate_br)
    out = y @ p['w_out_odd'][o]
    return out, (new_buf, hs[:, -1].astype(h.dtype))


def trunk(x, c, pos, past, p):
    B, L, D = x.shape
    silu_c = jax.nn.silu(c)
    ks, vs, kis, rets, convs, hs = [], [], [], [], [], []
    for l in range(DEPTH):
        mod = (silu_c @ p['w_ada'][l] + p['b_ada'][l]).reshape(B, N_MOD, D)
        m = [mod[:, j][:, None, :] for j in range(N_MOD)]
        hh = modulate(rmsnorm(x, p['g_pre'][l, 0]), m[0], m[1])
        f = swiglu(hh, p['w_ffn_gate'][l, 0], p['w_ffn_up'][l, 0], p['w_ffn_down'][l, 0])
        x = x + 0.5 * m[2] * rmsnorm(f, p['g_post'][l, 0])
        hh = modulate(rmsnorm(x, p['g_pre'][l, 1]), m[3], m[4])
        if l % 2 == 0:
            y, (k_new, v_new, ki_new, s_new) = even_mixer(hh, pos, l // 2, past, p)
            ks.append(k_new)
            vs.append(v_new)
            kis.append(ki_new)
            rets.append(s_new)
        else:
            y, (buf_new, h_new) = odd_mixer(hh, pos, l // 2, past, p)
            convs.append(buf_new)
            hs.append(h_new)
        x = x + m[5] * rmsnorm(y, p['g_post'][l, 1])
        hh = modulate(rmsnorm(x, p['g_pre'][l, 2]), m[6], m[7])
        f = swiglu(hh, p['w_ffn_gate'][l, 1], p['w_ffn_up'][l, 1], p['w_ffn_down'][l, 1])
        x = x + 0.5 * m[8] * rmsnorm(f, p['g_post'][l, 2])
    return x, (jnp.stack(ks), jnp.stack(vs), jnp.stack(kis), jnp.stack(rets), jnp.stack(convs), jnp.stack(hs))


def setup_inputs(seed: int = 0) -> dict:
    key = jax.random.key(seed)
    it = iter(jax.random.split(key, 32))

    def nrm(shape, s):
        return jax.random.normal(next(it), shape, jnp.float32) * s

    n_pages = PAST_LEN // PAGE_SIZE
    n_phys = (DEC_BATCH * n_pages * 5) // 4
    x_prompt = nrm((BATCH, SEQ, D_MODEL), 1.0)
    x_sample = nrm((DEC_BATCH, DEC_SEQ, D_MODEL), 1.0)
    cache_k = nrm((N_EVEN, n_phys, PAGE_SIZE, A_KV_HEADS, A_HEAD_DIM), 1.0)
    cache_v = nrm((N_EVEN, n_phys, PAGE_SIZE, A_KV_HEADS, A_HEAD_DIM), 1.0)
    cache_kidx = nrm((N_EVEN, n_phys, PAGE_SIZE, IDX_DIM), 1.0)
    state_ret = nrm((N_EVEN, DEC_BATCH, RET_HEADS, RET_HEAD_DIM, RET_HEAD_DIM), 0.1)
    state_conv = nrm((N_ODD, DEC_BATCH, CONV_W - 1, D_RNN), 1.0)
    state_rglru = nrm((N_ODD, DEC_BATCH, D_RNN), 0.5)
    page_table = jax.random.permutation(next(it), n_phys)[: DEC_BATCH * n_pages].reshape(DEC_BATCH, n_pages).astype(jnp.int32)
    c_prompt = nrm((BATCH, D_MODEL), 1.0)
    c_sample = nrm((DEC_BATCH, D_MODEL), 1.0)
    w_ada = nrm((DEPTH, D_MODEL, N_MOD * D_MODEL), 0.5 * D_MODEL ** -0.5)
    b_ada = nrm((DEPTH, N_MOD * D_MODEL), 0.02)
    g_pre = 1.0 + nrm((DEPTH, 3, D_MODEL), 0.02)
    g_post = 1.0 + nrm((DEPTH, 3, D_MODEL), 0.02)
    w_ffn_gate = nrm((DEPTH, 2, D_MODEL, D_FF), D_MODEL ** -0.5)
    w_ffn_up = nrm((DEPTH, 2, D_MODEL, D_FF), D_MODEL ** -0.5)
    w_ffn_down = nrm((DEPTH, 2, D_FF, D_MODEL), D_FF ** -0.5)
    w_in_even = nrm((N_EVEN, D_MODEL, P_EVEN), D_MODEL ** -0.5)
    w_out_even = nrm((N_EVEN, A_WIDTH + RET_WIDTH, D_MODEL), (A_WIDTH + RET_WIDTH) ** -0.5)
    w_in_odd = nrm((N_ODD, D_MODEL, 2 * D_RNN), D_MODEL ** -0.5)
    w_out_odd = nrm((N_ODD, D_RNN, D_MODEL), D_RNN ** -0.5)
    conv_w = nrm((N_ODD, CONV_W, D_RNN), 0.5)
    conv_b = nrm((N_ODD, D_RNN), 0.02)
    w_rg_a = nrm((N_ODD, RG_BLOCKS, RG_BLOCK_W, RG_BLOCK_W), RG_BLOCK_W ** -0.5)
    b_rg_a = nrm((N_ODD, D_RNN), 0.02)
    w_rg_x = nrm((N_ODD, RG_BLOCKS, RG_BLOCK_W, RG_BLOCK_W), RG_BLOCK_W ** -0.5)
    b_rg_x = nrm((N_ODD, D_RNN), 0.02)
    a0 = jax.random.uniform(next(it), (N_ODD, D_RNN), jnp.float32, minval=0.9, maxval=0.999)
    rg_lambda = jnp.log(a0) - jnp.log1p(-a0)
    return {'x_prompt': x_prompt, 'x_sample': x_sample, 'cache_k': cache_k, 'cache_v': cache_v,
            'cache_kidx': cache_kidx, 'state_ret': state_ret, 'state_conv': state_conv, 'state_rglru': state_rglru,
            'page_table': page_table, 'c_prompt': c_prompt, 'c_sample': c_sample, 'w_ada': w_ada, 'b_ada': b_ada,
            'g_pre': g_pre, 'g_post': g_post, 'w_ffn_gate': w_ffn_gate, 'w_ffn_up': w_ffn_up, 'w_ffn_down': w_ffn_down,
            'w_in_even': w_in_even, 'w_out_even': w_out_even, 'w_in_odd': w_in_odd, 'w_out_odd': w_out_odd,
            'conv_w': conv_w, 'conv_b': conv_b, 'w_rg_a': w_rg_a, 'b_rg_a': b_rg_a, 'w_rg_x': w_rg_x, 'b_rg_x': b_rg_x,
            'rg_lambda': rg_lambda}


def reference(x_prompt, x_sample, cache_k, cache_v, cache_kidx, state_ret, state_conv, state_rglru, page_table,
              c_prompt, c_sample, w_ada, b_ada, g_pre, g_post, w_ffn_gate, w_ffn_up, w_ffn_down,
              w_in_even, w_out_even, w_in_odd, w_out_odd, conv_w, conv_b, w_rg_a, b_rg_a, w_rg_x, b_rg_x, rg_lambda):
    p = {'w_ada': w_ada, 'b_ada': b_ada, 'g_pre': g_pre, 'g_post': g_post, 'w_ffn_gate': w_ffn_gate,
         'w_ffn_up': w_ffn_up, 'w_ffn_down': w_ffn_down, 'w_in_even': w_in_even, 'w_out_even': w_out_even,
         'w_in_odd': w_in_odd, 'w_out_odd': w_out_odd, 'conv_w': conv_w, 'conv_b': conv_b, 'w_rg_a': w_rg_a,
         'b_rg_a': b_rg_a, 'w_rg_x': w_rg_x, 'b_rg_x': b_rg_x, 'rg_lambda': rg_lambda}
    past_len = page_table.shape[1] * cache_k.shape[2]
    pos_prompt = jnp.arange(x_prompt.shape[1], dtype=jnp.int32)
    pos_sample = past_len + jnp.arange(x_sample.shape[1], dtype=jnp.int32)
    past = {'cache_k': cache_k, 'cache_v': cache_v, 'cache_kidx': cache_kidx, 'page_table': page_table,
            'state_ret': state_ret, 'state_conv': state_conv, 'state_rglru': state_rglru}
    y_prompt, st_p = trunk(x_prompt, c_prompt, pos_prompt, None, p)
    y_sample, st_s = trunk(x_sample, c_sample, pos_sample, past, p)
    k_p, v_p, ki_p, ret_p, conv_p, h_p = st_p
    k_s, v_s, ki_s, ret_s, conv_s, h_s = st_s
    return (y_prompt, y_sample, k_p, v_p, ki_p, ret_p, conv_p, h_p, k_s, v_s, ki_s, ret_s, conv_s, h_s)
```

```python
import functools
import math

import jax
import jax.numpy as jnp
import numpy as np
from jax import lax
from jax.experimental import pallas as pl
from jax.experimental.pallas import tpu as pltpu

F32 = jnp.float32
BF16 = jnp.bfloat16

D = 1024
DEPTH = 4
N_MOD = 9
A_HEADS = 8
A_KV = 2
A_DH = 64
ROT_DIM = 16
ROPE_THETA = 500000.0
IDX_HEADS = 4
IDX_DIM = 64
TOPK_MAX = 256
RET_HEADS = 4
RET_DH = 128
RET_THETA = 10000.0
RG_BLOCKS = 8
RG_BW = 128
CONV_W = 4
RG_C = 8.0
D_FF = 2816
EPS = 1e-6
P_IN = 3200
PAGE = 128

LANES = 128
SUBLANES = 8
VMEM_LIMIT = 56 * 1024 * 1024

NEG_INF = float("-inf")
INT_MIN = -(2 ** 31)


def _cparams(n_axes, vmem=VMEM_LIMIT):
    return pltpu.CompilerParams(dimension_semantics=("arbitrary",) * n_axes, vmem_limit_bytes=vmem)


def _dot(a, b):
    return jnp.dot(a, b, preferred_element_type=F32)


def _dot_nt(a, b):
    return lax.dot_general(a, b, (((1,), (1,)), ((), ())), preferred_element_type=F32)


def _dot_tn(a, b):
    return lax.dot_general(a, b, (((0,), (0,)), ((), ())), preferred_element_type=F32)


def _sigmoid(x):
    return 1.0 / (1.0 + jnp.exp(-x))


def _silu(x):
    return x * _sigmoid(x)


def _gelu_tanh(x):
    return 0.5 * x * (1.0 + jnp.tanh(math.sqrt(2.0 / math.pi) * (x + 0.044715 * (x * x * x))))


def _rms(x, g=None):
    y = x * lax.rsqrt(jnp.mean(x * x, axis=-1, keepdims=True) + EPS)
    return y if g is None else y * g


def _prenorm(x, g_pre, shift, scale, g, rt):
    h = _rms(x, g_pre)
    if g == 1:
        return h * (1.0 + scale[0]) + shift[0]
    h3 = h.reshape(g, rt, D)
    return (h3 * (1.0 + scale) + shift).reshape(g * rt, D)


def _gated(y, gate, g, rt):
    if g == 1:
        return y * gate[0]
    return (y.reshape(g, rt, D) * gate).reshape(g * rt, D)


class _Tiling:
    def __init__(self, G, R, g, rt):
        assert G % g == 0 and R % rt == 0 and (g == 1 or rt == R)
        self.G, self.R, self.g, self.rt = G, R, g, rt
        self.nr = R // rt
        self.rows = g * rt
        self.grid = (G // g, self.nr)
        self.n_tok = G * R

    def tok(self, width):
        nr = self.nr
        return pl.BlockSpec((self.rows, width), lambda i, j: (i * nr + j, 0))

    def mod(self, c):
        return pl.BlockSpec((self.g, 1, D), lambda i, j: (i, 0, c))

    def const(self, shape):
        nd = len(shape)
        return pl.BlockSpec(shape, lambda i, j: (0,) * nd, pipeline_mode=pl.Buffered(1))


def _ada_body(c_ref, w_ref, b_ref, o_ref):
    c = _silu(c_ref[...]).astype(BF16)
    o_ref[0] = _dot(c, w_ref[0].astype(BF16)) + b_ref[0]


def _ada(c_all, w_ada, b_ada):
    n = c_all.shape[0]
    nt = N_MOD
    return pl.pallas_call(
        _ada_body,
        out_shape=jax.ShapeDtypeStruct((DEPTH, n, N_MOD * D), F32),
        grid=(DEPTH, nt),
        in_specs=[pl.BlockSpec((n, D), lambda l, j: (0, 0)),
                  pl.BlockSpec((1, D, D), lambda l, j: (l, 0, j)),
                  pl.BlockSpec((1, 1, D), lambda l, j: (l, 0, j))],
        out_specs=pl.BlockSpec((1, n, D), lambda l, j: (l, 0, j)),
        compiler_params=_cparams(2),
        name="ada",
    )(c_all, w_ada, b_ada.reshape(DEPTH, 1, N_MOD * D))


def _ffn_body(x_ref, sh_ref, sc_ref, gt_ref, gpre_ref, gpost_ref, wg_ref, wu_ref, wd_ref, o_ref, *, g, rt):
    x = x_ref[...]
    h = _prenorm(x, gpre_ref[...], sh_ref[...], sc_ref[...], g, rt).astype(BF16)
    a = _dot(h, wg_ref[...])
    u = _dot(h, wu_ref[...])
    act = (_silu(a) * u).astype(BF16)
    f = _dot(act, wd_ref[...])
    y = _rms(f, gpost_ref[...])
    o_ref[...] = x + 0.5 * _gated(y, gt_ref[...], g, rt)


def _ffn(t, x, mod, mcol, g_pre, g_post, wg, wu, wd):
    return pl.pallas_call(
        functools.partial(_ffn_body, g=t.g, rt=t.rt),
        out_shape=jax.ShapeDtypeStruct((t.n_tok, D), F32),
        grid=t.grid,
        in_specs=[t.tok(D), t.mod(mcol), t.mod(mcol + 1), t.mod(mcol + 2),
                  t.const((1, D)), t.const((1, D)),
                  t.const((D, D_FF)), t.const((D, D_FF)), t.const((D_FF, D))],
        out_specs=t.tok(D),
        compiler_params=_cparams(2),
        name="ffn",
    )(x, mod, mod, mod, g_pre, g_post, wg, wu, wd)


def _even_in_body(x_ref, sh_ref, sc_ref, gpre_ref, w_ref, ra_ref, rr_ref,
                  q_ref, k_ref, v_ref, qi_ref, kw_ref, rq_ref, rk_ref, rv_ref, rg_ref, *, g, rt):
    h = _prenorm(x_ref[...], gpre_ref[...], sh_ref[...], sc_ref[...], g, rt).astype(BF16)
    proj = _dot(h, w_ref[...])

    def sec(c):
        return proj[:, c * LANES:(c + 1) * LANES]

    def rope_a(t, v):
        return (t * ra_ref[3 * v] + pltpu.roll(t, LANES - ROT_DIM // 2, 1) * ra_ref[3 * v + 1]
                + pltpu.roll(t, ROT_DIM // 2, 1) * ra_ref[3 * v + 2])

    def rope_r(t):
        return t * rr_ref[0] + pltpu.roll(t, RET_DH // 2, 1) * rr_ref[1]

    for c in range(4):
        q_ref[:, c * LANES:(c + 1) * LANES] = rope_a(sec(c), 0) * (A_DH ** -0.5)
    k_ref[...] = rope_a(sec(4), 0)
    v_ref[...] = sec(5)
    for c in range(2):
        qi_ref[:, c * LANES:(c + 1) * LANES] = rope_a(sec(6 + c), 0) * (IDX_DIM ** -0.5)
    kw_ref[...] = rope_a(sec(8), 1)
    for c in range(4):
        rq_ref[:, c * LANES:(c + 1) * LANES] = rope_r(sec(9 + c))
        rk_ref[:, c * LANES:(c + 1) * LANES] = rope_r(sec(13 + c) * (RET_DH ** -0.5))
        rv_ref[:, c * LANES:(c + 1) * LANES] = sec(17 + c)
        rg_ref[:, c * LANES:(c + 1) * LANES] = sec(21 + c)


def _even_in(t, x, mod, mcol, g_pre, w_in, rope_a, rope_r, rope_map):
    widths = (512, 128, 128, 256, 128, 512, 512, 512, 512)
    return pl.pallas_call(
        functools.partial(_even_in_body, g=t.g, rt=t.rt),
        out_shape=[jax.ShapeDtypeStruct((t.n_tok, w), F32) for w in widths],
        grid=t.grid,
        in_specs=[t.tok(D), t.mod(mcol), t.mod(mcol + 1), t.const((1, D)), t.const((D, P_IN)),
                  pl.BlockSpec((6, t.rows, LANES), rope_map),
                  pl.BlockSpec((2, t.rows, LANES), rope_map)],
        out_specs=[t.tok(w) for w in widths],
        compiler_params=_cparams(2),
        name="even_in",
    )(x, mod, mod, g_pre, w_in, rope_a, rope_r)


def _sortable(s):
    s = jnp.where(s == 0.0, 0.0, s)
    b = lax.bitcast_convert_type(s, jnp.int32)
    return b ^ ((b >> 31) & 0x7FFFFFFF)


def _topk_bias(scores, visible, k):
    rows, s_len = scores.shape
    key = _sortable(scores)

    def search(it, thr):
        cand = thr + jnp.left_shift(jnp.int32(1), 31 - it)
        cnt = jnp.sum(jnp.where(key >= cand, 1.0, 0.0), axis=-1, keepdims=True)
        return jnp.where(cnt >= k, cand, thr)

    thr = lax.fori_loop(0, 32, search, jnp.full((rows, 1), INT_MIN, jnp.int32))
    cnt_gt = jnp.sum(jnp.where(key > thr, 1.0, 0.0), axis=-1, keepdims=True)
    need = k - cnt_gt
    tri = jnp.where(lax.broadcasted_iota(jnp.int32, (LANES, LANES), 0)
                    <= lax.broadcasted_iota(jnp.int32, (LANES, LANES), 1), 1.0, 0.0).astype(BF16)
    off = jnp.zeros((rows, 1), F32)
    pieces = []
    for c in range(s_len // LANES):
        sl = slice(c * LANES, (c + 1) * LANES)
        kc = key[:, sl]
        eq = kc == thr
        pc = _dot(jnp.where(eq, 1.0, 0.0).astype(BF16), tri) + off
        off = pc[:, LANES - 1:LANES]
        tie_ok = jnp.where(eq, jnp.where(pc <= need, 0.0, NEG_INF), NEG_INF)
        b = jnp.where(kc > thr, 0.0, tie_ok)
        pieces.append(jnp.where(visible[:, sl], b, NEG_INF))
    return jnp.concatenate(pieces, axis=-1)


def _softmax_pv(logits, v):
    m = jnp.max(logits, axis=-1, keepdims=True)
    p = jnp.exp(logits - m)
    l = jnp.sum(p, axis=-1, keepdims=True)
    return _dot(p.astype(BF16), v) / l


def _dsa_p_body(q_ref, qi_ref, kwq_ref, k_ref, v_ref, kwk_ref, o_ref, *, tq, s_len, topk):
    i = pl.program_id(1)
    qi = qi_ref[...].astype(BF16)
    ki = kwk_ref[:, 0:IDX_DIM].astype(BF16)
    wi = kwq_ref[:, IDX_DIM:LANES]
    sc = jnp.zeros((tq, s_len), F32)
    for h in range(IDX_HEADS):
        raw = _dot_nt(qi[:, h * IDX_DIM:(h + 1) * IDX_DIM], ki)
        sc = sc + jnp.maximum(raw, 0.0) * wi[:, h:h + 1]
    qpos = i * tq + lax.broadcasted_iota(jnp.int32, (tq, 1), 0)
    kpos = lax.broadcasted_iota(jnp.int32, (1, s_len), 1)
    visible = kpos <= qpos
    sc = jnp.where(visible, sc, NEG_INF)
    bias = _topk_bias(sc, visible, topk)
    q = q_ref[...].astype(BF16)
    k = k_ref[...].astype(BF16)
    v = v_ref[...].astype(BF16)
    rep = A_HEADS // A_KV
    for hp in range(A_HEADS // 2):
        outs = []
        for h in (2 * hp, 2 * hp + 1):
            gsl = slice((h // rep) * A_DH, (h // rep + 1) * A_DH)
            logits = _dot_nt(q[:, h * A_DH:(h + 1) * A_DH], k[:, gsl]) + bias
            outs.append(_softmax_pv(logits, v[:, gsl]))
        o_ref[:, hp * LANES:(hp + 1) * LANES] = jnp.concatenate(outs, axis=-1)


def _dsa_prompt(B, L, q, qi, kw, k, v, tq=256):
    nq = L // tq
    tok = lambda w: pl.BlockSpec((tq, w), lambda b, i: (b * nq + i, 0))
    seq = lambda w: pl.BlockSpec((L, w), lambda b, i: (b, 0))
    return pl.pallas_call(
        functools.partial(_dsa_p_body, tq=tq, s_len=L, topk=min(TOPK_MAX, L // 4)),
        out_shape=jax.ShapeDtypeStruct((B * L, A_HEADS * A_DH), F32),
        grid=(B, nq),
        in_specs=[tok(512), tok(256), tok(LANES), seq(LANES), seq(LANES), seq(LANES)],
        out_specs=tok(512),
        compiler_params=_cparams(2),
        name="dsa_prompt",
    )(q, qi, kw, k, v, kw)


def _dsa_s_scores_body(pt_ref, qi_ref, kw_ref, *rest, n_pages):
    pages = rest[:n_pages]
    o_ref = rest[n_pages]
    t = qi_ref.shape[0]
    qi = qi_ref[...]
    qs = jnp.concatenate([qi[:, h * IDX_DIM:(h + 1) * IDX_DIM] for h in range(IDX_HEADS)], axis=0).astype(BF16)
    kw = kw_ref[...]
    wi = kw[:, IDX_DIM:LANES]
    ki_new = jnp.concatenate([kw[:, 0:IDX_DIM], jnp.zeros((PAGE - t, IDX_DIM), F32)], axis=0)

    def score(ki):
        raw = jnp.maximum(_dot_nt(qs, ki.astype(BF16)), 0.0)
        sc = raw[0:t] * wi[:, 0:1]
        for h in range(1, IDX_HEADS):
            sc = sc + raw[h * t:(h + 1) * t] * wi[:, h:h + 1]
        return sc

    for p in range(n_pages):
        o_ref[:, p * PAGE:(p + 1) * PAGE] = score(pages[p][0])
    lane = lax.broadcasted_iota(jnp.int32, (t, PAGE), 1)
    row = lax.broadcasted_iota(jnp.int32, (t, PAGE), 0)
    o_ref[:, n_pages * PAGE:(n_pages + 1) * PAGE] = jnp.where(lane <= row, score(ki_new), NEG_INF)


def _page_specs(n_pages, width):
    return [pl.BlockSpec((1, PAGE, width), functools.partial(lambda b, pt, p: (pt[b, p], 0, 0), p=p))
            for p in range(n_pages)]


def _dsa_s_scores(Bs, T, page_table, qi, kw, kidx_pool):
    n_pages = page_table.shape[1]
    tok = lambda w: pl.BlockSpec((T, w), lambda b, pt: (b, 0))
    s_pad = (n_pages + 1) * PAGE
    return pl.pallas_call(
        functools.partial(_dsa_s_scores_body, n_pages=n_pages),
        out_shape=jax.ShapeDtypeStruct((Bs * T, s_pad), F32),
        grid_spec=pltpu.PrefetchScalarGridSpec(
            num_scalar_prefetch=1, grid=(Bs,),
            in_specs=[tok(256), tok(LANES)] + _page_specs(n_pages, IDX_DIM),
            out_specs=tok(s_pad)),
        compiler_params=_cparams(1),
        name="dsa_s_scores",
    )(page_table, qi, kw, *([kidx_pool] * n_pages))


def _dsa_s_select_body(sc_ref, o_ref, *, t, past, topk):
    rows, s_pad = sc_ref.shape
    tpos = jnp.bitwise_and(lax.broadcasted_iota(jnp.int32, (rows, 1), 0), t - 1)
    kpos = lax.broadcasted_iota(jnp.int32, (1, s_pad), 1)
    visible = kpos <= past + tpos
    o_ref[...] = _topk_bias(sc_ref[...], visible, topk)


def _dsa_s_select(n_rows, T, past, scores, rows=256):
    s_pad = scores.shape[1]
    spec = pl.BlockSpec((rows, s_pad), lambda i: (i, 0))
    return pl.pallas_call(
        functools.partial(_dsa_s_select_body, t=T, past=past, topk=min(TOPK_MAX, (past + T) // 4)),
        out_shape=jax.ShapeDtypeStruct((n_rows, s_pad), F32),
        grid=(n_rows // rows,),
        in_specs=[spec], out_specs=spec,
        compiler_params=_cparams(1),
        name="dsa_s_select",
    )(scores)


def _dsa_s_attn_body(pt_ref, q_ref, kn_ref, vn_ref, b_ref, *rest, n_pages):
    kp = rest[:n_pages]
    vp = rest[n_pages:2 * n_pages]
    o_ref = rest[2 * n_pages]
    t = q_ref.shape[0]
    rep = A_HEADS // A_KV
    pad = jnp.zeros((PAGE - t, LANES), F32)
    k = jnp.concatenate([r[0] for r in kp] + [kn_ref[...], pad], axis=0).astype(BF16)
    v = jnp.concatenate([r[0] for r in vp] + [vn_ref[...], pad], axis=0).astype(BF16)
    q = q_ref[...].astype(BF16)
    bias = jnp.concatenate([b_ref[...]] * rep, axis=0)
    outs = []
    for gi in range(A_KV):
        gsl = slice(gi * A_DH, (gi + 1) * A_DH)
        qg = jnp.concatenate([q[:, (gi * rep + r) * A_DH:(gi * rep + r + 1) * A_DH] for r in range(rep)], axis=0)
        og = _softmax_pv(_dot_nt(qg, k[:, gsl]) + bias, v[:, gsl])
        outs += [og[r * t:(r + 1) * t] for r in range(rep)]
    for hp in range(A_HEADS // 2):
        o_ref[:, hp * LANES:(hp + 1) * LANES] = jnp.concatenate(outs[2 * hp:2 * hp + 2], axis=-1)


def _dsa_s_attn(Bs, T, page_table, q, k_new, v_new, bias, k_pool, v_pool):
    n_pages = page_table.shape[1]
    tok = lambda w: pl.BlockSpec((T, w), lambda b, pt: (b, 0))
    return pl.pallas_call(
        functools.partial(_dsa_s_attn_body, n_pages=n_pages),
        out_shape=jax.ShapeDtypeStruct((Bs * T, A_HEADS * A_DH), F32),
        grid_spec=pltpu.PrefetchScalarGridSpec(
            num_scalar_prefetch=1, grid=(Bs,),
            in_specs=[tok(512), tok(LANES), tok(LANES), tok(bias.shape[1])]
            + _page_specs(n_pages, LANES) + _page_specs(n_pages, LANES),
            out_specs=tok(512)),
        compiler_params=_cparams(1),
        name="dsa_s_attn",
    )(page_table, q, k_new, v_new, bias, *([k_pool] * n_pages), *([v_pool] * n_pages))


def _ret_log_gamma(h):
    return math.log1p(-(2.0 ** (-5.0 - h)))


def _ret_chunk(q, k, v, rg, s, h, c):
    lg = _ret_log_gamma(h)
    ri = lax.broadcasted_iota(jnp.int32, (c, c), 0)
    ci = lax.broadcasted_iota(jnp.int32, (c, c), 1)
    diff = (ri - ci).astype(F32)
    decay = jnp.where(diff >= 0, jnp.exp(jnp.maximum(diff, 0.0) * lg), 0.0)
    idx = lax.broadcasted_iota(jnp.int32, (c, 1), 0).astype(F32)
    q_dec = jnp.exp((idx + 1.0) * lg)
    k_dec = jnp.exp((c - 1.0 - idx) * lg)
    c_dec = math.exp(c * lg)
    qb = q.astype(BF16)
    att = _dot_nt(qb, k.astype(BF16)) * decay
    inner = _dot(att.astype(BF16), v.astype(BF16))
    cross = _dot(qb, s.astype(BF16)) * q_dec
    s_new = s * c_dec + _dot_tn((k * k_dec).astype(BF16), v.astype(BF16))
    return _rms(inner + cross) * _silu(rg), s_new


def _ret_p_body(rq_ref, rk_ref, rv_ref, rg_ref, o_ref, so_ref, s_ref, *, c):
    j = pl.program_id(1)

    @pl.when(j == 0)
    def _():
        s_ref[...] = jnp.zeros_like(s_ref)

    for h in range(RET_HEADS):
        sl = slice(h * RET_DH, (h + 1) * RET_DH)
        o, s_new = _ret_chunk(rq_ref[:, sl], rk_ref[:, sl], rv_ref[:, sl], rg_ref[:, sl], s_ref[h], h, c)
        o_ref[:, sl] = o
        s_ref[h] = s_new
    so_ref[0] = s_ref[...]


def _ret_prompt(B, L, rq, rk, rv, rg, c=256):
    nc = L // c
    tok = pl.BlockSpec((c, 512), lambda b, j: (b * nc + j, 0))
    return pl.pallas_call(
        functools.partial(_ret_p_body, c=c),
        out_shape=[jax.ShapeDtypeStruct((B * L, 512), F32),
                   jax.ShapeDtypeStruct((B, RET_HEADS, RET_DH, RET_DH), F32)],
        grid=(B, nc),
        in_specs=[tok, tok, tok, tok],
        out_specs=[tok, pl.BlockSpec((1, RET_HEADS, RET_DH, RET_DH), lambda b, j: (b, 0, 0, 0))],
        scratch_shapes=[pltpu.VMEM((RET_HEADS, RET_DH, RET_DH), F32)],
        compiler_params=_cparams(2),
        name="ret_prompt",
    )(rq, rk, rv, rg)


def _ret_s_body(rq_ref, rk_ref, rv_ref, rg_ref, s0_ref, o_ref, so_ref, *, t):
    for h in range(RET_HEADS):
        sl = slice(h * RET_DH, (h + 1) * RET_DH)
        o, s_new = _ret_chunk(rq_ref[:, sl], rk_ref[:, sl], rv_ref[:, sl], rg_ref[:, sl], s0_ref[0, h], h, t)
        o_ref[:, sl] = o
        so_ref[0, h] = s_new


def _ret_sample(Bs, T, rq, rk, rv, rg, s0):
    tok = pl.BlockSpec((T, 512), lambda b: (b, 0))
    st = pl.BlockSpec((1, RET_HEADS, RET_DH, RET_DH), lambda b: (b, 0, 0, 0))
    return pl.pallas_call(
        functools.partial(_ret_s_body, t=T),
        out_shape=[jax.ShapeDtypeStruct((Bs * T, 512), F32),
                   jax.ShapeDtypeStruct((Bs, RET_HEADS, RET_DH, RET_DH), F32)],
        grid=(Bs,),
        in_specs=[tok, tok, tok, tok, st],
        out_specs=[tok, st],
        compiler_params=_cparams(1),
        name="ret_sample",
    )(rq, rk, rv, rg, s0)


def _even_out_body(x_ref, oa_ref, or_ref, gt_ref, gpost_ref, wa_ref, wr_ref, o_ref, *, g, rt):
    y = _dot(oa_ref[...].astype(BF16), wa_ref[...]) + _dot(or_ref[...].astype(BF16), wr_ref[...])
    o_ref[...] = x_ref[...] + _gated(_rms(y, gpost_ref[...]), gt_ref[...], g, rt)


def _even_out(t, x, o_a, o_r, mod, mcol, g_post, w_a, w_r):
    return pl.pallas_call(
        functools.partial(_even_out_body, g=t.g, rt=t.rt),
        out_shape=jax.ShapeDtypeStruct((t.n_tok, D), F32),
        grid=t.grid,
        in_specs=[t.tok(D), t.tok(512), t.tok(512), t.mod(mcol), t.const((1, D)),
                  t.const((512, D)), t.const((512, D))],
        out_specs=t.tok(D),
        compiler_params=_cparams(2),
        name="even_out",
    )(x, o_a, o_r, mod, g_post, w_a, w_r)


def _softplus(x):
    return jnp.maximum(x, 0.0) + jnp.log1p(jnp.exp(-jnp.abs(x)))


def _odd_body(*refs, g, rt, prompt):
    (x_ref, sh_ref, sc_ref, gt_ref, gpre_ref, gpost_ref, win_ref, cw_ref, cb_ref,
     wa_ref, ba_ref, wx_ref, bx_ref, lam_ref, wout_ref) = refs[:15]
    if prompt:
        o_ref, tail_out_ref, h_out_ref, tail_ref, hcar_ref = refs[15:]
    else:
        prev_ref, h0_ref, o_ref, xb_out_ref, hs_out_ref = refs[15:]
    rows = g * rt
    x = x_ref[...]
    h = _prenorm(x, gpre_ref[...], sh_ref[...], sc_ref[...], g, rt).astype(BF16)
    proj = _dot(h, win_ref[...])
    gate_br = proj[:, :D]
    xb = proj[:, D:]
    row = lax.broadcasted_iota(jnp.int32, (rows, 1), 0)
    cw = cw_ref[...]

    if prompt:
        j = pl.program_id(1)

        @pl.when(j == 0)
        def _():
            tail_ref[...] = jnp.zeros_like(tail_ref)
            hcar_ref[...] = jnp.zeros_like(hcar_ref)

        pos = row
        xext = jnp.concatenate([tail_ref[...], xb], axis=0)
        conv = xb * cw[CONV_W - 1:CONV_W] + cb_ref[...]
        for jj in range(CONV_W - 1):
            d = CONV_W - 1 - jj
            conv = conv + pltpu.roll(xext, d, 0)[SUBLANES:] * cw[jj:jj + 1]
        tail_ref[...] = xb[rows - SUBLANES:]
        tail_out_ref[0] = xb[rows - SUBLANES:]
    else:
        pos = jnp.bitwise_and(row, rt - 1)
        prev = prev_ref[...]
        conv = xb * cw[CONV_W - 1:CONV_W] + cb_ref[...]
        for jj in range(CONV_W - 1):
            d = CONV_W - 1 - jj
            tap = jnp.where(pos >= d, pltpu.roll(xb, d, 0), pltpu.roll(prev, rows - (SUBLANES - d), 0))
            conv = conv + tap * cw[jj:jj + 1]
        xb_out_ref[...] = xb

    convb = conv.astype(BF16)
    ra = jnp.concatenate([_dot(convb[:, n * RG_BW:(n + 1) * RG_BW], wa_ref[n]) for n in range(RG_BLOCKS)], axis=-1)
    rx = jnp.concatenate([_dot(convb[:, n * RG_BW:(n + 1) * RG_BW], wx_ref[n]) for n in range(RG_BLOCKS)], axis=-1)
    r = _sigmoid(ra + ba_ref[...])
    ig = _sigmoid(rx + bx_ref[...])
    log_a = (-RG_C) * r * _softplus(-lam_ref[...])
    a = jnp.exp(log_a)
    mult = jnp.sqrt(1.0 - jnp.exp(2.0 * log_a))
    if prompt:
        mult = jnp.where(jnp.logical_and(j == 0, row == 0), 1.0, mult)
    b = mult * (ig * conv)
    if prompt:
        b = b + jnp.where(row == 0, a * hcar_ref[0:1], 0.0)
    else:
        b = b + a * h0_ref[...]

    d = 1
    while d < rt:
        keep = pos >= d
        a_sh = jnp.where(keep, pltpu.roll(a, d, 0), 1.0)
        b_sh = jnp.where(keep, pltpu.roll(b, d, 0), 0.0)
        b = b + a * b_sh
        a = a * a_sh
        d *= 2
    hs = b

    if prompt:
        hcar_ref[0:1] = hs[rows - 1:rows]
        h_out_ref[0] = hs[rows - SUBLANES:]
    else:
        hs_out_ref[...] = hs
    y = (hs * _gelu_tanh(gate_br)).astype(BF16)
    out = _dot(y, wout_ref[...])
    o_ref[...] = x + _gated(_rms(out, gpost_ref[...]), gt_ref[...], g, rt)


def _odd(t, prompt, x, mod, mcol, g_pre, g_post, w_in, cw, cb, wa, ba, wx, bx, lam, w_out, prev=None, h0=None):
    in_specs = [t.tok(D), t.mod(mcol), t.mod(mcol + 1), t.mod(mcol + 2), t.const((1, D)), t.const((1, D)),
                t.const((D, 2 * D)), t.const((CONV_W, D)), t.const((1, D)),
                t.const((RG_BLOCKS, RG_BW, RG_BW)), t.const((1, D)),
                t.const((RG_BLOCKS, RG_BW, RG_BW)), t.const((1, D)), t.const((1, D)), t.const((D, D))]
    args = [x, mod, mod, mod, g_pre, g_post, w_in, cw, cb, wa, ba, wx, bx, lam, w_out]
    if prompt:
        last = pl.BlockSpec((1, SUBLANES, D), lambda i, j: (i, 0, 0))
        out_shape = [jax.ShapeDtypeStruct((t.n_tok, D), F32),
                     jax.ShapeDtypeStruct((t.G, SUBLANES, D), F32),
                     jax.ShapeDtypeStruct((t.G, SUBLANES, D), F32)]
        out_specs = [t.tok(D), last, last]
        scratch = [pltpu.VMEM((SUBLANES, D), F32), pltpu.VMEM((SUBLANES, D), F32)]
    else:
        in_specs += [t.tok(D), t.tok(D)]
        args += [prev, h0]
        out_shape = [jax.ShapeDtypeStruct((t.n_tok, D), F32)] * 3
        out_specs = [t.tok(D)] * 3
        scratch = []
    return pl.pallas_call(
        functools.partial(_odd_body, g=t.g, rt=t.rt, prompt=prompt),
        out_shape=out_shape, grid=t.grid, in_specs=in_specs, out_specs=out_specs,
        scratch_shapes=scratch, compiler_params=_cparams(2),
        name="odd_prompt" if prompt else "odd_sample",
    )(*args)


def _rope_tables(pos):
    posf = pos.astype(F32)[:, None]
    inv_a = jnp.power(jnp.float32(ROPE_THETA), -jnp.arange(0, ROT_DIM, 2, dtype=F32) / ROT_DIM)
    ang = posf * inv_a[None, :]
    cos, sin = jnp.cos(ang), jnp.sin(ang)
    n = pos.shape[0]
    half = ROT_DIM // 2
    rest = A_DH - ROT_DIM
    one, zero = jnp.ones((n, rest), F32), jnp.zeros((n, rest), F32)
    zh = jnp.zeros((n, half), F32)
    cos_h = jnp.concatenate([cos, cos, one], axis=1)
    s1_h = jnp.concatenate([-sin, zh, zero], axis=1)
    s2_h = jnp.concatenate([zh, sin, zero], axis=1)
    both = lambda a: jnp.concatenate([a, a], axis=1)
    wi_scale = jnp.full((n, A_DH), IDX_HEADS ** -0.5, F32)
    z64 = jnp.zeros((n, A_DH), F32)
    rope_a = jnp.stack([both(cos_h), both(s1_h), both(s2_h),
                        jnp.concatenate([cos_h, wi_scale], axis=1),
                        jnp.concatenate([s1_h, z64], axis=1),
                        jnp.concatenate([s2_h, z64], axis=1)])
    inv_r = jnp.power(jnp.float32(RET_THETA), -jnp.linspace(0.0, 1.0, RET_DH // 2, dtype=F32))
    ang_r = posf * inv_r[None, :]
    cr, sr = jnp.cos(ang_r), jnp.sin(ang_r)
    rope_r = jnp.stack([jnp.concatenate([cr, cr], axis=1), jnp.concatenate([-sr, sr], axis=1)])
    return rope_a, rope_r


def _pack_w_in_even(w):
    o = np.cumsum((0, 512, 128, 128, 256, IDX_HEADS, IDX_DIM, 512, 512, 512, 512))
    pad = jnp.zeros((D, LANES - IDX_DIM - IDX_HEADS), w.dtype)
    return jnp.concatenate([w[:, o[0]:o[4]], w[:, o[5]:o[6]], w[:, o[4]:o[5]], pad, w[:, o[6]:o[10]]],
                           axis=1).astype(BF16)


def kernel(x_prompt, x_sample, cache_k, cache_v, cache_kidx, state_ret, state_conv, state_rglru, page_table,
           c_prompt, c_sample, w_ada, b_ada, g_pre, g_post, w_ffn_gate, w_ffn_up, w_ffn_down,
           w_in_even, w_out_even, w_in_odd, w_out_odd, conv_w, conv_b, w_rg_a, b_rg_a, w_rg_x, b_rg_x, rg_lambda):
    B, L, _ = x_prompt.shape
    Bs, T, _ = x_sample.shape
    n_pages = page_table.shape[1]
    past = n_pages * cache_k.shape[2]
    n_phys = cache_k.shape[1]
    assert cache_k.shape[2] == PAGE and T == SUBLANES

    tp = _Tiling(B, L, 1, 512)
    ts = _Tiling(Bs, T, 64, T)
    tp_odd = _Tiling(B, L, 1, 256)
    ts_odd = _Tiling(Bs, T, 32, T)

    mod_all = _ada(jnp.concatenate([c_prompt, c_sample], axis=0), w_ada, b_ada)
    mod_p = mod_all[:, :B].reshape(DEPTH, B, 1, N_MOD * D)
    mod_s = mod_all[:, B:].reshape(DEPTH, Bs, 1, N_MOD * D)

    rope_a_p, rope_r_p = _rope_tables(jnp.arange(L, dtype=jnp.int32))
    rope_a_s, rope_r_s = _rope_tables(jnp.tile(past + jnp.arange(T, dtype=jnp.int32), ts.g))
    rope_map_p = lambda i, j: (0, j, 0)
    rope_map_s = lambda i, j: (0, 0, 0)

    xp = x_prompt.reshape(B * L, D)
    xs = x_sample.reshape(Bs * T, D)
    row = lambda a: a.reshape(1, -1)
    ks, vs, kis, rets, convs, hs = ([[], []] for _ in range(6))

    for l in range(DEPTH):
        e = l // 2
        wg = w_ffn_gate[l].astype(BF16)
        wu = w_ffn_up[l].astype(BF16)
        wd = w_ffn_down[l].astype(BF16)
        groups = ((0, tp, xp, mod_p[l]), (1, ts, xs, mod_s[l]))
        new_x = []
        if l % 2 == 0:
            w_in = _pack_w_in_even(w_in_even[e])
            w_oa = w_out_even[e, :512].astype(BF16)
            w_or = w_out_even[e, 512:].astype(BF16)
        else:
            w_in = w_in_odd[e].astype(BF16)
            w_out = w_out_odd[e].astype(BF16)
            wa = w_rg_a[e].astype(BF16)
            wx = w_rg_x[e].astype(BF16)
        for gi, t, x, mod in groups:
            x = _ffn(t, x, mod, 0, row(g_pre[l, 0]), row(g_post[l, 0]), wg[0], wu[0], wd[0])
            if l % 2 == 0:
                ra, rr, rmap = (rope_a_p, rope_r_p, rope_map_p) if gi == 0 else (rope_a_s, rope_r_s, rope_map_s)
                q, k, v, qi, kw, rq, rk, rv, rg = _even_in(t, x, mod, 3, row(g_pre[l, 1]), w_in, ra, rr, rmap)
                if gi == 0:
                    o_a = _dsa_prompt(B, L, q, qi, kw, k, v)
                    o_r, s_new = _ret_prompt(B, L, rq, rk, rv, rg)
                    ks[0].append(k.reshape(B, L, A_KV, A_DH))
                    vs[0].append(v.reshape(B, L, A_KV, A_DH))
                    kis[0].append(kw[:, :IDX_DIM].reshape(B, L, IDX_DIM))
                else:
                    scores = _dsa_s_scores(Bs, T, page_table, qi, kw, cache_kidx[e])
                    bias = _dsa_s_select(Bs * T, T, past, scores)
                    o_a = _dsa_s_attn(Bs, T, page_table, q, k, v, bias,
                                      cache_k[e].reshape(n_phys, PAGE, LANES), cache_v[e].reshape(n_phys, PAGE, LANES))
                    o_r, s_new = _ret_sample(Bs, T, rq, rk, rv, rg, state_ret[e])
                    ks[1].append(k.reshape(Bs, T, A_KV, A_DH))
                    vs[1].append(v.reshape(Bs, T, A_KV, A_DH))
                    kis[1].append(kw[:, :IDX_DIM].reshape(Bs, T, IDX_DIM))
                rets[gi].append(s_new)
                x = _even_out(t, x, o_a, o_r, mod, 5, row(g_post[l, 1]), w_oa, w_or)
            else:
                common = (row(g_pre[l, 1]), row(g_post[l, 1]), w_in, conv_w[e], row(conv_b[e]), wa, row(b_rg_a[e]),
                          wx, row(b_rg_x[e]), row(rg_lambda[e]), w_out)
                if gi == 0:
                    x, tail, hlast = _odd(tp_odd, True, x, mod, 3, *common)
                    convs[0].append(tail[:, SUBLANES - (CONV_W - 1):])
                    hs[0].append(hlast[:, SUBLANES - 1])
                else:
                    prev = jnp.pad(state_conv[e], ((0, 0), (SUBLANES - (CONV_W - 1), 0), (0, 0))).reshape(Bs * T, D)
                    h0 = jnp.pad(state_rglru[e][:, None, :], ((0, 0), (0, T - 1), (0, 0))).reshape(Bs * T, D)
                    x, xb, hseq = _odd(ts_odd, False, x, mod, 3, *common, prev=prev, h0=h0)
                    convs[1].append(xb.reshape(Bs, T, D)[:, T - (CONV_W - 1):])
                    hs[1].append(hseq.reshape(Bs, T, D)[:, T - 1])
            x = _ffn(t, x, mod, 6, row(g_pre[l, 2]), row(g_post[l, 2]), wg[1], wu[1], wd[1])
            new_x.append(x)
        xp, xs = new_x

    st = lambda lists, gi: jnp.stack(lists[gi])
    return (xp.reshape(B, L, D), xs.reshape(Bs, T, D),
            st(ks, 0), st(vs, 0), st(kis, 0), st(rets, 0), st(convs, 0), st(hs, 0),
            st(ks, 1), st(vs, 1), st(kis, 1), st(rets, 1), st(convs, 1), st(hs, 1))
```

```python
import functools
import math

import jax
import jax.numpy as jnp
import numpy as np
from jax import lax
from jax.experimental import pallas as pl
from jax.experimental.pallas import tpu as pltpu

F32 = jnp.float32
BF16 = jnp.bfloat16

D = 1024
DEPTH = 4
N_MOD = 9
A_HEADS = 8
A_KV = 2
A_DH = 64
ROT_DIM = 16
ROPE_THETA = 500000.0
IDX_HEADS = 4
IDX_DIM = 64
TOPK_MAX = 256
RET_HEADS = 4
RET_DH = 128
RET_THETA = 10000.0
RG_BLOCKS = 8
RG_BW = 128
CONV_W = 4
RG_C = 8.0
D_FF = 2816
EPS = 1e-6
P_IN = 3200
PAGE = 128

LANES = 128
SUBLANES = 8
VMEM_LIMIT = 56 * 1024 * 1024

NEG_INF = float("-inf")
INT_MIN = -(2 ** 31)


def _cparams(n_axes, vmem=VMEM_LIMIT):
    return pltpu.CompilerParams(dimension_semantics=("arbitrary",) * n_axes, vmem_limit_bytes=vmem)


def _dot(a, b):
    return jnp.dot(a, b, preferred_element_type=F32)


def _dot_nt(a, b):
    return lax.dot_general(a, b, (((1,), (1,)), ((), ())), preferred_element_type=F32)


def _dot_tn(a, b):
    return lax.dot_general(a, b, (((0,), (0,)), ((), ())), preferred_element_type=F32)


def _sigmoid(x):
    return 1.0 / (1.0 + jnp.exp(-x))


def _silu(x):
    return x * _sigmoid(x)


def _gelu_tanh(x):
    return 0.5 * x * (1.0 + jnp.tanh(math.sqrt(2.0 / math.pi) * (x + 0.044715 * (x * x * x))))


def _rms(x, g=None):
    y = x * lax.rsqrt(jnp.mean(x * x, axis=-1, keepdims=True) + EPS)
    return y if g is None else y * g


def _prenorm(x, g_pre, shift, scale, g, rt):
    h = _rms(x, g_pre)
    if g == 1:
        return h * (1.0 + scale[0]) + shift[0]
    h3 = h.reshape(g, rt, D)
    return (h3 * (1.0 + scale) + shift).reshape(g * rt, D)


def _gated(y, gate, g, rt):
    if g == 1:
        return y * gate[0]
    return (y.reshape(g, rt, D) * gate).reshape(g * rt, D)


class _Tiling:
    def __init__(self, G, R, g, rt):
        assert G % g == 0 and R % rt == 0 and (g == 1 or rt == R)
        self.G, self.R, self.g, self.rt = G, R, g, rt
        self.nr = R // rt
        self.rows = g * rt
        self.grid = (G // g, self.nr)
        self.n_tok = G * R

    def tok(self, width):
        nr = self.nr
        return pl.BlockSpec((self.rows, width), lambda i, j: (i * nr + j, 0))

    def mod(self, c):
        return pl.BlockSpec((self.g, 1, D), lambda i, j: (i, 0, c))

    def sel(self, block, idx):
        return pl.BlockSpec(block, lambda i, j: idx, pipeline_mode=pl.Buffered(1))


def _ada_body(c_ref, w_ref, b_ref, o_ref):
    c = _silu(c_ref[...]).astype(BF16)
    o_ref[0] = _dot(c, w_ref[0].astype(BF16)) + b_ref[0]


def _ada(c_all, w_ada, b_ada):
    n = c_all.shape[0]
    return pl.pallas_call(
        _ada_body,
        out_shape=jax.ShapeDtypeStruct((DEPTH, n, N_MOD * D), F32),
        grid=(DEPTH, N_MOD),
        in_specs=[pl.BlockSpec((n, D), lambda l, j: (0, 0)),
                  pl.BlockSpec((1, D, D), lambda l, j: (l, 0, j)),
                  pl.BlockSpec((1, 1, D), lambda l, j: (l, 0, j))],
        out_specs=pl.BlockSpec((1, n, D), lambda l, j: (l, 0, j)),
        compiler_params=_cparams(2),
        name="ada",
    )(c_all, w_ada, b_ada.reshape(DEPTH, 1, N_MOD * D))


def _ffn_body(x_ref, sh_ref, sc_ref, gt_ref, gpre_ref, gpost_ref, wg_ref, wu_ref, wd_ref, o_ref, *, g, rt):
    x = x_ref[...]
    h = _prenorm(x, gpre_ref[...], sh_ref[...], sc_ref[...], g, rt).astype(BF16)
    a = _dot(h, wg_ref[...])
    u = _dot(h, wu_ref[...])
    act = (_silu(a) * u).astype(BF16)
    f = _dot(act, wd_ref[...])
    y = _rms(f, gpost_ref[...])
    o_ref[...] = x + 0.5 * _gated(y, gt_ref[...], g, rt)


def _ffn(t, x, mod, mcol, l, s, n, p):
    return pl.pallas_call(
        functools.partial(_ffn_body, g=t.g, rt=t.rt),
        out_shape=jax.ShapeDtypeStruct((t.n_tok, D), F32),
        grid=t.grid,
        in_specs=[t.tok(D), t.mod(mcol), t.mod(mcol + 1), t.mod(mcol + 2),
                  t.sel((None, None, 1, D), (l, n, 0, 0)), t.sel((None, None, 1, D), (l, n, 0, 0)),
                  t.sel((None, None, D, D_FF), (l, s, 0, 0)), t.sel((None, None, D, D_FF), (l, s, 0, 0)),
                  t.sel((None, None, D_FF, D), (l, s, 0, 0))],
        out_specs=t.tok(D),
        compiler_params=_cparams(2),
        name="ffn",
    )(x, mod, mod, mod, p["g_pre"], p["g_post"], p["wg"], p["wu"], p["wd"])


def _even_in_body(*refs, g, rt, prompt):
    x_ref, sh_ref, sc_ref, gpre_ref, w_ref, ra_ref, rr_ref = refs[:7]
    if prompt:
        q_ref, kt_ref, v_ref, qi_ref, kw_ref, kit_ref, rq_ref, rk_ref, rv_ref, rg_ref = refs[7:]
    else:
        q_ref, k_ref, v_ref, qi_ref, kw_ref, rq_ref, rk_ref, rv_ref, rg_ref = refs[7:]
    h = _prenorm(x_ref[...], gpre_ref[...], sh_ref[...], sc_ref[...], g, rt).astype(BF16)
    proj = _dot(h, w_ref[...])

    def sec(c):
        return proj[:, c * LANES:(c + 1) * LANES]

    def rope_a(t, v):
        return (t * ra_ref[3 * v] + pltpu.roll(t, LANES - ROT_DIM // 2, 1) * ra_ref[3 * v + 1]
                + pltpu.roll(t, ROT_DIM // 2, 1) * ra_ref[3 * v + 2])

    def rope_r(t):
        return t * rr_ref[0] + pltpu.roll(t, RET_DH // 2, 1) * rr_ref[1]

    for c in range(4):
        q_ref[:, c * LANES:(c + 1) * LANES] = rope_a(sec(c), 0) * (A_DH ** -0.5)
    k = rope_a(sec(4), 0)
    v_ref[...] = sec(5)
    for c in range(2):
        qi_ref[:, c * LANES:(c + 1) * LANES] = rope_a(sec(6 + c), 0) * (IDX_DIM ** -0.5)
    kw = rope_a(sec(8), 1)
    kw_ref[...] = kw
    if prompt:
        kt_ref[...] = k.T
        kit_ref[...] = kw.T[0:IDX_DIM]
    else:
        k_ref[...] = k
    for c in range(4):
        rq_ref[:, c * LANES:(c + 1) * LANES] = rope_r(sec(9 + c))
        rk_ref[:, c * LANES:(c + 1) * LANES] = rope_r(sec(13 + c) * (RET_DH ** -0.5))
        rv_ref[:, c * LANES:(c + 1) * LANES] = sec(17 + c)
        rg_ref[:, c * LANES:(c + 1) * LANES] = sec(21 + c)


def _even_in(t, prompt, x, mod, mcol, l, e, p, rope_a, rope_r, rope_map):
    tokw = lambda w: (jax.ShapeDtypeStruct((t.n_tok, w), F32), t.tok(w))
    if prompt:
        nr = t.nr
        seq_t = lambda w: (jax.ShapeDtypeStruct((t.G, w, t.R), F32),
                           pl.BlockSpec((None, w, t.rt), lambda i, j: (i, 0, j)))
        outs = [tokw(512), seq_t(LANES), tokw(LANES), tokw(256), tokw(LANES), seq_t(IDX_DIM)]
    else:
        outs = [tokw(512), tokw(LANES), tokw(LANES), tokw(256), tokw(LANES)]
    outs += [tokw(512)] * 4
    return pl.pallas_call(
        functools.partial(_even_in_body, g=t.g, rt=t.rt, prompt=prompt),
        out_shape=[o[0] for o in outs],
        grid=t.grid,
        in_specs=[t.tok(D), t.mod(mcol), t.mod(mcol + 1), t.sel((None, None, 1, D), (l, 1, 0, 0)),
                  t.sel((None, D, P_IN), (e, 0, 0)),
                  pl.BlockSpec((6, t.rows, LANES), rope_map),
                  pl.BlockSpec((2, t.rows, LANES), rope_map)],
        out_specs=[o[1] for o in outs],
        compiler_params=_cparams(2),
        name="even_in_p" if prompt else "even_in_s",
    )(x, mod, mod, p["g_pre"], p["w_in_even"], rope_a, rope_r)


def _sortable(s):
    s = jnp.where(s == 0.0, 0.0, s)
    b = lax.bitcast_convert_type(s, jnp.int32)
    return b ^ ((b >> 31) & 0x7FFFFFFF)


def _topk_bias(scores, visible, k):
    rows, s_len = scores.shape
    key = _sortable(scores)

    def search(it, thr):
        cand = thr + jnp.left_shift(jnp.int32(1), 31 - it)
        cnt = jnp.sum(jnp.where(key >= cand, 1.0, 0.0), axis=-1, keepdims=True)
        return jnp.where(cnt >= k, cand, thr)

    thr = lax.fori_loop(0, 32, search, jnp.full((rows, 1), INT_MIN, jnp.int32))
    cnt_gt = jnp.sum(jnp.where(key > thr, 1.0, 0.0), axis=-1, keepdims=True)
    need = k - cnt_gt
    tri = jnp.where(lax.broadcasted_iota(jnp.int32, (LANES, LANES), 0)
                    <= lax.broadcasted_iota(jnp.int32, (LANES, LANES), 1), 1.0, 0.0).astype(BF16)
    off = jnp.zeros((rows, 1), F32)
    pieces = []
    for c in range(s_len // LANES):
        sl = slice(c * LANES, (c + 1) * LANES)
        kc = key[:, sl]
        eq = kc == thr
        pc = _dot(jnp.where(eq, 1.0, 0.0).astype(BF16), tri) + off
        off = pc[:, LANES - 1:LANES]
        tie_ok = jnp.where(eq, jnp.where(pc <= need, 0.0, NEG_INF), NEG_INF)
        b = jnp.where(kc > thr, 0.0, tie_ok)
        pieces.append(jnp.where(visible[:, sl], b, NEG_INF))
    return jnp.concatenate(pieces, axis=-1)


def _softmax(logits):
    m = jnp.max(logits, axis=-1, keepdims=True)
    p = jnp.exp(logits - m)
    return p, jnp.sum(p, axis=-1, keepdims=True)


def _dsa_p_body(q_ref, qi_ref, kw_ref, kt_ref, kit_ref, v_ref, o_ref, *, tq, topk):
    s_len = q_ref.shape[0]
    rep = A_HEADS // A_KV
    kt = kt_ref[...].astype(BF16)
    kit = kit_ref[...].astype(BF16)
    v = v_ref[...].astype(BF16)
    for i in range(s_len // tq):
        rows = slice(i * tq, (i + 1) * tq)
        s_vis = (i + 1) * tq
        qi = qi_ref[rows, :].astype(BF16)
        wi = kw_ref[rows, IDX_DIM:LANES]
        sc = jnp.zeros((tq, s_vis), F32)
        for h in range(IDX_HEADS):
            raw = _dot(qi[:, h * IDX_DIM:(h + 1) * IDX_DIM], kit[:, :s_vis])
            sc = sc + jnp.maximum(raw, 0.0) * wi[:, h:h + 1]
        qpos = i * tq + lax.broadcasted_iota(jnp.int32, (tq, 1), 0)
        kpos = lax.broadcasted_iota(jnp.int32, (1, s_vis), 1)
        visible = kpos <= qpos
        bias = _topk_bias(jnp.where(visible, sc, NEG_INF), visible, topk)
        q = q_ref[rows, :].astype(BF16)
        for hp in range(A_HEADS // 2):
            outs = []
            for h in (2 * hp, 2 * hp + 1):
                gsl = slice((h // rep) * A_DH, (h // rep + 1) * A_DH)
                p, l = _softmax(_dot(q[:, h * A_DH:(h + 1) * A_DH], kt[gsl, :s_vis]) + bias)
                outs.append(_dot(p.astype(BF16), v[:s_vis, gsl]) / l)
            o_ref[rows, hp * LANES:(hp + 1) * LANES] = jnp.concatenate(outs, axis=-1)


def _dsa_prompt(B, L, q, qi, kw, kt, kit, v, tq=256):
    tok = lambda w: pl.BlockSpec((L, w), lambda b: (b, 0))
    seq_t = lambda w: pl.BlockSpec((None, w, L), lambda b: (b, 0, 0))
    return pl.pallas_call(
        functools.partial(_dsa_p_body, tq=tq, topk=min(TOPK_MAX, L // 4)),
        out_shape=jax.ShapeDtypeStruct((B * L, A_HEADS * A_DH), F32),
        grid=(B,),
        in_specs=[tok(512), tok(256), tok(LANES), seq_t(LANES), seq_t(IDX_DIM), tok(LANES)],
        out_specs=tok(512),
        compiler_params=_cparams(1),
        name="dsa_prompt",
    )(q, qi, kw, kt, kit, v)


def _page_specs(e, n_pages, rows):
    return [pl.BlockSpec((None, None, rows, PAGE), functools.partial(lambda b, pt, p: (e, pt[b, p], 0, 0), p=p))
            for p in range(n_pages)]


def _dsa_s_scores_body(pt_ref, qi_ref, kw_ref, *rest, n_pages):
    pages = rest[:n_pages]
    o_ref = rest[n_pages]
    t = qi_ref.shape[0]
    qi = qi_ref[...]
    qs = jnp.concatenate([qi[:, h * IDX_DIM:(h + 1) * IDX_DIM] for h in range(IDX_HEADS)], axis=0).astype(BF16)
    kw = kw_ref[...]
    wi = kw[:, IDX_DIM:LANES]
    ki_new = jnp.concatenate([kw[:, 0:IDX_DIM], jnp.zeros((PAGE - t, IDX_DIM), F32)], axis=0).astype(BF16)
    kit = jnp.concatenate([r[...] for r in pages], axis=-1).astype(BF16)

    def weigh(raw):
        raw = jnp.maximum(raw, 0.0)
        sc = raw[0:t] * wi[:, 0:1]
        for h in range(1, IDX_HEADS):
            sc = sc + raw[h * t:(h + 1) * t] * wi[:, h:h + 1]
        return sc

    past = n_pages * PAGE
    o_ref[:, :past] = weigh(_dot(qs, kit))
    lane = lax.broadcasted_iota(jnp.int32, (t, PAGE), 1)
    row = lax.broadcasted_iota(jnp.int32, (t, PAGE), 0)
    o_ref[:, past:] = jnp.where(lane <= row, weigh(_dot_nt(qs, ki_new)), NEG_INF)


def _dsa_s_scores(Bs, T, e, page_table, qi, kw, kidx_t):
    n_pages = page_table.shape[1]
    tok = lambda w: pl.BlockSpec((T, w), lambda b, pt: (b, 0))
    s_pad = (n_pages + 1) * PAGE
    return pl.pallas_call(
        functools.partial(_dsa_s_scores_body, n_pages=n_pages),
        out_shape=jax.ShapeDtypeStruct((Bs * T, s_pad), F32),
        grid_spec=pltpu.PrefetchScalarGridSpec(
            num_scalar_prefetch=1, grid=(Bs,),
            in_specs=[tok(256), tok(LANES)] + _page_specs(e, n_pages, IDX_DIM),
            out_specs=tok(s_pad)),
        compiler_params=_cparams(1),
        name="dsa_s_scores",
    )(page_table, qi, kw, *([kidx_t] * n_pages))


def _dsa_s_select_body(sc_ref, o_ref, *, t, past, topk):
    rows, s_pad = sc_ref.shape
    tpos = jnp.bitwise_and(lax.broadcasted_iota(jnp.int32, (rows, 1), 0), t - 1)
    kpos = lax.broadcasted_iota(jnp.int32, (1, s_pad), 1)
    visible = kpos <= past + tpos
    o_ref[...] = _topk_bias(sc_ref[...], visible, topk)


def _dsa_s_select(n_rows, T, past, scores, rows=256):
    s_pad = scores.shape[1]
    spec = pl.BlockSpec((rows, s_pad), lambda i: (i, 0))
    return pl.pallas_call(
        functools.partial(_dsa_s_select_body, t=T, past=past, topk=min(TOPK_MAX, (past + T) // 4)),
        out_shape=jax.ShapeDtypeStruct((n_rows, s_pad), F32),
        grid=(n_rows // rows,),
        in_specs=[spec], out_specs=spec,
        compiler_params=_cparams(1),
        name="dsa_s_select",
    )(scores)


def _dsa_s_attn_body(pt_ref, q_ref, kn_ref, vn_ref, b_ref, *rest, n_pages):
    kp = rest[:n_pages]
    vp = rest[n_pages:2 * n_pages]
    o_ref = rest[2 * n_pages]
    t = q_ref.shape[0]
    past = n_pages * PAGE
    rep = A_HEADS // A_KV
    pad = jnp.zeros((PAGE - t, LANES), F32)
    kt = jnp.concatenate([r[...] for r in kp], axis=-1).astype(BF16)
    vt = jnp.concatenate([r[...] for r in vp], axis=-1).astype(BF16)
    k_new = jnp.concatenate([kn_ref[...], pad], axis=0).astype(BF16)
    v_new = jnp.concatenate([vn_ref[...], pad], axis=0).astype(BF16)
    q = q_ref[...].astype(BF16)
    bias = jnp.concatenate([b_ref[...]] * rep, axis=0)
    outs = []
    for gi in range(A_KV):
        gsl = slice(gi * A_DH, (gi + 1) * A_DH)
        qg = jnp.concatenate([q[:, (gi * rep + r) * A_DH:(gi * rep + r + 1) * A_DH] for r in range(rep)], axis=0)
        logits = jnp.concatenate([_dot(qg, kt[gsl, :]), _dot_nt(qg, k_new[:, gsl])], axis=-1) + bias
        p, l = _softmax(logits)
        p = p.astype(BF16)
        og = (_dot_nt(p[:, :past], vt[gsl, :]) + _dot(p[:, past:], v_new[:, gsl])) / l
        outs += [og[r * t:(r + 1) * t] for r in range(rep)]
    for hp in range(A_HEADS // 2):
        o_ref[:, hp * LANES:(hp + 1) * LANES] = jnp.concatenate(outs[2 * hp:2 * hp + 2], axis=-1)


def _dsa_s_attn(Bs, T, e, page_table, q, k_new, v_new, bias, k_t, v_t):
    n_pages = page_table.shape[1]
    tok = lambda w: pl.BlockSpec((T, w), lambda b, pt: (b, 0))
    return pl.pallas_call(
        functools.partial(_dsa_s_attn_body, n_pages=n_pages),
        out_shape=jax.ShapeDtypeStruct((Bs * T, A_HEADS * A_DH), F32),
        grid_spec=pltpu.PrefetchScalarGridSpec(
            num_scalar_prefetch=1, grid=(Bs,),
            in_specs=[tok(512), tok(LANES), tok(LANES), tok(bias.shape[1])]
            + _page_specs(e, n_pages, LANES) + _page_specs(e, n_pages, LANES),
            out_specs=tok(512)),
        compiler_params=_cparams(1),
        name="dsa_s_attn",
    )(page_table, q, k_new, v_new, bias, *([k_t] * n_pages), *([v_t] * n_pages))


def _ret_log_gamma(h):
    return math.log1p(-(2.0 ** (-5.0 - h)))


def _ret_chunk(q, k, v, rg, s, h, c):
    lg = _ret_log_gamma(h)
    ri = lax.broadcasted_iota(jnp.int32, (c, c), 0)
    ci = lax.broadcasted_iota(jnp.int32, (c, c), 1)
    diff = (ri - ci).astype(F32)
    decay = jnp.where(diff >= 0, jnp.exp(jnp.maximum(diff, 0.0) * lg), 0.0)
    idx = lax.broadcasted_iota(jnp.int32, (c, 1), 0).astype(F32)
    q_dec = jnp.exp((idx + 1.0) * lg)
    k_dec = jnp.exp((c - 1.0 - idx) * lg)
    c_dec = math.exp(c * lg)
    qb = q.astype(BF16)
    att = _dot_nt(qb, k.astype(BF16)) * decay
    inner = _dot(att.astype(BF16), v.astype(BF16))
    cross = _dot(qb, s.astype(BF16)) * q_dec
    s_new = s * c_dec + _dot_tn((k * k_dec).astype(BF16), v.astype(BF16))
    return _rms(inner + cross) * _silu(rg), s_new


def _ret_p_body(rq_ref, rk_ref, rv_ref, rg_ref, o_ref, so_ref, s_ref, *, c):
    j = pl.program_id(1)

    @pl.when(j == 0)
    def _():
        s_ref[...] = jnp.zeros_like(s_ref)

    for h in range(RET_HEADS):
        sl = slice(h * RET_DH, (h + 1) * RET_DH)
        o, s_new = _ret_chunk(rq_ref[:, sl], rk_ref[:, sl], rv_ref[:, sl], rg_ref[:, sl], s_ref[h], h, c)
        o_ref[:, sl] = o
        s_ref[h] = s_new
    so_ref[0] = s_ref[...]


def _ret_prompt(B, L, rq, rk, rv, rg, c=256):
    nc = L // c
    tok = pl.BlockSpec((c, 512), lambda b, j: (b * nc + j, 0))
    return pl.pallas_call(
        functools.partial(_ret_p_body, c=c),
        out_shape=[jax.ShapeDtypeStruct((B * L, 512), F32),
                   jax.ShapeDtypeStruct((B, RET_HEADS, RET_DH, RET_DH), F32)],
        grid=(B, nc),
        in_specs=[tok, tok, tok, tok],
        out_specs=[tok, pl.BlockSpec((1, RET_HEADS, RET_DH, RET_DH), lambda b, j: (b, 0, 0, 0))],
        scratch_shapes=[pltpu.VMEM((RET_HEADS, RET_DH, RET_DH), F32)],
        compiler_params=_cparams(2),
        name="ret_prompt",
    )(rq, rk, rv, rg)


def _ret_s_body(rq_ref, rk_ref, rv_ref, rg_ref, s0_ref, o_ref, so_ref, *, t):
    for h in range(RET_HEADS):
        sl = slice(h * RET_DH, (h + 1) * RET_DH)
        o, s_new = _ret_chunk(rq_ref[:, sl], rk_ref[:, sl], rv_ref[:, sl], rg_ref[:, sl], s0_ref[h], h, t)
        o_ref[:, sl] = o
        so_ref[0, h] = s_new


def _ret_sample(Bs, T, e, rq, rk, rv, rg, state_ret):
    tok = pl.BlockSpec((T, 512), lambda b: (b, 0))
    return pl.pallas_call(
        functools.partial(_ret_s_body, t=T),
        out_shape=[jax.ShapeDtypeStruct((Bs * T, 512), F32),
                   jax.ShapeDtypeStruct((Bs, RET_HEADS, RET_DH, RET_DH), F32)],
        grid=(Bs,),
        in_specs=[tok, tok, tok, tok,
                  pl.BlockSpec((None, None, RET_HEADS, RET_DH, RET_DH), lambda b: (e, b, 0, 0, 0))],
        out_specs=[tok, pl.BlockSpec((1, RET_HEADS, RET_DH, RET_DH), lambda b: (b, 0, 0, 0))],
        compiler_params=_cparams(1),
        name="ret_sample",
    )(rq, rk, rv, rg, state_ret)


def _even_out_body(x_ref, oa_ref, or_ref, gt_ref, gpost_ref, wa_ref, wr_ref, o_ref, *, g, rt):
    y = _dot(oa_ref[...].astype(BF16), wa_ref[...]) + _dot(or_ref[...].astype(BF16), wr_ref[...])
    o_ref[...] = x_ref[...] + _gated(_rms(y, gpost_ref[...]), gt_ref[...], g, rt)


def _even_out(t, x, o_a, o_r, mod, mcol, l, e, p):
    half = A_HEADS * A_DH
    return pl.pallas_call(
        functools.partial(_even_out_body, g=t.g, rt=t.rt),
        out_shape=jax.ShapeDtypeStruct((t.n_tok, D), F32),
        grid=t.grid,
        in_specs=[t.tok(D), t.tok(half), t.tok(half), t.mod(mcol), t.sel((None, None, 1, D), (l, 1, 0, 0)),
                  t.sel((None, half, D), (e, 0, 0)), t.sel((None, half, D), (e, 1, 0))],
        out_specs=t.tok(D),
        compiler_params=_cparams(2),
        name="even_out",
    )(x, o_a, o_r, mod, p["g_post"], p["w_out_even"], p["w_out_even"])


def _softplus(x):
    return jnp.maximum(x, 0.0) + jnp.log1p(jnp.exp(-jnp.abs(x)))


def _odd_body(*refs, g, rt, prompt):
    (x_ref, sh_ref, sc_ref, gt_ref, gpre_ref, gpost_ref, win_ref, cw_ref, cb_ref,
     wa_ref, ba_ref, wx_ref, bx_ref, lam_ref, wout_ref) = refs[:15]
    if prompt:
        o_ref, tail_out_ref, h_out_ref, tail_ref, hcar_ref = refs[15:]
    else:
        prev_ref, h0_ref, o_ref, xb_out_ref, hs_out_ref = refs[15:]
    rows = g * rt
    x = x_ref[...]
    h = _prenorm(x, gpre_ref[...], sh_ref[...], sc_ref[...], g, rt).astype(BF16)
    proj = _dot(h, win_ref[...])
    gate_br = proj[:, :D]
    xb = proj[:, D:]
    row = lax.broadcasted_iota(jnp.int32, (rows, 1), 0)
    cw = cw_ref[...]

    if prompt:
        j = pl.program_id(1)

        @pl.when(j == 0)
        def _():
            tail_ref[...] = jnp.zeros_like(tail_ref)
            hcar_ref[...] = jnp.zeros_like(hcar_ref)

        pos = row
        xext = jnp.concatenate([tail_ref[...], xb], axis=0)
        conv = xb * cw[CONV_W - 1:CONV_W] + cb_ref[...]
        for jj in range(CONV_W - 1):
            d = CONV_W - 1 - jj
            conv = conv + pltpu.roll(xext, d, 0)[SUBLANES:] * cw[jj:jj + 1]
        tail_ref[...] = xb[rows - SUBLANES:]
        tail_out_ref[0] = xb[rows - SUBLANES:]
    else:
        pos = jnp.bitwise_and(row, rt - 1)
        prev = prev_ref[...]
        conv = xb * cw[CONV_W - 1:CONV_W] + cb_ref[...]
        for jj in range(CONV_W - 1):
            d = CONV_W - 1 - jj
            tap = jnp.where(pos >= d, pltpu.roll(xb, d, 0), pltpu.roll(prev, rows - (SUBLANES - d), 0))
            conv = conv + tap * cw[jj:jj + 1]
        xb_out_ref[...] = xb

    convb = conv.astype(BF16)
    ra = jnp.concatenate([_dot(convb[:, n * RG_BW:(n + 1) * RG_BW], wa_ref[n]) for n in range(RG_BLOCKS)], axis=-1)
    rx = jnp.concatenate([_dot(convb[:, n * RG_BW:(n + 1) * RG_BW], wx_ref[n]) for n in range(RG_BLOCKS)], axis=-1)
    r = _sigmoid(ra + ba_ref[...])
    ig = _sigmoid(rx + bx_ref[...])
    log_a = (-RG_C) * r * _softplus(-lam_ref[...])
    a = jnp.exp(log_a)
    mult = jnp.sqrt(1.0 - jnp.exp(2.0 * log_a))
    if prompt:
        mult = jnp.where(jnp.logical_and(j == 0, row == 0), 1.0, mult)
    b = mult * (ig * conv)
    if prompt:
        b = b + jnp.where(row == 0, a * hcar_ref[0:1], 0.0)
    else:
        b = b + a * h0_ref[...]

    d = 1
    while d < rt:
        keep = pos >= d
        a_sh = jnp.where(keep, pltpu.roll(a, d, 0), 1.0)
        b_sh = jnp.where(keep, pltpu.roll(b, d, 0), 0.0)
        b = b + a * b_sh
        a = a * a_sh
        d *= 2
    hs = b

    if prompt:
        hcar_ref[0:1] = hs[rows - 1:rows]
        h_out_ref[0] = hs[rows - SUBLANES:]
    else:
        hs_out_ref[...] = hs
    y = (hs * _gelu_tanh(gate_br)).astype(BF16)
    out = _dot(y, wout_ref[...])
    o_ref[...] = x + _gated(_rms(out, gpost_ref[...]), gt_ref[...], g, rt)


def _odd(t, prompt, x, mod, mcol, l, e, p, prev=None, h0=None):
    vec = t.sel((None, 1, D), (e, 0, 0))
    rgw = t.sel((None, RG_BLOCKS, RG_BW, RG_BW), (e, 0, 0, 0))
    in_specs = [t.tok(D), t.mod(mcol), t.mod(mcol + 1), t.mod(mcol + 2),
                t.sel((None, None, 1, D), (l, 1, 0, 0)), t.sel((None, None, 1, D), (l, 1, 0, 0)),
                t.sel((None, D, 2 * D), (e, 0, 0)), t.sel((None, CONV_W, D), (e, 0, 0)), vec,
                rgw, vec, rgw, vec, vec, t.sel((None, D, D), (e, 0, 0))]
    args = [x, mod, mod, mod, p["g_pre"], p["g_post"], p["w_in_odd"], p["conv_w"], p["conv_b"],
            p["w_rg_a"], p["b_rg_a"], p["w_rg_x"], p["b_rg_x"], p["rg_lambda"], p["w_out_odd"]]
    if prompt:
        last = pl.BlockSpec((1, SUBLANES, D), lambda i, j: (i, 0, 0))
        out_shape = [jax.ShapeDtypeStruct((t.n_tok, D), F32),
                     jax.ShapeDtypeStruct((t.G, SUBLANES, D), F32),
                     jax.ShapeDtypeStruct((t.G, SUBLANES, D), F32)]
        out_specs = [t.tok(D), last, last]
        scratch = [pltpu.VMEM((SUBLANES, D), F32), pltpu.VMEM((SUBLANES, D), F32)]
    else:
        in_specs += [t.tok(D), t.tok(D)]
        args += [prev, h0]
        out_shape = [jax.ShapeDtypeStruct((t.n_tok, D), F32)] * 3
        out_specs = [t.tok(D)] * 3
        scratch = []
    return pl.pallas_call(
        functools.partial(_odd_body, g=t.g, rt=t.rt, prompt=prompt),
        out_shape=out_shape, grid=t.grid, in_specs=in_specs, out_specs=out_specs,
        scratch_shapes=scratch, compiler_params=_cparams(2),
        name="odd_prompt" if prompt else "odd_sample",
    )(*args)


def _rope_tables(pos):
    posf = pos.astype(F32)[:, None]
    inv_a = jnp.power(jnp.float32(ROPE_THETA), -jnp.arange(0, ROT_DIM, 2, dtype=F32) / ROT_DIM)
    ang = posf * inv_a[None, :]
    cos, sin = jnp.cos(ang), jnp.sin(ang)
    n = pos.shape[0]
    half = ROT_DIM // 2
    rest = A_DH - ROT_DIM
    one, zero = jnp.ones((n, rest), F32), jnp.zeros((n, rest), F32)
    zh = jnp.zeros((n, half), F32)
    cos_h = jnp.concatenate([cos, cos, one], axis=1)
    s1_h = jnp.concatenate([-sin, zh, zero], axis=1)
    s2_h = jnp.concatenate([zh, sin, zero], axis=1)
    both = lambda a: jnp.concatenate([a, a], axis=1)
    wi_scale = jnp.full((n, A_DH), IDX_HEADS ** -0.5, F32)
    z64 = jnp.zeros((n, A_DH), F32)
    rope_a = jnp.stack([both(cos_h), both(s1_h), both(s2_h),
                        jnp.concatenate([cos_h, wi_scale], axis=1),
                        jnp.concatenate([s1_h, z64], axis=1),
                        jnp.concatenate([s2_h, z64], axis=1)])
    inv_r = jnp.power(jnp.float32(RET_THETA), -jnp.linspace(0.0, 1.0, RET_DH // 2, dtype=F32))
    ang_r = posf * inv_r[None, :]
    cr, sr = jnp.cos(ang_r), jnp.sin(ang_r)
    rope_r = jnp.stack([jnp.concatenate([cr, cr], axis=1), jnp.concatenate([-sr, sr], axis=1)])
    return rope_a, rope_r


def _pack_w_in_even(w):
    o = np.cumsum((0, 512, 128, 128, 256, IDX_HEADS, IDX_DIM, 512, 512, 512, 512))
    pad = jnp.zeros(w.shape[:2] + (LANES - IDX_DIM - IDX_HEADS,), w.dtype)
    return jnp.concatenate([w[..., o[0]:o[4]], w[..., o[5]:o[6]], w[..., o[4]:o[5]], pad, w[..., o[6]:o[10]]],
                           axis=-1).astype(BF16)


def kernel(x_prompt, x_sample, cache_k, cache_v, cache_kidx, state_ret, state_conv, state_rglru, page_table,
           c_prompt, c_sample, w_ada, b_ada, g_pre, g_post, w_ffn_gate, w_ffn_up, w_ffn_down,
           w_in_even, w_out_even, w_in_odd, w_out_odd, conv_w, conv_b, w_rg_a, b_rg_a, w_rg_x, b_rg_x, rg_lambda):
    B, L, _ = x_prompt.shape
    Bs, T, _ = x_sample.shape
    n_pages = page_table.shape[1]
    past = n_pages * cache_k.shape[2]
    n_phys = cache_k.shape[1]
    assert cache_k.shape[2] == PAGE and T == SUBLANES

    tp = _Tiling(B, L, 1, 512)
    ts = _Tiling(Bs, T, 64, T)
    tp_odd = _Tiling(B, L, 1, 256)
    ts_odd = _Tiling(Bs, T, 32, T)

    vec3 = lambda a: a.reshape(a.shape[0], 1, a.shape[1])
    p = {
        "g_pre": g_pre.reshape(DEPTH, 3, 1, D), "g_post": g_post.reshape(DEPTH, 3, 1, D),
        "wg": w_ffn_gate.astype(BF16), "wu": w_ffn_up.astype(BF16), "wd": w_ffn_down.astype(BF16),
        "w_in_even": _pack_w_in_even(w_in_even), "w_out_even": w_out_even.astype(BF16),
        "w_in_odd": w_in_odd.astype(BF16), "w_out_odd": w_out_odd.astype(BF16),
        "conv_w": conv_w, "conv_b": vec3(conv_b),
        "w_rg_a": w_rg_a.astype(BF16), "b_rg_a": vec3(b_rg_a),
        "w_rg_x": w_rg_x.astype(BF16), "b_rg_x": vec3(b_rg_x), "rg_lambda": vec3(rg_lambda),
    }
    k_t = jnp.transpose(cache_k, (0, 1, 3, 4, 2)).reshape(cache_k.shape[0], n_phys, LANES, PAGE)
    v_t = jnp.transpose(cache_v, (0, 1, 3, 4, 2)).reshape(cache_v.shape[0], n_phys, LANES, PAGE)
    kidx_t = jnp.transpose(cache_kidx, (0, 1, 3, 2))

    mod_all = _ada(jnp.concatenate([c_prompt, c_sample], axis=0), w_ada, b_ada)
    mod_p = mod_all[:, :B].reshape(DEPTH, B, 1, N_MOD * D)
    mod_s = mod_all[:, B:].reshape(DEPTH, Bs, 1, N_MOD * D)

    rope_a_p, rope_r_p = _rope_tables(jnp.arange(L, dtype=jnp.int32))
    rope_a_s, rope_r_s = _rope_tables(jnp.tile(past + jnp.arange(T, dtype=jnp.int32), ts.g))
    rope_map_p = lambda i, j: (0, j, 0)
    rope_map_s = lambda i, j: (0, 0, 0)

    xp = x_prompt.reshape(B * L, D)
    xs = x_sample.reshape(Bs * T, D)
    ks, vs, kis, rets, convs, hs = ([[], []] for _ in range(6))

    for l in range(DEPTH):
        e = l // 2
        groups = ((0, tp, xp, mod_p[l]), (1, ts, xs, mod_s[l]))
        new_x = []
        for gi, t, x, mod in groups:
            x = _ffn(t, x, mod, 0, l, 0, 0, p)
            if l % 2 == 0:
                if gi == 0:
                    q, kt, v, qi, kw, kit, rq, rk, rv, rg = _even_in(
                        t, True, x, mod, 3, l, e, p, rope_a_p, rope_r_p, rope_map_p)
                    o_a = _dsa_prompt(B, L, q, qi, kw, kt, kit, v)
                    o_r, s_new = _ret_prompt(B, L, rq, rk, rv, rg)
                    ks[0].append(jnp.transpose(kt.reshape(B, A_KV, A_DH, L), (0, 3, 1, 2)))
                    vs[0].append(v.reshape(B, L, A_KV, A_DH))
                    kis[0].append(jnp.transpose(kit, (0, 2, 1)))
                else:
                    q, k, v, qi, kw, rq, rk, rv, rg = _even_in(
                        t, False, x, mod, 3, l, e, p, rope_a_s, rope_r_s, rope_map_s)
                    scores = _dsa_s_scores(Bs, T, e, page_table, qi, kw, kidx_t)
                    bias = _dsa_s_select(Bs * T, T, past, scores)
                    o_a = _dsa_s_attn(Bs, T, e, page_table, q, k, v, bias, k_t, v_t)
                    o_r, s_new = _ret_sample(Bs, T, e, rq, rk, rv, rg, state_ret)
                    ks[1].append(k.reshape(Bs, T, A_KV, A_DH))
                    vs[1].append(v.reshape(Bs, T, A_KV, A_DH))
                    kis[1].append(kw[:, :IDX_DIM].reshape(Bs, T, IDX_DIM))
                rets[gi].append(s_new)
                x = _even_out(t, x, o_a, o_r, mod, 5, l, e, p)
            else:
                if gi == 0:
                    x, tail, hlast = _odd(tp_odd, True, x, mod, 3, l, e, p)
                    convs[0].append(tail[:, SUBLANES - (CONV_W - 1):])
                    hs[0].append(hlast[:, SUBLANES - 1])
                else:
                    prev = jnp.pad(state_conv[e], ((0, 0), (SUBLANES - (CONV_W - 1), 0), (0, 0))).reshape(Bs * T, D)
                    h0 = jnp.pad(state_rglru[e][:, None, :], ((0, 0), (0, T - 1), (0, 0))).reshape(Bs * T, D)
                    x, xb, hseq = _odd(ts_odd, False, x, mod, 3, l, e, p, prev=prev, h0=h0)
                    convs[1].append(xb.reshape(Bs, T, D)[:, T - (CONV_W - 1):])
                    hs[1].append(hseq.reshape(Bs, T, D)[:, T - 1])
            x = _ffn(t, x, mod, 6, l, 1, 2, p)
            new_x.append(x)
        xp, xs = new_x

    st = lambda lists, gi: jnp.stack(lists[gi])
    return (xp.reshape(B, L, D), xs.reshape(Bs, T, D),
            st(ks, 0), st(vs, 0), st(kis, 0), st(rets, 0), st(convs, 0), st(hs, 0),
            st(ks, 1), st(vs, 1), st(kis, 1), st(rets, 1), st(convs, 1), st(hs, 1))
```

```python
import functools
import math

import jax
import jax.numpy as jnp
import numpy as np
from jax import lax
from jax.experimental import pallas as pl
from jax.experimental.pallas import tpu as pltpu

F32 = jnp.float32
BF16 = jnp.bfloat16

D = 1024
DEPTH = 4
N_MOD = 9
A_HEADS = 8
A_KV = 2
A_DH = 64
ROT_DIM = 16
ROPE_THETA = 500000.0
IDX_HEADS = 4
IDX_DIM = 64
TOPK_MAX = 256
RET_HEADS = 4
RET_DH = 128
RET_THETA = 10000.0
RG_BLOCKS = 8
RG_BW = 128
CONV_W = 4
RG_C = 8.0
D_FF = 2816
EPS = 1e-6
P_IN = 3200
PAGE = 128

LANES = 128
SUBLANES = 8
VMEM_LIMIT = 56 * 1024 * 1024

NEG_INF = float("-inf")
INT_MIN = -(2 ** 31)


def _cparams(n_axes, vmem=VMEM_LIMIT):
    return pltpu.CompilerParams(dimension_semantics=("arbitrary",) * n_axes, vmem_limit_bytes=vmem)


def _dot(a, b):
    return jnp.dot(a, b, preferred_element_type=F32)


def _dot_nt(a, b):
    return lax.dot_general(a, b, (((1,), (1,)), ((), ())), preferred_element_type=F32)


def _dot_tn(a, b):
    return lax.dot_general(a, b, (((0,), (0,)), ((), ())), preferred_element_type=F32)


def _sigmoid(x):
    return 1.0 / (1.0 + jnp.exp(-x))


def _silu(x):
    return x * _sigmoid(x)


def _gelu_tanh(x):
    return 0.5 * x * (1.0 + jnp.tanh(math.sqrt(2.0 / math.pi) * (x + 0.044715 * (x * x * x))))


def _rms(x, g=None):
    y = x * lax.rsqrt(jnp.mean(x * x, axis=-1, keepdims=True) + EPS)
    return y if g is None else y * g


def _prenorm(x, g_pre, shift, scale, g, rt):
    h = _rms(x, g_pre)
    if g == 1:
        return h * (1.0 + scale[0]) + shift[0]
    h3 = h.reshape(g, rt, D)
    return (h3 * (1.0 + scale) + shift).reshape(g * rt, D)


def _gated(y, gate, g, rt):
    if g == 1:
        return y * gate[0]
    return (y.reshape(g, rt, D) * gate).reshape(g * rt, D)


class _Tiling:
    def __init__(self, G, R, g, rt):
        assert G % g == 0 and R % rt == 0 and (g == 1 or rt == R)
        self.G, self.R, self.g, self.rt = G, R, g, rt
        self.nr = R // rt
        self.rows = g * rt
        self.grid = (G // g, self.nr)
        self.n_tok = G * R

    def tok(self, width):
        nr = self.nr
        return pl.BlockSpec((self.rows, width), lambda i, j: (i * nr + j, 0))

    def mod(self, c):
        return pl.BlockSpec((self.g, 1, D), lambda i, j: (i, 0, c))

    def sel(self, block, idx):
        return pl.BlockSpec(block, lambda i, j: idx, pipeline_mode=pl.Buffered(1))


def _ada_body(c_ref, w_ref, b_ref, o_ref):
    c = _silu(c_ref[...]).astype(BF16)
    o_ref[0] = _dot(c, w_ref[0].astype(BF16)) + b_ref[0]


def _ada(c_all, w_ada, b_ada):
    n = c_all.shape[0]
    return pl.pallas_call(
        _ada_body,
        out_shape=jax.ShapeDtypeStruct((DEPTH, n, N_MOD * D), F32),
        grid=(DEPTH, N_MOD),
        in_specs=[pl.BlockSpec((n, D), lambda l, j: (0, 0)),
                  pl.BlockSpec((1, D, D), lambda l, j: (l, 0, j)),
                  pl.BlockSpec((1, 1, D), lambda l, j: (l, 0, j))],
        out_specs=pl.BlockSpec((1, n, D), lambda l, j: (l, 0, j)),
        compiler_params=_cparams(2),
        name="ada",
    )(c_all, w_ada, b_ada.reshape(DEPTH, 1, N_MOD * D))


def _ffn_body(x_ref, sh_ref, sc_ref, gt_ref, gpre_ref, gpost_ref, wg_ref, wu_ref, wd_ref, o_ref, *, g, rt):
    x = x_ref[...]
    h = _prenorm(x, gpre_ref[...], sh_ref[...], sc_ref[...], g, rt).astype(BF16)
    a = _dot(h, wg_ref[...])
    u = _dot(h, wu_ref[...])
    act = (_silu(a) * u).astype(BF16)
    f = _dot(act, wd_ref[...])
    y = _rms(f, gpost_ref[...])
    o_ref[...] = x + 0.5 * _gated(y, gt_ref[...], g, rt)


def _ffn(t, x, mod, mcol, l, s, n, p):
    return pl.pallas_call(
        functools.partial(_ffn_body, g=t.g, rt=t.rt),
        out_shape=jax.ShapeDtypeStruct((t.n_tok, D), F32),
        grid=t.grid,
        in_specs=[t.tok(D), t.mod(mcol), t.mod(mcol + 1), t.mod(mcol + 2),
                  t.sel((None, None, 1, D), (l, n, 0, 0)), t.sel((None, None, 1, D), (l, n, 0, 0)),
                  t.sel((None, None, D, D_FF), (l, s, 0, 0)), t.sel((None, None, D, D_FF), (l, s, 0, 0)),
                  t.sel((None, None, D_FF, D), (l, s, 0, 0))],
        out_specs=t.tok(D),
        compiler_params=_cparams(2),
        name="ffn",
    )(x, mod, mod, mod, p["g_pre"], p["g_post"], p["wg"], p["wu"], p["wd"])


def _even_in_body(*refs, g, rt, prompt):
    x_ref, sh_ref, sc_ref, gpre_ref, w_ref, ra_ref, rr_ref = refs[:7]
    if prompt:
        q_ref, kt_ref, v_ref, qi_ref, kw_ref, kit_ref, rq_ref, rk_ref, rv_ref, rg_ref = refs[7:]
    else:
        q_ref, k_ref, v_ref, qi_ref, kw_ref, rq_ref, rk_ref, rv_ref, rg_ref = refs[7:]
    h = _prenorm(x_ref[...], gpre_ref[...], sh_ref[...], sc_ref[...], g, rt).astype(BF16)
    proj = _dot(h, w_ref[...])

    def sec(c):
        return proj[:, c * LANES:(c + 1) * LANES]

    def rope_a(t, v):
        return (t * ra_ref[3 * v] + pltpu.roll(t, LANES - ROT_DIM // 2, 1) * ra_ref[3 * v + 1]
                + pltpu.roll(t, ROT_DIM // 2, 1) * ra_ref[3 * v + 2])

    def rope_r(t):
        return t * rr_ref[0] + pltpu.roll(t, RET_DH // 2, 1) * rr_ref[1]

    for c in range(4):
        q_ref[:, c * LANES:(c + 1) * LANES] = rope_a(sec(c), 0) * (A_DH ** -0.5)
    k = rope_a(sec(4), 0)
    v_ref[...] = sec(5)
    for c in range(2):
        qi_ref[:, c * LANES:(c + 1) * LANES] = rope_a(sec(6 + c), 0) * (IDX_DIM ** -0.5)
    kw = rope_a(sec(8), 1)
    kw_ref[...] = kw
    if prompt:
        kt_ref[...] = k.T
        kit_ref[...] = kw.T[0:IDX_DIM]
    else:
        k_ref[...] = k
    for c in range(4):
        rq_ref[:, c * LANES:(c + 1) * LANES] = rope_r(sec(9 + c))
        rk_ref[:, c * LANES:(c + 1) * LANES] = rope_r(sec(13 + c) * (RET_DH ** -0.5))
        rv_ref[:, c * LANES:(c + 1) * LANES] = sec(17 + c)
        rg_ref[:, c * LANES:(c + 1) * LANES] = sec(21 + c)


def _even_in(t, prompt, x, mod, mcol, l, e, p, rope_a, rope_r, rope_map):
    tokw = lambda w: (jax.ShapeDtypeStruct((t.n_tok, w), F32), t.tok(w))
    if prompt:
        nr = t.nr
        seq_t = lambda w: (jax.ShapeDtypeStruct((t.G, w, t.R), F32),
                           pl.BlockSpec((None, w, t.rt), lambda i, j: (i, 0, j)))
        outs = [tokw(512), seq_t(LANES), tokw(LANES), tokw(256), tokw(LANES), seq_t(IDX_DIM)]
    else:
        outs = [tokw(512), tokw(LANES), tokw(LANES), tokw(256), tokw(LANES)]
    outs += [tokw(512)] * 4
    return pl.pallas_call(
        functools.partial(_even_in_body, g=t.g, rt=t.rt, prompt=prompt),
        out_shape=[o[0] for o in outs],
        grid=t.grid,
        in_specs=[t.tok(D), t.mod(mcol), t.mod(mcol + 1), t.sel((None, None, 1, D), (l, 1, 0, 0)),
                  t.sel((None, D, P_IN), (e, 0, 0)),
                  pl.BlockSpec((6, t.rows, LANES), rope_map),
                  pl.BlockSpec((2, t.rows, LANES), rope_map)],
        out_specs=[o[1] for o in outs],
        compiler_params=_cparams(2),
        name="even_in_p" if prompt else "even_in_s",
    )(x, mod, mod, p["g_pre"], p["w_in_even"], rope_a, rope_r)


def _sortable(s):
    s = jnp.where(s == 0.0, 0.0, s)
    b = lax.bitcast_convert_type(s, jnp.int32)
    return b ^ ((b >> 31) & 0x7FFFFFFF)


def _topk_bias_t(scores_t, visible_t, k):
    s_len, n_q = scores_t.shape
    key = _sortable(scores_t)
    fold = 8 * SUBLANES
    assert s_len % fold == 0

    def count(mask):
        part = jnp.sum(jnp.where(mask, 1.0, 0.0).reshape(s_len // fold, fold, n_q), axis=0)
        return jnp.sum(part, axis=0, keepdims=True)

    def search(it, thr):
        cand = thr + jnp.left_shift(jnp.int32(1), 31 - it)
        return jnp.where(count(key >= cand) >= k, cand, thr)

    thr = lax.fori_loop(0, 32, search, jnp.full((1, n_q), INT_MIN, jnp.int32))
    ge = key >= thr
    cnt_ge = count(ge)

    def no_ties():
        return jnp.where(visible_t, jnp.where(ge, 0.0, NEG_INF), NEG_INF)

    def ties():
        need = k - count(key > thr)
        tri = jnp.where(lax.broadcasted_iota(jnp.int32, (LANES, LANES), 0)
                        >= lax.broadcasted_iota(jnp.int32, (LANES, LANES), 1), 1.0, 0.0).astype(BF16)
        off = jnp.zeros((1, n_q), F32)
        pieces = []
        for c in range(s_len // LANES):
            sl = slice(c * LANES, (c + 1) * LANES)
            kc = key[sl, :]
            eq = kc == thr
            pc = _dot(tri, jnp.where(eq, 1.0, 0.0).astype(BF16)) + off
            off = pc[LANES - 1:LANES, :]
            tie_ok = jnp.where(eq, jnp.where(pc <= need, 0.0, NEG_INF), NEG_INF)
            b = jnp.where(kc > thr, 0.0, tie_ok)
            pieces.append(jnp.where(visible_t[sl, :], b, NEG_INF))
        return jnp.concatenate(pieces, axis=0)

    return lax.cond(jnp.max(cnt_ge) > k, ties, no_ties)


def _softmax(logits):
    m = jnp.max(logits, axis=-1, keepdims=True)
    p = jnp.exp(logits - m)
    return p, jnp.sum(p, axis=-1, keepdims=True)


def _dsa_p_body(q_ref, qi_ref, kw_ref, kt_ref, v_ref, o_ref, *, tq, topk):
    s_len = q_ref.shape[0]
    rep = A_HEADS // A_KV
    kt = kt_ref[...].astype(BF16)
    ki = kw_ref[:, 0:IDX_DIM].astype(BF16)
    v = v_ref[...].astype(BF16)
    for i in range(s_len // tq):
        rows = slice(i * tq, (i + 1) * tq)
        s_vis = (i + 1) * tq
        qit = qi_ref[rows, :].T.astype(BF16)
        wit = kw_ref[rows, :].T
        sc = jnp.zeros((s_vis, tq), F32)
        for h in range(IDX_HEADS):
            raw = _dot(ki[:s_vis], qit[h * IDX_DIM:(h + 1) * IDX_DIM, :])
            sc = sc + jnp.maximum(raw, 0.0) * wit[IDX_DIM + h:IDX_DIM + h + 1, :]
        kpos = lax.broadcasted_iota(jnp.int32, (s_vis, 1), 0)
        qpos = i * tq + lax.broadcasted_iota(jnp.int32, (1, tq), 1)
        visible = kpos <= qpos
        bias = _topk_bias_t(jnp.where(visible, sc, NEG_INF), visible, topk).T
        q = q_ref[rows, :].astype(BF16)
        for hp in range(A_HEADS // 2):
            outs = []
            for h in (2 * hp, 2 * hp + 1):
                gsl = slice((h // rep) * A_DH, (h // rep + 1) * A_DH)
                p, l = _softmax(_dot(q[:, h * A_DH:(h + 1) * A_DH], kt[gsl, :s_vis]) + bias)
                outs.append(_dot(p.astype(BF16), v[:s_vis, gsl]) / l)
            o_ref[rows, hp * LANES:(hp + 1) * LANES] = jnp.concatenate(outs, axis=-1)


def _dsa_prompt(B, L, q, qi, kw, kt, v, tq=256):
    tok = lambda w: pl.BlockSpec((L, w), lambda b: (b, 0))
    seq_t = lambda w: pl.BlockSpec((None, w, L), lambda b: (b, 0, 0))
    return pl.pallas_call(
        functools.partial(_dsa_p_body, tq=tq, topk=min(TOPK_MAX, L // 4)),
        out_shape=jax.ShapeDtypeStruct((B * L, A_HEADS * A_DH), F32),
        grid=(B,),
        in_specs=[tok(512), tok(256), tok(LANES), seq_t(LANES), tok(LANES)],
        out_specs=tok(512),
        compiler_params=_cparams(1),
        name="dsa_prompt",
    )(q, qi, kw, kt, v)


SEQ_PER_STEP = 4


def _page_specs(e, n_pages, rows):
    def index(b, pt, s, p):
        return (e, pt[b * SEQ_PER_STEP + s, p], 0, 0)
    return [pl.BlockSpec((None, None, rows, PAGE), functools.partial(index, s=s, p=p))
            for s in range(SEQ_PER_STEP) for p in range(n_pages)]


def _dsa_s_scores_body(pt_ref, qi_ref, kw_ref, *rest, n_pages, t):
    o_ref = rest[SEQ_PER_STEP * n_pages]
    past = n_pages * PAGE
    lane = lax.broadcasted_iota(jnp.int32, (t, PAGE), 1)
    row = lax.broadcasted_iota(jnp.int32, (t, PAGE), 0)
    for s in range(SEQ_PER_STEP):
        pages = rest[s * n_pages:(s + 1) * n_pages]
        rows = slice(s * t, (s + 1) * t)
        qi = qi_ref[rows, :]
        qs = jnp.concatenate([qi[:, h * IDX_DIM:(h + 1) * IDX_DIM] for h in range(IDX_HEADS)], axis=0).astype(BF16)
        kw = kw_ref[rows, :]
        wi = kw[:, IDX_DIM:LANES]
        ki_new = jnp.concatenate([kw[:, 0:IDX_DIM], jnp.zeros((PAGE - t, IDX_DIM), F32)], axis=0).astype(BF16)
        kit = jnp.concatenate([r[...] for r in pages], axis=-1).astype(BF16)

        def weigh(raw):
            raw = jnp.maximum(raw, 0.0)
            sc = raw[0:t] * wi[:, 0:1]
            for h in range(1, IDX_HEADS):
                sc = sc + raw[h * t:(h + 1) * t] * wi[:, h:h + 1]
            return sc

        o_ref[rows, :past] = weigh(_dot(qs, kit))
        o_ref[rows, past:] = jnp.where(lane <= row, weigh(_dot_nt(qs, ki_new)), NEG_INF)


def _dsa_s_scores(Bs, T, e, page_table, qi, kw, kidx_t):
    n_pages = page_table.shape[1]
    tok = lambda w: pl.BlockSpec((SEQ_PER_STEP * T, w), lambda b, pt: (b, 0))
    s_pad = (n_pages + 1) * PAGE
    return pl.pallas_call(
        functools.partial(_dsa_s_scores_body, n_pages=n_pages, t=T),
        out_shape=jax.ShapeDtypeStruct((Bs * T, s_pad), F32),
        grid_spec=pltpu.PrefetchScalarGridSpec(
            num_scalar_prefetch=1, grid=(Bs // SEQ_PER_STEP,),
            in_specs=[tok(256), tok(LANES)] + _page_specs(e, n_pages, IDX_DIM),
            out_specs=tok(s_pad)),
        compiler_params=_cparams(1),
        name="dsa_s_scores",
    )(page_table, qi, kw, *([kidx_t] * (SEQ_PER_STEP * n_pages)))


def _dsa_s_select_body(sc_ref, o_ref, *, t, past, topk):
    rows, s_pad = sc_ref.shape
    tpos = jnp.bitwise_and(lax.broadcasted_iota(jnp.int32, (1, rows), 1), t - 1)
    kpos = lax.broadcasted_iota(jnp.int32, (s_pad, 1), 0)
    visible = kpos <= past + tpos
    o_ref[...] = _topk_bias_t(sc_ref[...].T, visible, topk).T


def _dsa_s_select(n_rows, T, past, scores, rows=256):
    s_pad = scores.shape[1]
    spec = pl.BlockSpec((rows, s_pad), lambda i: (i, 0))
    return pl.pallas_call(
        functools.partial(_dsa_s_select_body, t=T, past=past, topk=min(TOPK_MAX, (past + T) // 4)),
        out_shape=jax.ShapeDtypeStruct((n_rows, s_pad), F32),
        grid=(n_rows // rows,),
        in_specs=[spec], out_specs=spec,
        compiler_params=_cparams(1),
        name="dsa_s_select",
    )(scores)


def _dsa_s_attn_body(pt_ref, q_ref, kn_ref, vn_ref, b_ref, *rest, n_pages, t):
    n_slots = SEQ_PER_STEP * n_pages
    o_ref = rest[2 * n_slots]
    past = n_pages * PAGE
    rep = A_HEADS // A_KV
    pad = jnp.zeros((PAGE - t, LANES), F32)
    for s in range(SEQ_PER_STEP):
        kp = rest[s * n_pages:(s + 1) * n_pages]
        vp = rest[n_slots + s * n_pages:n_slots + (s + 1) * n_pages]
        rows = slice(s * t, (s + 1) * t)
        kt = jnp.concatenate([r[...] for r in kp], axis=-1).astype(BF16)
        vt = jnp.concatenate([r[...] for r in vp], axis=-1).astype(BF16)
        k_new = jnp.concatenate([kn_ref[rows, :], pad], axis=0).astype(BF16)
        v_new = jnp.concatenate([vn_ref[rows, :], pad], axis=0).astype(BF16)
        q = q_ref[rows, :].astype(BF16)
        bias = jnp.concatenate([b_ref[rows, :]] * rep, axis=0)
        outs = []
        for gi in range(A_KV):
            gsl = slice(gi * A_DH, (gi + 1) * A_DH)
            qg = jnp.concatenate([q[:, (gi * rep + r) * A_DH:(gi * rep + r + 1) * A_DH] for r in range(rep)], axis=0)
            logits = jnp.concatenate([_dot(qg, kt[gsl, :]), _dot_nt(qg, k_new[:, gsl])], axis=-1) + bias
            p, l = _softmax(logits)
            p = p.astype(BF16)
            og = (_dot_nt(p[:, :past], vt[gsl, :]) + _dot(p[:, past:], v_new[:, gsl])) / l
            outs += [og[r * t:(r + 1) * t] for r in range(rep)]
        for hp in range(A_HEADS // 2):
            o_ref[rows, hp * LANES:(hp + 1) * LANES] = jnp.concatenate(outs[2 * hp:2 * hp + 2], axis=-1)


def _dsa_s_attn(Bs, T, e, page_table, q, k_new, v_new, bias, k_t, v_t):
    n_pages = page_table.shape[1]
    n_slots = SEQ_PER_STEP * n_pages
    tok = lambda w: pl.BlockSpec((SEQ_PER_STEP * T, w), lambda b, pt: (b, 0))
    return pl.pallas_call(
        functools.partial(_dsa_s_attn_body, n_pages=n_pages, t=T),
        out_shape=jax.ShapeDtypeStruct((Bs * T, A_HEADS * A_DH), F32),
        grid_spec=pltpu.PrefetchScalarGridSpec(
            num_scalar_prefetch=1, grid=(Bs // SEQ_PER_STEP,),
            in_specs=[tok(512), tok(LANES), tok(LANES), tok(bias.shape[1])]
            + _page_specs(e, n_pages, LANES) + _page_specs(e, n_pages, LANES),
            out_specs=tok(512)),
        compiler_params=_cparams(1),
        name="dsa_s_attn",
    )(page_table, q, k_new, v_new, bias, *([k_t] * n_slots), *([v_t] * n_slots))


def _ret_log_gamma(h):
    return math.log1p(-(2.0 ** (-5.0 - h)))


def _ret_chunk(q, k, v, rg, s, h, c):
    lg = _ret_log_gamma(h)
    ri = lax.broadcasted_iota(jnp.int32, (c, c), 0)
    ci = lax.broadcasted_iota(jnp.int32, (c, c), 1)
    diff = (ri - ci).astype(F32)
    decay = jnp.where(diff >= 0, jnp.exp(jnp.maximum(diff, 0.0) * lg), 0.0)
    idx = lax.broadcasted_iota(jnp.int32, (c, 1), 0).astype(F32)
    q_dec = jnp.exp((idx + 1.0) * lg)
    k_dec = jnp.exp((c - 1.0 - idx) * lg)
    c_dec = math.exp(c * lg)
    qb = q.astype(BF16)
    att = _dot_nt(qb, k.astype(BF16)) * decay
    inner = _dot(att.astype(BF16), v.astype(BF16))
    cross = _dot(qb, s.astype(BF16)) * q_dec
    s_new = s * c_dec + _dot_tn((k * k_dec).astype(BF16), v.astype(BF16))
    return _rms(inner + cross) * _silu(rg), s_new


def _ret_p_body(rq_ref, rk_ref, rv_ref, rg_ref, o_ref, so_ref, s_ref, *, c):
    j = pl.program_id(1)

    @pl.when(j == 0)
    def _():
        s_ref[...] = jnp.zeros_like(s_ref)

    for h in range(RET_HEADS):
        sl = slice(h * RET_DH, (h + 1) * RET_DH)
        o, s_new = _ret_chunk(rq_ref[:, sl], rk_ref[:, sl], rv_ref[:, sl], rg_ref[:, sl], s_ref[h], h, c)
        o_ref[:, sl] = o
        s_ref[h] = s_new
    so_ref[0] = s_ref[...]


def _ret_prompt(B, L, rq, rk, rv, rg, c=256):
    nc = L // c
    tok = pl.BlockSpec((c, 512), lambda b, j: (b * nc + j, 0))
    return pl.pallas_call(
        functools.partial(_ret_p_body, c=c),
        out_shape=[jax.ShapeDtypeStruct((B * L, 512), F32),
                   jax.ShapeDtypeStruct((B, RET_HEADS, RET_DH, RET_DH), F32)],
        grid=(B, nc),
        in_specs=[tok, tok, tok, tok],
        out_specs=[tok, pl.BlockSpec((1, RET_HEADS, RET_DH, RET_DH), lambda b, j: (b, 0, 0, 0))],
        scratch_shapes=[pltpu.VMEM((RET_HEADS, RET_DH, RET_DH), F32)],
        compiler_params=_cparams(2),
        name="ret_prompt",
    )(rq, rk, rv, rg)


def _ret_s_body(rq_ref, rk_ref, rv_ref, rg_ref, s0_ref, o_ref, so_ref, *, t):
    for h in range(RET_HEADS):
        sl = slice(h * RET_DH, (h + 1) * RET_DH)
        o, s_new = _ret_chunk(rq_ref[:, sl], rk_ref[:, sl], rv_ref[:, sl], rg_ref[:, sl], s0_ref[h], h, t)
        o_ref[:, sl] = o
        so_ref[0, h] = s_new


def _ret_sample(Bs, T, e, rq, rk, rv, rg, state_ret):
    tok = pl.BlockSpec((T, 512), lambda b: (b, 0))
    return pl.pallas_call(
        functools.partial(_ret_s_body, t=T),
        out_shape=[jax.ShapeDtypeStruct((Bs * T, 512), F32),
                   jax.ShapeDtypeStruct((Bs, RET_HEADS, RET_DH, RET_DH), F32)],
        grid=(Bs,),
        in_specs=[tok, tok, tok, tok,
                  pl.BlockSpec((None, None, RET_HEADS, RET_DH, RET_DH), lambda b: (e, b, 0, 0, 0))],
        out_specs=[tok, pl.BlockSpec((1, RET_HEADS, RET_DH, RET_DH), lambda b: (b, 0, 0, 0))],
        compiler_params=_cparams(1),
        name="ret_sample",
    )(rq, rk, rv, rg, state_ret)


def _even_out_body(x_ref, oa_ref, or_ref, gt_ref, gpost_ref, wa_ref, wr_ref, o_ref, *, g, rt):
    y = _dot(oa_ref[...].astype(BF16), wa_ref[...]) + _dot(or_ref[...].astype(BF16), wr_ref[...])
    o_ref[...] = x_ref[...] + _gated(_rms(y, gpost_ref[...]), gt_ref[...], g, rt)


def _even_out(t, x, o_a, o_r, mod, mcol, l, e, p):
    half = A_HEADS * A_DH
    return pl.pallas_call(
        functools.partial(_even_out_body, g=t.g, rt=t.rt),
        out_shape=jax.ShapeDtypeStruct((t.n_tok, D), F32),
        grid=t.grid,
        in_specs=[t.tok(D), t.tok(half), t.tok(half), t.mod(mcol), t.sel((None, None, 1, D), (l, 1, 0, 0)),
                  t.sel((None, half, D), (e, 0, 0)), t.sel((None, half, D), (e, 1, 0))],
        out_specs=t.tok(D),
        compiler_params=_cparams(2),
        name="even_out",
    )(x, o_a, o_r, mod, p["g_post"], p["w_out_even"], p["w_out_even"])


def _softplus(x):
    return jnp.maximum(x, 0.0) + jnp.log1p(jnp.exp(-jnp.abs(x)))


def _odd_body(*refs, g, rt, prompt):
    (x_ref, sh_ref, sc_ref, gt_ref, gpre_ref, gpost_ref, win_ref, cw_ref, cb_ref,
     wa_ref, ba_ref, wx_ref, bx_ref, lam_ref, wout_ref) = refs[:15]
    if prompt:
        o_ref, tail_out_ref, h_out_ref, tail_ref, hcar_ref = refs[15:]
    else:
        prev_ref, h0_ref, o_ref, xb_out_ref, hs_out_ref = refs[15:]
    rows = g * rt
    x = x_ref[...]
    h = _prenorm(x, gpre_ref[...], sh_ref[...], sc_ref[...], g, rt).astype(BF16)
    proj = _dot(h, win_ref[...])
    gate_br = proj[:, :D]
    xb = proj[:, D:]
    row = lax.broadcasted_iota(jnp.int32, (rows, 1), 0)
    cw = cw_ref[...]

    if prompt:
        j = pl.program_id(1)

        @pl.when(j == 0)
        def _():
            tail_ref[...] = jnp.zeros_like(tail_ref)
            hcar_ref[...] = jnp.zeros_like(hcar_ref)

        pos = row
        xext = jnp.concatenate([tail_ref[...], xb], axis=0)
        conv = xb * cw[CONV_W - 1:CONV_W] + cb_ref[...]
        for jj in range(CONV_W - 1):
            d = CONV_W - 1 - jj
            conv = conv + pltpu.roll(xext, d, 0)[SUBLANES:] * cw[jj:jj + 1]
        tail_ref[...] = xb[rows - SUBLANES:]
        tail_out_ref[0] = xb[rows - SUBLANES:]
    else:
        pos = jnp.bitwise_and(row, rt - 1)
        prev = prev_ref[...]
        conv = xb * cw[CONV_W - 1:CONV_W] + cb_ref[...]
        for jj in range(CONV_W - 1):
            d = CONV_W - 1 - jj
            tap = jnp.where(pos >= d, pltpu.roll(xb, d, 0), pltpu.roll(prev, rows - (SUBLANES - d), 0))
            conv = conv + tap * cw[jj:jj + 1]
        xb_out_ref[...] = xb

    convb = conv.astype(BF16)
    ra = jnp.concatenate([_dot(convb[:, n * RG_BW:(n + 1) * RG_BW], wa_ref[n]) for n in range(RG_BLOCKS)], axis=-1)
    rx = jnp.concatenate([_dot(convb[:, n * RG_BW:(n + 1) * RG_BW], wx_ref[n]) for n in range(RG_BLOCKS)], axis=-1)
    r = _sigmoid(ra + ba_ref[...])
    ig = _sigmoid(rx + bx_ref[...])
    log_a = (-RG_C) * r * _softplus(-lam_ref[...])
    a = jnp.exp(log_a)
    mult = jnp.sqrt(1.0 - jnp.exp(2.0 * log_a))
    if prompt:
        mult = jnp.where(jnp.logical_and(j == 0, row == 0), 1.0, mult)
    b = mult * (ig * conv)
    if not prompt:
        b = b + a * h0_ref[...]

    n_grp = rows // SUBLANES
    a = a.reshape(n_grp, SUBLANES, D)
    b = b.reshape(n_grp, SUBLANES, D)
    sub = lax.broadcasted_iota(jnp.int32, (1, SUBLANES, D), 1)
    d = 1
    while d < SUBLANES:
        keep = sub >= d
        a_sh = jnp.where(keep, pltpu.roll(a, d, 1), 1.0)
        b_sh = jnp.where(keep, pltpu.roll(b, d, 1), 0.0)
        b = b + a * b_sh
        a = a * a_sh
        d *= 2

    if prompt:
        carry = hcar_ref[0:1]
        parts = []
        for r in range(n_grp):
            part = b[r] + a[r] * carry
            carry = part[SUBLANES - 1:SUBLANES]
            parts.append(part)
        hs = jnp.concatenate(parts, axis=0)
        hcar_ref[0:1] = carry
        h_out_ref[0] = parts[-1]
    else:
        hs = b.reshape(rows, D)
        hs_out_ref[...] = hs
    y = (hs * _gelu_tanh(gate_br)).astype(BF16)
    out = _dot(y, wout_ref[...])
    o_ref[...] = x + _gated(_rms(out, gpost_ref[...]), gt_ref[...], g, rt)


def _odd(t, prompt, x, mod, mcol, l, e, p, prev=None, h0=None):
    vec = t.sel((None, 1, D), (e, 0, 0))
    rgw = t.sel((None, RG_BLOCKS, RG_BW, RG_BW), (e, 0, 0, 0))
    in_specs = [t.tok(D), t.mod(mcol), t.mod(mcol + 1), t.mod(mcol + 2),
                t.sel((None, None, 1, D), (l, 1, 0, 0)), t.sel((None, None, 1, D), (l, 1, 0, 0)),
                t.sel((None, D, 2 * D), (e, 0, 0)), t.sel((None, CONV_W, D), (e, 0, 0)), vec,
                rgw, vec, rgw, vec, vec, t.sel((None, D, D), (e, 0, 0))]
    args = [x, mod, mod, mod, p["g_pre"], p["g_post"], p["w_in_odd"], p["conv_w"], p["conv_b"],
            p["w_rg_a"], p["b_rg_a"], p["w_rg_x"], p["b_rg_x"], p["rg_lambda"], p["w_out_odd"]]
    if prompt:
        last = pl.BlockSpec((1, SUBLANES, D), lambda i, j: (i, 0, 0))
        out_shape = [jax.ShapeDtypeStruct((t.n_tok, D), F32),
                     jax.ShapeDtypeStruct((t.G, SUBLANES, D), F32),
                     jax.ShapeDtypeStruct((t.G, SUBLANES, D), F32)]
        out_specs = [t.tok(D), last, last]
        scratch = [pltpu.VMEM((SUBLANES, D), F32), pltpu.VMEM((SUBLANES, D), F32)]
    else:
        in_specs += [t.tok(D), t.tok(D)]
        args += [prev, h0]
        out_shape = [jax.ShapeDtypeStruct((t.n_tok, D), F32)] * 3
        out_specs = [t.tok(D)] * 3
        scratch = []
    return pl.pallas_call(
        functools.partial(_odd_body, g=t.g, rt=t.rt, prompt=prompt),
        out_shape=out_shape, grid=t.grid, in_specs=in_specs, out_specs=out_specs,
        scratch_shapes=scratch, compiler_params=_cparams(2),
        name="odd_prompt" if prompt else "odd_sample",
    )(*args)


def _rope_tables(pos):
    posf = pos.astype(F32)[:, None]
    inv_a = jnp.power(jnp.float32(ROPE_THETA), -jnp.arange(0, ROT_DIM, 2, dtype=F32) / ROT_DIM)
    ang = posf * inv_a[None, :]
    cos, sin = jnp.cos(ang), jnp.sin(ang)
    n = pos.shape[0]
    half = ROT_DIM // 2
    rest = A_DH - ROT_DIM
    one, zero = jnp.ones((n, rest), F32), jnp.zeros((n, rest), F32)
    zh = jnp.zeros((n, half), F32)
    cos_h = jnp.concatenate([cos, cos, one], axis=1)
    s1_h = jnp.concatenate([-sin, zh, zero], axis=1)
    s2_h = jnp.concatenate([zh, sin, zero], axis=1)
    both = lambda a: jnp.concatenate([a, a], axis=1)
    wi_scale = jnp.full((n, A_DH), IDX_HEADS ** -0.5, F32)
    z64 = jnp.zeros((n, A_DH), F32)
    rope_a = jnp.stack([both(cos_h), both(s1_h), both(s2_h),
                        jnp.concatenate([cos_h, wi_scale], axis=1),
                        jnp.concatenate([s1_h, z64], axis=1),
                        jnp.concatenate([s2_h, z64], axis=1)])
    inv_r = jnp.power(jnp.float32(RET_THETA), -jnp.linspace(0.0, 1.0, RET_DH // 2, dtype=F32))
    ang_r = posf * inv_r[None, :]
    cr, sr = jnp.cos(ang_r), jnp.sin(ang_r)
    rope_r = jnp.stack([jnp.concatenate([cr, cr], axis=1), jnp.concatenate([-sr, sr], axis=1)])
    return rope_a, rope_r


def _pack_w_in_even(w):
    o = np.cumsum((0, 512, 128, 128, 256, IDX_HEADS, IDX_DIM, 512, 512, 512, 512))
    pad = jnp.zeros(w.shape[:2] + (LANES - IDX_DIM - IDX_HEADS,), w.dtype)
    return jnp.concatenate([w[..., o[0]:o[4]], w[..., o[5]:o[6]], w[..., o[4]:o[5]], pad, w[..., o[6]:o[10]]],
                           axis=-1).astype(BF16)


def kernel(x_prompt, x_sample, cache_k, cache_v, cache_kidx, state_ret, state_conv, state_rglru, page_table,
           c_prompt, c_sample, w_ada, b_ada, g_pre, g_post, w_ffn_gate, w_ffn_up, w_ffn_down,
           w_in_even, w_out_even, w_in_odd, w_out_odd, conv_w, conv_b, w_rg_a, b_rg_a, w_rg_x, b_rg_x, rg_lambda):
    B, L, _ = x_prompt.shape
    Bs, T, _ = x_sample.shape
    n_pages = page_table.shape[1]
    past = n_pages * cache_k.shape[2]
    n_phys = cache_k.shape[1]
    assert cache_k.shape[2] == PAGE and T == SUBLANES

    tp = _Tiling(B, L, 1, 512)
    ts = _Tiling(Bs, T, 64, T)
    tp_odd = _Tiling(B, L, 1, 256)
    ts_odd = _Tiling(Bs, T, 32, T)

    vec3 = lambda a: a.reshape(a.shape[0], 1, a.shape[1])
    p = {
        "g_pre": g_pre.reshape(DEPTH, 3, 1, D), "g_post": g_post.reshape(DEPTH, 3, 1, D),
        "wg": w_ffn_gate.astype(BF16), "wu": w_ffn_up.astype(BF16), "wd": w_ffn_down.astype(BF16),
        "w_in_even": _pack_w_in_even(w_in_even), "w_out_even": w_out_even.astype(BF16),
        "w_in_odd": w_in_odd.astype(BF16), "w_out_odd": w_out_odd.astype(BF16),
        "conv_w": conv_w, "conv_b": vec3(conv_b),
        "w_rg_a": w_rg_a.astype(BF16), "b_rg_a": vec3(b_rg_a),
        "w_rg_x": w_rg_x.astype(BF16), "b_rg_x": vec3(b_rg_x), "rg_lambda": vec3(rg_lambda),
    }
    k_t = jnp.transpose(cache_k, (0, 1, 3, 4, 2)).reshape(cache_k.shape[0], n_phys, LANES, PAGE)
    v_t = jnp.transpose(cache_v, (0, 1, 3, 4, 2)).reshape(cache_v.shape[0], n_phys, LANES, PAGE)
    kidx_t = jnp.transpose(cache_kidx, (0, 1, 3, 2))

    mod_all = _ada(jnp.concatenate([c_prompt, c_sample], axis=0), w_ada, b_ada)
    mod_p = mod_all[:, :B].reshape(DEPTH, B, 1, N_MOD * D)
    mod_s = mod_all[:, B:].reshape(DEPTH, Bs, 1, N_MOD * D)

    rope_a_p, rope_r_p = _rope_tables(jnp.arange(L, dtype=jnp.int32))
    rope_a_s, rope_r_s = _rope_tables(jnp.tile(past + jnp.arange(T, dtype=jnp.int32), ts.g))
    rope_map_p = lambda i, j: (0, j, 0)
    rope_map_s = lambda i, j: (0, 0, 0)

    xp = x_prompt.reshape(B * L, D)
    xs = x_sample.reshape(Bs * T, D)
    ks, vs, kis, rets, convs, hs = ([[], []] for _ in range(6))

    for l in range(DEPTH):
        e = l // 2
        groups = ((0, tp, xp, mod_p[l]), (1, ts, xs, mod_s[l]))
        new_x = []
        for gi, t, x, mod in groups:
            x = _ffn(t, x, mod, 0, l, 0, 0, p)
            if l % 2 == 0:
                if gi == 0:
                    q, kt, v, qi, kw, kit, rq, rk, rv, rg = _even_in(
                        t, True, x, mod, 3, l, e, p, rope_a_p, rope_r_p, rope_map_p)
                    o_a = _dsa_prompt(B, L, q, qi, kw, kt, v)
                    o_r, s_new = _ret_prompt(B, L, rq, rk, rv, rg)
                    ks[0].append(jnp.transpose(kt.reshape(B, A_KV, A_DH, L), (0, 3, 1, 2)))
                    vs[0].append(v.reshape(B, L, A_KV, A_DH))
                    kis[0].append(jnp.transpose(kit, (0, 2, 1)))
                else:
                    q, k, v, qi, kw, rq, rk, rv, rg = _even_in(
                        t, False, x, mod, 3, l, e, p, rope_a_s, rope_r_s, rope_map_s)
                    scores = _dsa_s_scores(Bs, T, e, page_table, qi, kw, kidx_t)
                    bias = _dsa_s_select(Bs * T, T, past, scores)
                    o_a = _dsa_s_attn(Bs, T, e, page_table, q, k, v, bias, k_t, v_t)
                    o_r, s_new = _ret_sample(Bs, T, e, rq, rk, rv, rg, state_ret)
                    ks[1].append(k.reshape(Bs, T, A_KV, A_DH))
                    vs[1].append(v.reshape(Bs, T, A_KV, A_DH))
                    kis[1].append(kw[:, :IDX_DIM].reshape(Bs, T, IDX_DIM))
                rets[gi].append(s_new)
                x = _even_out(t, x, o_a, o_r, mod, 5, l, e, p)
            else:
                if gi == 0:
                    x, tail, hlast = _odd(tp_odd, True, x, mod, 3, l, e, p)
                    convs[0].append(tail[:, SUBLANES - (CONV_W - 1):])
                    hs[0].append(hlast[:, SUBLANES - 1])
                else:
                    prev = jnp.pad(state_conv[e], ((0, 0), (SUBLANES - (CONV_W - 1), 0), (0, 0))).reshape(Bs * T, D)
                    h0 = jnp.pad(state_rglru[e][:, None, :], ((0, 0), (0, T - 1), (0, 0))).reshape(Bs * T, D)
                    x, xb, hseq = _odd(ts_odd, False, x, mod, 3, l, e, p, prev=prev, h0=h0)
                    convs[1].append(xb.reshape(Bs, T, D)[:, T - (CONV_W - 1):])
                    hs[1].append(hseq.reshape(Bs, T, D)[:, T - 1])
            x = _ffn(t, x, mod, 6, l, 1, 2, p)
            new_x.append(x)
        xp, xs = new_x

    st = lambda lists, gi: jnp.stack(lists[gi])
    return (xp.reshape(B, L, D), xs.reshape(Bs, T, D),
            st(ks, 0), st(vs, 0), st(kis, 0), st(rets, 0), st(convs, 0), st(hs, 0),
            st(ks, 1), st(vs, 1), st(kis, 1), st(rets, 1), st(convs, 1), st(hs, 1))
```

```python
import functools
import math

import jax
import jax.numpy as jnp
import numpy as np
from jax import lax
from jax.experimental import pallas as pl
from jax.experimental.pallas import tpu as pltpu

F32 = jnp.float32
BF16 = jnp.bfloat16

D = 1024
DEPTH = 4
N_MOD = 9
A_HEADS = 8
A_KV = 2
A_DH = 64
ROT_DIM = 16
ROPE_THETA = 500000.0
IDX_HEADS = 4
IDX_DIM = 64
TOPK_MAX = 256
RET_HEADS = 4
RET_DH = 128
RET_THETA = 10000.0
RG_BLOCKS = 8
RG_BW = 128
CONV_W = 4
RG_C = 8.0
D_FF = 2816
EPS = 1e-6
P_IN = 3200
PAGE = 128

LANES = 128
SUBLANES = 8
VMEM_LIMIT = 56 * 1024 * 1024

NEG_INF = float("-inf")
INT_MIN = -(2 ** 31)


def _cparams(n_axes, vmem=VMEM_LIMIT):
    return pltpu.CompilerParams(dimension_semantics=("arbitrary",) * n_axes, vmem_limit_bytes=vmem)


def _dot(a, b):
    return jnp.dot(a, b, preferred_element_type=F32)


def _dot_nt(a, b):
    return lax.dot_general(a, b, (((1,), (1,)), ((), ())), preferred_element_type=F32)


def _dot_tn(a, b):
    return lax.dot_general(a, b, (((0,), (0,)), ((), ())), preferred_element_type=F32)


def _sigmoid(x):
    return 1.0 / (1.0 + jnp.exp(-x))


def _silu(x):
    return x * _sigmoid(x)


def _gelu_tanh(x):
    return 0.5 * x * (1.0 + jnp.tanh(math.sqrt(2.0 / math.pi) * (x + 0.044715 * (x * x * x))))


def _rms(x, g=None):
    y = x * lax.rsqrt(jnp.mean(x * x, axis=-1, keepdims=True) + EPS)
    return y if g is None else y * g


def _prenorm(x, g_pre, shift, scale, g, rt):
    h = _rms(x, g_pre)
    if g == 1:
        return h * (1.0 + scale[0]) + shift[0]
    h3 = h.reshape(g, rt, D)
    return (h3 * (1.0 + scale) + shift).reshape(g * rt, D)


def _gated(y, gate, g, rt):
    if g == 1:
        return y * gate[0]
    return (y.reshape(g, rt, D) * gate).reshape(g * rt, D)


class _Tiling:
    def __init__(self, G, R, g, rt):
        assert G % g == 0 and R % rt == 0 and (g == 1 or rt == R)
        self.G, self.R, self.g, self.rt = G, R, g, rt
        self.nr = R // rt
        self.rows = g * rt
        self.grid = (G // g, self.nr)
        self.n_tok = G * R

    def tok(self, width):
        nr = self.nr
        return pl.BlockSpec((self.rows, width), lambda i, j: (i * nr + j, 0))

    def mod(self, c):
        return pl.BlockSpec((self.g, 1, D), lambda i, j: (i, 0, c))

    def sel(self, block, idx):
        return pl.BlockSpec(block, lambda i, j: idx, pipeline_mode=pl.Buffered(1))


def _ada_body(c_ref, w_ref, b_ref, o_ref):
    c = _silu(c_ref[...]).astype(BF16)
    o_ref[0] = _dot(c, w_ref[0].astype(BF16)) + b_ref[0]


def _ada(c_all, w_ada, b_ada):
    n = c_all.shape[0]
    return pl.pallas_call(
        _ada_body,
        out_shape=jax.ShapeDtypeStruct((DEPTH, n, N_MOD * D), F32),
        grid=(DEPTH, N_MOD),
        in_specs=[pl.BlockSpec((n, D), lambda l, j: (0, 0)),
                  pl.BlockSpec((1, D, D), lambda l, j: (l, 0, j)),
                  pl.BlockSpec((1, 1, D), lambda l, j: (l, 0, j))],
        out_specs=pl.BlockSpec((1, n, D), lambda l, j: (l, 0, j)),
        compiler_params=_cparams(2),
        name="ada",
    )(c_all, w_ada, b_ada.reshape(DEPTH, 1, N_MOD * D))


def _ffn_body(x_ref, sh_ref, sc_ref, gt_ref, gpre_ref, gpost_ref, wg_ref, wu_ref, wd_ref, o_ref, *, g, rt):
    x = x_ref[...]
    h = _prenorm(x, gpre_ref[...], sh_ref[...], sc_ref[...], g, rt).astype(BF16)
    a = _dot(h, wg_ref[...])
    u = _dot(h, wu_ref[...])
    act = (_silu(a) * u).astype(BF16)
    f = _dot(act, wd_ref[...])
    y = _rms(f, gpost_ref[...])
    o_ref[...] = x + 0.5 * _gated(y, gt_ref[...], g, rt)


def _ffn(t, x, mod, mcol, l, s, n, p):
    return pl.pallas_call(
        functools.partial(_ffn_body, g=t.g, rt=t.rt),
        out_shape=jax.ShapeDtypeStruct((t.n_tok, D), F32),
        grid=t.grid,
        in_specs=[t.tok(D), t.mod(mcol), t.mod(mcol + 1), t.mod(mcol + 2),
                  t.sel((None, None, 1, D), (l, n, 0, 0)), t.sel((None, None, 1, D), (l, n, 0, 0)),
                  t.sel((None, None, D, D_FF), (l, s, 0, 0)), t.sel((None, None, D, D_FF), (l, s, 0, 0)),
                  t.sel((None, None, D_FF, D), (l, s, 0, 0))],
        out_specs=t.tok(D),
        compiler_params=_cparams(2),
        name="ffn",
    )(x, mod, mod, mod, p["g_pre"], p["g_post"], p["wg"], p["wu"], p["wd"])


def _even_in_body(*refs, g, rt, prompt):
    x_ref, sh_ref, sc_ref, gpre_ref, w_ref, ra_ref, rr_ref = refs[:7]
    q_ref, k_ref, v_ref, qi_ref, kw_ref = refs[7:12]
    if prompt:
        kt_ref, kit_ref, rq_ref, rk_ref, rv_ref, rg_ref = refs[12:]
    else:
        rq_ref, rk_ref, rv_ref, rg_ref = refs[12:]
    h = _prenorm(x_ref[...], gpre_ref[...], sh_ref[...], sc_ref[...], g, rt).astype(BF16)
    proj = _dot(h, w_ref[...])

    def sec(c):
        return proj[:, c * LANES:(c + 1) * LANES]

    def rope_a(t, v):
        return (t * ra_ref[3 * v] + pltpu.roll(t, LANES - ROT_DIM // 2, 1) * ra_ref[3 * v + 1]
                + pltpu.roll(t, ROT_DIM // 2, 1) * ra_ref[3 * v + 2])

    def rope_r(t):
        return t * rr_ref[0] + pltpu.roll(t, RET_DH // 2, 1) * rr_ref[1]

    for c in range(4):
        q_ref[:, c * LANES:(c + 1) * LANES] = rope_a(sec(c), 0) * (A_DH ** -0.5)
    k = rope_a(sec(4), 0)
    v_ref[...] = sec(5)
    for c in range(2):
        qi_ref[:, c * LANES:(c + 1) * LANES] = rope_a(sec(6 + c), 0) * (IDX_DIM ** -0.5)
    kw = rope_a(sec(8), 1)
    kw_ref[...] = kw
    k_ref[...] = k
    if prompt:
        kt_ref[...] = k.T
        kit_ref[...] = kw.T[0:IDX_DIM]
    for c in range(4):
        rq_ref[:, c * LANES:(c + 1) * LANES] = rope_r(sec(9 + c))
        rk_ref[:, c * LANES:(c + 1) * LANES] = rope_r(sec(13 + c) * (RET_DH ** -0.5))
        rv_ref[:, c * LANES:(c + 1) * LANES] = sec(17 + c)
        rg_ref[:, c * LANES:(c + 1) * LANES] = sec(21 + c)


def _even_in(t, prompt, x, mod, mcol, l, e, p, rope_a, rope_r, rope_map):
    tokw = lambda w: (jax.ShapeDtypeStruct((t.n_tok, w), F32), t.tok(w))
    outs = [tokw(512), tokw(LANES), tokw(LANES), tokw(256), tokw(LANES)]
    if prompt:
        seq_t = lambda w: (jax.ShapeDtypeStruct((t.G, w, t.R), F32),
                           pl.BlockSpec((None, w, t.rt), lambda i, j: (i, 0, j)))
        outs += [seq_t(LANES), seq_t(IDX_DIM)]
    outs += [tokw(512)] * 4
    return pl.pallas_call(
        functools.partial(_even_in_body, g=t.g, rt=t.rt, prompt=prompt),
        out_shape=[o[0] for o in outs],
        grid=t.grid,
        in_specs=[t.tok(D), t.mod(mcol), t.mod(mcol + 1), t.sel((None, None, 1, D), (l, 1, 0, 0)),
                  t.sel((None, D, P_IN), (e, 0, 0)),
                  pl.BlockSpec((6, t.rows, LANES), rope_map),
                  pl.BlockSpec((2, t.rows, LANES), rope_map)],
        out_specs=[o[1] for o in outs],
        compiler_params=_cparams(2),
        name="even_in_p" if prompt else "even_in_s",
    )(x, mod, mod, p["g_pre"], p["w_in_even"], rope_a, rope_r)


def _sortable(s):
    s = jnp.where(s == 0.0, 0.0, s)
    b = lax.bitcast_convert_type(s, jnp.int32)
    return b ^ ((b >> 31) & 0x7FFFFFFF)


def _topk_bias_t(scores_t, visible_t, k):
    s_len, n_q = scores_t.shape
    key = _sortable(scores_t)
    fold = 8 * SUBLANES
    assert s_len % fold == 0

    def count(mask):
        part = jnp.sum(jnp.where(mask, 1.0, 0.0).reshape(s_len // fold, fold, n_q), axis=0)
        return jnp.sum(part, axis=0, keepdims=True)

    def search(it, thr):
        cand = thr + jnp.left_shift(jnp.int32(1), 31 - it)
        return jnp.where(count(key >= cand) >= k, cand, thr)

    thr = lax.fori_loop(0, 32, search, jnp.full((1, n_q), INT_MIN, jnp.int32))
    ge = key >= thr
    cnt_ge = count(ge)

    def no_ties():
        return jnp.where(visible_t, jnp.where(ge, 0.0, NEG_INF), NEG_INF)

    def ties():
        need = k - count(key > thr)
        tri = jnp.where(lax.broadcasted_iota(jnp.int32, (LANES, LANES), 0)
                        >= lax.broadcasted_iota(jnp.int32, (LANES, LANES), 1), 1.0, 0.0).astype(BF16)
        off = jnp.zeros((1, n_q), F32)
        pieces = []
        for c in range(s_len // LANES):
            sl = slice(c * LANES, (c + 1) * LANES)
            kc = key[sl, :]
            eq = kc == thr
            pc = _dot(tri, jnp.where(eq, 1.0, 0.0).astype(BF16)) + off
            off = pc[LANES - 1:LANES, :]
            tie_ok = jnp.where(eq, jnp.where(pc <= need, 0.0, NEG_INF), NEG_INF)
            b = jnp.where(kc > thr, 0.0, tie_ok)
            pieces.append(jnp.where(visible_t[sl, :], b, NEG_INF))
        return jnp.concatenate(pieces, axis=0)

    return lax.cond(jnp.max(cnt_ge) > k, ties, no_ties)


def _softmax(logits):
    m = jnp.max(logits, axis=-1, keepdims=True)
    p = jnp.exp(logits - m)
    return p, jnp.sum(p, axis=-1, keepdims=True)


SOFTMAX_FLOOR = -1e30


def _dsa_p_body(q_ref, qi_ref, kwq_ref, k_ref, kw_ref, v_ref, o_ref, key_ref, bias_ref, *, tq, topk):
    i = pl.program_id(1)
    n_ch = i + 1
    half = tq // 2
    rep = A_HEADS // A_KV
    qit = qi_ref[...].T.astype(BF16)
    wit = kwq_ref[...].T
    qpos = i * tq + lax.broadcasted_iota(jnp.int32, (1, tq), 1)
    krow = lax.broadcasted_iota(jnp.int32, (tq, 1), 0)

    def visible(c):
        return c * tq + krow <= qpos

    def score_chunk(c, carry):
        r0 = pl.multiple_of(c * tq, tq)
        ki = kw_ref[pl.ds(r0, tq), 0:IDX_DIM].astype(BF16)
        sc = jnp.zeros((tq, tq), F32)
        for h in range(IDX_HEADS):
            raw = _dot(ki, qit[h * IDX_DIM:(h + 1) * IDX_DIM, :])
            sc = sc + jnp.maximum(raw, 0.0) * wit[IDX_DIM + h:IDX_DIM + h + 1, :]
        key_ref[c] = _sortable(jnp.where(visible(c), sc, NEG_INF))
        return carry

    lax.fori_loop(0, n_ch, score_chunk, 0)

    fold = 8 * SUBLANES

    def count(pred):
        def body(c, part):
            ind = jnp.where(pred(key_ref[c]), 1.0, 0.0)
            return part + jnp.sum(ind.reshape(tq // fold, fold, tq), axis=0)
        part = lax.fori_loop(0, n_ch, body, jnp.zeros((fold, tq), F32))
        return jnp.sum(part, axis=0, keepdims=True)

    def search(it, thr):
        cand = thr + jnp.left_shift(jnp.int32(1), 31 - it)
        return jnp.where(count(lambda kc: kc >= cand) >= topk, cand, thr)

    thr = lax.fori_loop(0, 32, search, jnp.full((1, tq), INT_MIN, jnp.int32))
    has_ties = jnp.max(count(lambda kc: kc >= thr)) > topk

    def put_bias(c, b):
        bias_ref[2 * c] = b[:half]
        bias_ref[2 * c + 1] = b[half:]

    @pl.when(jnp.logical_not(has_ties))
    def _():
        def body(c, carry):
            put_bias(c, jnp.where(visible(c), jnp.where(key_ref[c] >= thr, 0.0, NEG_INF), NEG_INF))
            return carry
        lax.fori_loop(0, n_ch, body, 0)

    @pl.when(has_ties)
    def _():
        need = topk - count(lambda kc: kc > thr)
        tri = jnp.where(lax.broadcasted_iota(jnp.int32, (tq, tq), 0)
                        >= lax.broadcasted_iota(jnp.int32, (tq, tq), 1), 1.0, 0.0).astype(BF16)

        def body(c, off):
            kc = key_ref[c]
            eq = kc == thr
            pc = _dot(tri, jnp.where(eq, 1.0, 0.0).astype(BF16)) + off
            tie_ok = jnp.where(eq, jnp.where(pc <= need, 0.0, NEG_INF), NEG_INF)
            put_bias(c, jnp.where(visible(c), jnp.where(kc > thr, 0.0, tie_ok), NEG_INF))
            return pc[tq - 1:tq, :]
        lax.fori_loop(0, n_ch, body, jnp.zeros((1, tq), F32))

    qt = q_ref[...].T.astype(BF16)

    def attend(c, state):
        r0 = pl.multiple_of(c * half, half)
        kc = k_ref[pl.ds(r0, half), :].astype(BF16)
        vc = v_ref[pl.ds(r0, half), :].astype(BF16)
        bias = bias_ref[c]
        gsl = [slice((h // rep) * A_DH, (h // rep + 1) * A_DH) for h in range(A_HEADS)]
        logits = [_dot(kc[:, gsl[h]], qt[h * A_DH:(h + 1) * A_DH, :]) + bias for h in range(A_HEADS)]
        m_new = [jnp.maximum(state[h][0], jnp.max(logits[h], axis=0, keepdims=True)) for h in range(A_HEADS)]
        alpha = [jnp.exp(state[h][0] - m_new[h]) for h in range(A_HEADS)]
        p = [jnp.exp(logits[h] - m_new[h]) for h in range(A_HEADS)]
        l = [state[h][1] * alpha[h] + jnp.sum(p[h], axis=0, keepdims=True) for h in range(A_HEADS)]
        pv = [_dot_tn(vc[:, gsl[h]], p[h].astype(BF16)) for h in range(A_HEADS)]
        return tuple((m_new[h], l[h], state[h][2] * alpha[h] + pv[h]) for h in range(A_HEADS))

    init = (jnp.full((1, tq), SOFTMAX_FLOOR, F32), jnp.zeros((1, tq), F32), jnp.zeros((A_DH, tq), F32))
    state = lax.fori_loop(0, 2 * n_ch, attend, (init,) * A_HEADS)
    for hp in range(A_HEADS // 2):
        outs = [(state[h][2] / state[h][1]).T for h in (2 * hp, 2 * hp + 1)]
        o_ref[:, hp * LANES:(hp + 1) * LANES] = jnp.concatenate(outs, axis=-1)


def _dsa_prompt(B, L, q, qi, kw, k, v, tq=256):
    nq = L // tq
    tok = lambda w: pl.BlockSpec((tq, w), lambda b, i: (b * nq + i, 0))
    seq = lambda w: pl.BlockSpec((L, w), lambda b, i: (b, 0))
    return pl.pallas_call(
        functools.partial(_dsa_p_body, tq=tq, topk=min(TOPK_MAX, L // 4)),
        out_shape=jax.ShapeDtypeStruct((B * L, A_HEADS * A_DH), F32),
        grid=(B, nq),
        in_specs=[tok(512), tok(256), tok(LANES), seq(LANES), seq(LANES), seq(LANES)],
        out_specs=tok(512),
        scratch_shapes=[pltpu.VMEM((nq, tq, tq), jnp.int32), pltpu.VMEM((2 * nq, tq // 2, tq), F32)],
        compiler_params=_cparams(2),
        name="dsa_prompt",
    )(q, qi, kw, k, kw, v)


SEQ_PER_STEP = 4


def _page_specs(e, n_pages, rows):
    def index(b, pt, s, p):
        return (e, pt[b * SEQ_PER_STEP + s, p], 0, 0)
    return [pl.BlockSpec((None, None, rows, PAGE), functools.partial(index, s=s, p=p))
            for s in range(SEQ_PER_STEP) for p in range(n_pages)]


def _dsa_s_scores_body(pt_ref, qi_ref, kw_ref, *rest, n_pages, t):
    o_ref = rest[SEQ_PER_STEP * n_pages]
    past = n_pages * PAGE
    lane = lax.broadcasted_iota(jnp.int32, (t, PAGE), 1)
    row = lax.broadcasted_iota(jnp.int32, (t, PAGE), 0)
    for s in range(SEQ_PER_STEP):
        pages = rest[s * n_pages:(s + 1) * n_pages]
        rows = slice(s * t, (s + 1) * t)
        qi = qi_ref[rows, :]
        qs = jnp.concatenate([qi[:, h * IDX_DIM:(h + 1) * IDX_DIM] for h in range(IDX_HEADS)], axis=0).astype(BF16)
        kw = kw_ref[rows, :]
        wi = kw[:, IDX_DIM:LANES]
        ki_new = jnp.concatenate([kw[:, 0:IDX_DIM], jnp.zeros((PAGE - t, IDX_DIM), F32)], axis=0).astype(BF16)
        kit = jnp.concatenate([r[...] for r in pages], axis=-1).astype(BF16)

        def weigh(raw):
            raw = jnp.maximum(raw, 0.0)
            sc = raw[0:t] * wi[:, 0:1]
            for h in range(1, IDX_HEADS):
                sc = sc + raw[h * t:(h + 1) * t] * wi[:, h:h + 1]
            return sc

        o_ref[rows, :past] = weigh(_dot(qs, kit))
        o_ref[rows, past:] = jnp.where(lane <= row, weigh(_dot_nt(qs, ki_new)), NEG_INF)


def _dsa_s_scores(Bs, T, e, page_table, qi, kw, kidx_t):
    n_pages = page_table.shape[1]
    tok = lambda w: pl.BlockSpec((SEQ_PER_STEP * T, w), lambda b, pt: (b, 0))
    s_pad = (n_pages + 1) * PAGE
    return pl.pallas_call(
        functools.partial(_dsa_s_scores_body, n_pages=n_pages, t=T),
        out_shape=jax.ShapeDtypeStruct((Bs * T, s_pad), F32),
        grid_spec=pltpu.PrefetchScalarGridSpec(
            num_scalar_prefetch=1, grid=(Bs // SEQ_PER_STEP,),
            in_specs=[tok(256), tok(LANES)] + _page_specs(e, n_pages, IDX_DIM),
            out_specs=tok(s_pad)),
        compiler_params=_cparams(1),
        name="dsa_s_scores",
    )(page_table, qi, kw, *([kidx_t] * (SEQ_PER_STEP * n_pages)))


def _dsa_s_select_body(sc_ref, o_ref, *, t, past, topk):
    rows, s_pad = sc_ref.shape
    tpos = jnp.bitwise_and(lax.broadcasted_iota(jnp.int32, (1, rows), 1), t - 1)
    kpos = lax.broadcasted_iota(jnp.int32, (s_pad, 1), 0)
    visible = kpos <= past + tpos
    o_ref[...] = _topk_bias_t(sc_ref[...].T, visible, topk).T


def _dsa_s_select(n_rows, T, past, scores, rows=256):
    s_pad = scores.shape[1]
    spec = pl.BlockSpec((rows, s_pad), lambda i: (i, 0))
    return pl.pallas_call(
        functools.partial(_dsa_s_select_body, t=T, past=past, topk=min(TOPK_MAX, (past + T) // 4)),
        out_shape=jax.ShapeDtypeStruct((n_rows, s_pad), F32),
        grid=(n_rows // rows,),
        in_specs=[spec], out_specs=spec,
        compiler_params=_cparams(1),
        name="dsa_s_select",
    )(scores)


def _dsa_s_attn_body(pt_ref, q_ref, kn_ref, vn_ref, b_ref, *rest, n_pages, t):
    n_slots = SEQ_PER_STEP * n_pages
    o_ref = rest[2 * n_slots]
    past = n_pages * PAGE
    rep = A_HEADS // A_KV
    pad = jnp.zeros((PAGE - t, LANES), F32)
    for s in range(SEQ_PER_STEP):
        kp = rest[s * n_pages:(s + 1) * n_pages]
        vp = rest[n_slots + s * n_pages:n_slots + (s + 1) * n_pages]
        rows = slice(s * t, (s + 1) * t)
        kt = jnp.concatenate([r[...] for r in kp], axis=-1).astype(BF16)
        vt = jnp.concatenate([r[...] for r in vp], axis=-1).astype(BF16)
        k_new = jnp.concatenate([kn_ref[rows, :], pad], axis=0).astype(BF16)
        v_new = jnp.concatenate([vn_ref[rows, :], pad], axis=0).astype(BF16)
        q = q_ref[rows, :].astype(BF16)
        bias = jnp.concatenate([b_ref[rows, :]] * rep, axis=0)
        outs = []
        for gi in range(A_KV):
            gsl = slice(gi * A_DH, (gi + 1) * A_DH)
            qg = jnp.concatenate([q[:, (gi * rep + r) * A_DH:(gi * rep + r + 1) * A_DH] for r in range(rep)], axis=0)
            logits = jnp.concatenate([_dot(qg, kt[gsl, :]), _dot_nt(qg, k_new[:, gsl])], axis=-1) + bias
            p, l = _softmax(logits)
            p = p.astype(BF16)
            og = (_dot_nt(p[:, :past], vt[gsl, :]) + _dot(p[:, past:], v_new[:, gsl])) / l
            outs += [og[r * t:(r + 1) * t] for r in range(rep)]
        for hp in range(A_HEADS // 2):
            o_ref[rows, hp * LANES:(hp + 1) * LANES] = jnp.concatenate(outs[2 * hp:2 * hp + 2], axis=-1)


def _dsa_s_attn(Bs, T, e, page_table, q, k_new, v_new, bias, k_t, v_t):
    n_pages = page_table.shape[1]
    n_slots = SEQ_PER_STEP * n_pages
    tok = lambda w: pl.BlockSpec((SEQ_PER_STEP * T, w), lambda b, pt: (b, 0))
    return pl.pallas_call(
        functools.partial(_dsa_s_attn_body, n_pages=n_pages, t=T),
        out_shape=jax.ShapeDtypeStruct((Bs * T, A_HEADS * A_DH), F32),
        grid_spec=pltpu.PrefetchScalarGridSpec(
            num_scalar_prefetch=1, grid=(Bs // SEQ_PER_STEP,),
            in_specs=[tok(512), tok(LANES), tok(LANES), tok(bias.shape[1])]
            + _page_specs(e, n_pages, LANES) + _page_specs(e, n_pages, LANES),
            out_specs=tok(512)),
        compiler_params=_cparams(1),
        name="dsa_s_attn",
    )(page_table, q, k_new, v_new, bias, *([k_t] * n_slots), *([v_t] * n_slots))


def _ret_log_gamma(h):
    return math.log1p(-(2.0 ** (-5.0 - h)))


def _ret_chunk(q, k, v, rg, s, h, c):
    lg = _ret_log_gamma(h)
    ri = lax.broadcasted_iota(jnp.int32, (c, c), 0)
    ci = lax.broadcasted_iota(jnp.int32, (c, c), 1)
    diff = (ri - ci).astype(F32)
    decay = jnp.where(diff >= 0, jnp.exp(jnp.maximum(diff, 0.0) * lg), 0.0)
    idx = lax.broadcasted_iota(jnp.int32, (c, 1), 0).astype(F32)
    q_dec = jnp.exp((idx + 1.0) * lg)
    k_dec = jnp.exp((c - 1.0 - idx) * lg)
    c_dec = math.exp(c * lg)
    qb = q.astype(BF16)
    att = _dot_nt(qb, k.astype(BF16)) * decay
    inner = _dot(att.astype(BF16), v.astype(BF16))
    cross = _dot(qb, s.astype(BF16)) * q_dec
    s_new = s * c_dec + _dot_tn((k * k_dec).astype(BF16), v.astype(BF16))
    return _rms(inner + cross) * _silu(rg), s_new


def _ret_p_body(rq_ref, rk_ref, rv_ref, rg_ref, o_ref, so_ref, s_ref, *, c):
    j = pl.program_id(1)

    @pl.when(j == 0)
    def _():
        s_ref[...] = jnp.zeros_like(s_ref)

    for h in range(RET_HEADS):
        sl = slice(h * RET_DH, (h + 1) * RET_DH)
        o, s_new = _ret_chunk(rq_ref[:, sl], rk_ref[:, sl], rv_ref[:, sl], rg_ref[:, sl], s_ref[h], h, c)
        o_ref[:, sl] = o
        s_ref[h] = s_new
    so_ref[0] = s_ref[...]


def _ret_prompt(B, L, rq, rk, rv, rg, c=256):
    nc = L // c
    tok = pl.BlockSpec((c, 512), lambda b, j: (b * nc + j, 0))
    return pl.pallas_call(
        functools.partial(_ret_p_body, c=c),
        out_shape=[jax.ShapeDtypeStruct((B * L, 512), F32),
                   jax.ShapeDtypeStruct((B, RET_HEADS, RET_DH, RET_DH), F32)],
        grid=(B, nc),
        in_specs=[tok, tok, tok, tok],
        out_specs=[tok, pl.BlockSpec((1, RET_HEADS, RET_DH, RET_DH), lambda b, j: (b, 0, 0, 0))],
        scratch_shapes=[pltpu.VMEM((RET_HEADS, RET_DH, RET_DH), F32)],
        compiler_params=_cparams(2),
        name="ret_prompt",
    )(rq, rk, rv, rg)


def _ret_s_body(rq_ref, rk_ref, rv_ref, rg_ref, s0_ref, o_ref, so_ref, *, t):
    for s in range(SEQ_PER_STEP):
        rows = slice(s * t, (s + 1) * t)
        for h in range(RET_HEADS):
            sl = slice(h * RET_DH, (h + 1) * RET_DH)
            o, s_new = _ret_chunk(rq_ref[rows, sl], rk_ref[rows, sl], rv_ref[rows, sl], rg_ref[rows, sl],
                                  s0_ref[s, h], h, t)
            o_ref[rows, sl] = o
            so_ref[s, h] = s_new


def _ret_sample(Bs, T, e, rq, rk, rv, rg, state_ret):
    tok = pl.BlockSpec((SEQ_PER_STEP * T, 512), lambda b: (b, 0))
    return pl.pallas_call(
        functools.partial(_ret_s_body, t=T),
        out_shape=[jax.ShapeDtypeStruct((Bs * T, 512), F32),
                   jax.ShapeDtypeStruct((Bs, RET_HEADS, RET_DH, RET_DH), F32)],
        grid=(Bs // SEQ_PER_STEP,),
        in_specs=[tok, tok, tok, tok,
                  pl.BlockSpec((None, SEQ_PER_STEP, RET_HEADS, RET_DH, RET_DH), lambda b: (e, b, 0, 0, 0))],
        out_specs=[tok, pl.BlockSpec((SEQ_PER_STEP, RET_HEADS, RET_DH, RET_DH), lambda b: (b, 0, 0, 0))],
        compiler_params=_cparams(1),
        name="ret_sample",
    )(rq, rk, rv, rg, state_ret)


def _even_out_body(x_ref, oa_ref, or_ref, gt_ref, gpost_ref, wa_ref, wr_ref, o_ref, *, g, rt):
    y = _dot(oa_ref[...].astype(BF16), wa_ref[...]) + _dot(or_ref[...].astype(BF16), wr_ref[...])
    o_ref[...] = x_ref[...] + _gated(_rms(y, gpost_ref[...]), gt_ref[...], g, rt)


def _even_out(t, x, o_a, o_r, mod, mcol, l, e, p):
    half = A_HEADS * A_DH
    return pl.pallas_call(
        functools.partial(_even_out_body, g=t.g, rt=t.rt),
        out_shape=jax.ShapeDtypeStruct((t.n_tok, D), F32),
        grid=t.grid,
        in_specs=[t.tok(D), t.tok(half), t.tok(half), t.mod(mcol), t.sel((None, None, 1, D), (l, 1, 0, 0)),
                  t.sel((None, half, D), (e, 0, 0)), t.sel((None, half, D), (e, 1, 0))],
        out_specs=t.tok(D),
        compiler_params=_cparams(2),
        name="even_out",
    )(x, o_a, o_r, mod, p["g_post"], p["w_out_even"], p["w_out_even"])


def _softplus(x):
    return jnp.maximum(x, 0.0) + jnp.log1p(jnp.exp(-jnp.abs(x)))


def _odd_body(*refs, g, rt, prompt):
    (x_ref, sh_ref, sc_ref, gt_ref, gpre_ref, gpost_ref, win_ref, cw_ref, cb_ref,
     wa_ref, ba_ref, wx_ref, bx_ref, lam_ref, wout_ref) = refs[:15]
    if prompt:
        o_ref, tail_out_ref, h_out_ref, tail_ref, hcar_ref = refs[15:]
    else:
        prev_ref, h0_ref, o_ref, xb_out_ref, hs_out_ref = refs[15:]
    rows = g * rt
    x = x_ref[...]
    h = _prenorm(x, gpre_ref[...], sh_ref[...], sc_ref[...], g, rt).astype(BF16)
    proj = _dot(h, win_ref[...])
    gate_br = proj[:, :D]
    xb = proj[:, D:]
    row = lax.broadcasted_iota(jnp.int32, (rows, 1), 0)
    cw = cw_ref[...]

    if prompt:
        j = pl.program_id(1)

        @pl.when(j == 0)
        def _():
            tail_ref[...] = jnp.zeros_like(tail_ref)
            hcar_ref[...] = jnp.zeros_like(hcar_ref)

        pos = row
        xext = jnp.concatenate([tail_ref[...], xb], axis=0)
        conv = xb * cw[CONV_W - 1:CONV_W] + cb_ref[...]
        for jj in range(CONV_W - 1):
            d = CONV_W - 1 - jj
            conv = conv + pltpu.roll(xext, d, 0)[SUBLANES:] * cw[jj:jj + 1]
        tail_ref[...] = xb[rows - SUBLANES:]
        tail_out_ref[0] = xb[rows - SUBLANES:]
    else:
        pos = jnp.bitwise_and(row, rt - 1)
        prev = prev_ref[...]
        conv = xb * cw[CONV_W - 1:CONV_W] + cb_ref[...]
        for jj in range(CONV_W - 1):
            d = CONV_W - 1 - jj
            tap = jnp.where(pos >= d, pltpu.roll(xb, d, 0), pltpu.roll(prev, rows - (SUBLANES - d), 0))
            conv = conv + tap * cw[jj:jj + 1]
        xb_out_ref[...] = xb

    convb = conv.astype(BF16)
    ra = jnp.concatenate([_dot(convb[:, n * RG_BW:(n + 1) * RG_BW], wa_ref[n]) for n in range(RG_BLOCKS)], axis=-1)
    rx = jnp.concatenate([_dot(convb[:, n * RG_BW:(n + 1) * RG_BW], wx_ref[n]) for n in range(RG_BLOCKS)], axis=-1)
    r = _sigmoid(ra + ba_ref[...])
    ig = _sigmoid(rx + bx_ref[...])
    log_a = (-RG_C) * r * _softplus(-lam_ref[...])
    a = jnp.exp(log_a)
    mult = jnp.sqrt(1.0 - jnp.exp(2.0 * log_a))
    if prompt:
        mult = jnp.where(jnp.logical_and(j == 0, row == 0), 1.0, mult)
    b = mult * (ig * conv)
    if not prompt:
        b = b + a * h0_ref[...]

    n_grp = rows // SUBLANES
    a = a.reshape(n_grp, SUBLANES, D)
    b = b.reshape(n_grp, SUBLANES, D)
    sub = lax.broadcasted_iota(jnp.int32, (1, SUBLANES, D), 1)
    d = 1
    while d < SUBLANES:
        keep = sub >= d
        a_sh = jnp.where(keep, pltpu.roll(a, d, 1), 1.0)
        b_sh = jnp.where(keep, pltpu.roll(b, d, 1), 0.0)
        b = b + a * b_sh
        a = a * a_sh
        d *= 2

    if prompt:
        carry = hcar_ref[0:1]
        parts = []
        for r in range(n_grp):
            part = b[r] + a[r] * carry
            carry = part[SUBLANES - 1:SUBLANES]
            parts.append(part)
        hs = jnp.concatenate(parts, axis=0)
        hcar_ref[0:1] = carry
        h_out_ref[0] = parts[-1]
    else:
        hs = b.reshape(rows, D)
        hs_out_ref[...] = hs
    y = (hs * _gelu_tanh(gate_br)).astype(BF16)
    out = _dot(y, wout_ref[...])
    o_ref[...] = x + _gated(_rms(out, gpost_ref[...]), gt_ref[...], g, rt)


def _odd(t, prompt, x, mod, mcol, l, e, p, prev=None, h0=None):
    vec = t.sel((None, 1, D), (e, 0, 0))
    rgw = t.sel((None, RG_BLOCKS, RG_BW, RG_BW), (e, 0, 0, 0))
    in_specs = [t.tok(D), t.mod(mcol), t.mod(mcol + 1), t.mod(mcol + 2),
                t.sel((None, None, 1, D), (l, 1, 0, 0)), t.sel((None, None, 1, D), (l, 1, 0, 0)),
                t.sel((None, D, 2 * D), (e, 0, 0)), t.sel((None, CONV_W, D), (e, 0, 0)), vec,
                rgw, vec, rgw, vec, vec, t.sel((None, D, D), (e, 0, 0))]
    args = [x, mod, mod, mod, p["g_pre"], p["g_post"], p["w_in_odd"], p["conv_w"], p["conv_b"],
            p["w_rg_a"], p["b_rg_a"], p["w_rg_x"], p["b_rg_x"], p["rg_lambda"], p["w_out_odd"]]
    if prompt:
        last = pl.BlockSpec((1, SUBLANES, D), lambda i, j: (i, 0, 0))
        out_shape = [jax.ShapeDtypeStruct((t.n_tok, D), F32),
                     jax.ShapeDtypeStruct((t.G, SUBLANES, D), F32),
                     jax.ShapeDtypeStruct((t.G, SUBLANES, D), F32)]
        out_specs = [t.tok(D), last, last]
        scratch = [pltpu.VMEM((SUBLANES, D), F32), pltpu.VMEM((SUBLANES, D), F32)]
    else:
        in_specs += [t.tok(D), t.tok(D)]
        args += [prev, h0]
        out_shape = [jax.ShapeDtypeStruct((t.n_tok, D), F32)] * 3
        out_specs = [t.tok(D)] * 3
        scratch = []
    return pl.pallas_call(
        functools.partial(_odd_body, g=t.g, rt=t.rt, prompt=prompt),
        out_shape=out_shape, grid=t.grid, in_specs=in_specs, out_specs=out_specs,
        scratch_shapes=scratch, compiler_params=_cparams(2),
        name="odd_prompt" if prompt else "odd_sample",
    )(*args)


def _rope_tables(pos):
    posf = pos.astype(F32)[:, None]
    inv_a = jnp.power(jnp.float32(ROPE_THETA), -jnp.arange(0, ROT_DIM, 2, dtype=F32) / ROT_DIM)
    ang = posf * inv_a[None, :]
    cos, sin = jnp.cos(ang), jnp.sin(ang)
    n = pos.shape[0]
    half = ROT_DIM // 2
    rest = A_DH - ROT_DIM
    one, zero = jnp.ones((n, rest), F32), jnp.zeros((n, rest), F32)
    zh = jnp.zeros((n, half), F32)
    cos_h = jnp.concatenate([cos, cos, one], axis=1)
    s1_h = jnp.concatenate([-sin, zh, zero], axis=1)
    s2_h = jnp.concatenate([zh, sin, zero], axis=1)
    both = lambda a: jnp.concatenate([a, a], axis=1)
    wi_scale = jnp.full((n, A_DH), IDX_HEADS ** -0.5, F32)
    z64 = jnp.zeros((n, A_DH), F32)
    rope_a = jnp.stack([both(cos_h), both(s1_h), both(s2_h),
                        jnp.concatenate([cos_h, wi_scale], axis=1),
                        jnp.concatenate([s1_h, z64], axis=1),
                        jnp.concatenate([s2_h, z64], axis=1)])
    inv_r = jnp.power(jnp.float32(RET_THETA), -jnp.linspace(0.0, 1.0, RET_DH // 2, dtype=F32))
    ang_r = posf * inv_r[None, :]
    cr, sr = jnp.cos(ang_r), jnp.sin(ang_r)
    rope_r = jnp.stack([jnp.concatenate([cr, cr], axis=1), jnp.concatenate([-sr, sr], axis=1)])
    return rope_a, rope_r


def _pack_w_in_even(w):
    o = np.cumsum((0, 512, 128, 128, 256, IDX_HEADS, IDX_DIM, 512, 512, 512, 512))
    pad = jnp.zeros(w.shape[:2] + (LANES - IDX_DIM - IDX_HEADS,), w.dtype)
    return jnp.concatenate([w[..., o[0]:o[4]], w[..., o[5]:o[6]], w[..., o[4]:o[5]], pad, w[..., o[6]:o[10]]],
                           axis=-1).astype(BF16)


def kernel(x_prompt, x_sample, cache_k, cache_v, cache_kidx, state_ret, state_conv, state_rglru, page_table,
           c_prompt, c_sample, w_ada, b_ada, g_pre, g_post, w_ffn_gate, w_ffn_up, w_ffn_down,
           w_in_even, w_out_even, w_in_odd, w_out_odd, conv_w, conv_b, w_rg_a, b_rg_a, w_rg_x, b_rg_x, rg_lambda):
    B, L, _ = x_prompt.shape
    Bs, T, _ = x_sample.shape
    n_pages = page_table.shape[1]
    past = n_pages * cache_k.shape[2]
    n_phys = cache_k.shape[1]
    assert cache_k.shape[2] == PAGE and T == SUBLANES

    tp = _Tiling(B, L, 1, 512)
    ts = _Tiling(Bs, T, 64, T)
    tp_odd = _Tiling(B, L, 1, 256)
    ts_odd = _Tiling(Bs, T, 32, T)

    vec3 = lambda a: a.reshape(a.shape[0], 1, a.shape[1])
    p = {
        "g_pre": g_pre.reshape(DEPTH, 3, 1, D), "g_post": g_post.reshape(DEPTH, 3, 1, D),
        "wg": w_ffn_gate.astype(BF16), "wu": w_ffn_up.astype(BF16), "wd": w_ffn_down.astype(BF16),
        "w_in_even": _pack_w_in_even(w_in_even), "w_out_even": w_out_even.astype(BF16),
        "w_in_odd": w_in_odd.astype(BF16), "w_out_odd": w_out_odd.astype(BF16),
        "conv_w": conv_w, "conv_b": vec3(conv_b),
        "w_rg_a": w_rg_a.astype(BF16), "b_rg_a": vec3(b_rg_a),
        "w_rg_x": w_rg_x.astype(BF16), "b_rg_x": vec3(b_rg_x), "rg_lambda": vec3(rg_lambda),
    }
    k_t = jnp.transpose(cache_k, (0, 1, 3, 4, 2)).reshape(cache_k.shape[0], n_phys, LANES, PAGE)
    v_t = jnp.transpose(cache_v, (0, 1, 3, 4, 2)).reshape(cache_v.shape[0], n_phys, LANES, PAGE)
    kidx_t = jnp.transpose(cache_kidx, (0, 1, 3, 2))

    mod_all = _ada(jnp.concatenate([c_prompt, c_sample], axis=0), w_ada, b_ada)
    mod_p = mod_all[:, :B].reshape(DEPTH, B, 1, N_MOD * D)
    mod_s = mod_all[:, B:].reshape(DEPTH, Bs, 1, N_MOD * D)

    rope_a_p, rope_r_p = _rope_tables(jnp.arange(L, dtype=jnp.int32))
    rope_a_s, rope_r_s = _rope_tables(jnp.tile(past + jnp.arange(T, dtype=jnp.int32), ts.g))
    rope_map_p = lambda i, j: (0, j, 0)
    rope_map_s = lambda i, j: (0, 0, 0)

    xp = x_prompt.reshape(B * L, D)
    xs = x_sample.reshape(Bs * T, D)
    ks, vs, kis, rets, convs, hs = ([[], []] for _ in range(6))

    for l in range(DEPTH):
        e = l // 2
        groups = ((0, tp, xp, mod_p[l]), (1, ts, xs, mod_s[l]))
        new_x = []
        for gi, t, x, mod in groups:
            x = _ffn(t, x, mod, 0, l, 0, 0, p)
            if l % 2 == 0:
                if gi == 0:
                    q, k, v, qi, kw, kt, kit, rq, rk, rv, rg = _even_in(
                        t, True, x, mod, 3, l, e, p, rope_a_p, rope_r_p, rope_map_p)
                    o_a = _dsa_prompt(B, L, q, qi, kw, k, v)
                    o_r, s_new = _ret_prompt(B, L, rq, rk, rv, rg)
                    ks[0].append(jnp.transpose(kt.reshape(B, A_KV, A_DH, L), (0, 3, 1, 2)))
                    vs[0].append(v.reshape(B, L, A_KV, A_DH))
                    kis[0].append(jnp.transpose(kit, (0, 2, 1)))
                else:
                    q, k, v, qi, kw, rq, rk, rv, rg = _even_in(
                        t, False, x, mod, 3, l, e, p, rope_a_s, rope_r_s, rope_map_s)
                    scores = _dsa_s_scores(Bs, T, e, page_table, qi, kw, kidx_t)
                    bias = _dsa_s_select(Bs * T, T, past, scores)
                    o_a = _dsa_s_attn(Bs, T, e, page_table, q, k, v, bias, k_t, v_t)
                    o_r, s_new = _ret_sample(Bs, T, e, rq, rk, rv, rg, state_ret)
                    ks[1].append(k.reshape(Bs, T, A_KV, A_DH))
                    vs[1].append(v.reshape(Bs, T, A_KV, A_DH))
                    kis[1].append(kw[:, :IDX_DIM].reshape(Bs, T, IDX_DIM))
                rets[gi].append(s_new)
                x = _even_out(t, x, o_a, o_r, mod, 5, l, e, p)
            else:
                if gi == 0:
                    x, tail, hlast = _odd(tp_odd, True, x, mod, 3, l, e, p)
                    convs[0].append(tail[:, SUBLANES - (CONV_W - 1):])
                    hs[0].append(hlast[:, SUBLANES - 1])
                else:
                    prev = jnp.pad(state_conv[e], ((0, 0), (SUBLANES - (CONV_W - 1), 0), (0, 0))).reshape(Bs * T, D)
                    h0 = jnp.pad(state_rglru[e][:, None, :], ((0, 0), (0, T - 1), (0, 0))).reshape(Bs * T, D)
                    x, xb, hseq = _odd(ts_odd, False, x, mod, 3, l, e, p, prev=prev, h0=h0)
                    convs[1].append(xb.reshape(Bs, T, D)[:, T - (CONV_W - 1):])
                    hs[1].append(hseq.reshape(Bs, T, D)[:, T - 1])
            x = _ffn(t, x, mod, 6, l, 1, 2, p)
            new_x.append(x)
        xp, xs = new_x

    st = lambda lists, gi: jnp.stack(lists[gi])
    return (xp.reshape(B, L, D), xs.reshape(Bs, T, D),
            st(ks, 0), st(vs, 0), st(kis, 0), st(rets, 0), st(convs, 0), st(hs, 0),
            st(ks, 1), st(vs, 1), st(kis, 1), st(rets, 1), st(convs, 1), st(hs, 1))
```

```python
import functools
import math

import jax
import jax.numpy as jnp
import numpy as np
from jax import lax
from jax.experimental import pallas as pl
from jax.experimental.pallas import tpu as pltpu

F32 = jnp.float32
BF16 = jnp.bfloat16

D = 1024
DEPTH = 4
N_MOD = 9
A_HEADS = 8
A_KV = 2
A_DH = 64
ROT_DIM = 16
ROPE_THETA = 500000.0
IDX_HEADS = 4
IDX_DIM = 64
TOPK_MAX = 256
RET_HEADS = 4
RET_DH = 128
RET_THETA = 10000.0
RG_BLOCKS = 8
RG_BW = 128
CONV_W = 4
RG_C = 8.0
D_FF = 2816
EPS = 1e-6
P_IN = 3200
PAGE = 128

LANES = 128
SUBLANES = 8
VMEM_LIMIT = 56 * 1024 * 1024

NEG_INF = float("-inf")
INT_MIN = -(2 ** 31)


def _cparams(n_axes, vmem=VMEM_LIMIT):
    return pltpu.CompilerParams(dimension_semantics=("arbitrary",) * n_axes, vmem_limit_bytes=vmem)


def _dot(a, b):
    return jnp.dot(a, b, preferred_element_type=F32)


def _dot_nt(a, b):
    return lax.dot_general(a, b, (((1,), (1,)), ((), ())), preferred_element_type=F32)


def _dot_tn(a, b):
    return lax.dot_general(a, b, (((0,), (0,)), ((), ())), preferred_element_type=F32)


def _sigmoid(x):
    return 1.0 / (1.0 + jnp.exp(-x))


def _silu(x):
    return x * _sigmoid(x)


def _gelu_tanh(x):
    return 0.5 * x * (1.0 + jnp.tanh(math.sqrt(2.0 / math.pi) * (x + 0.044715 * (x * x * x))))


def _rms(x, g=None):
    y = x * lax.rsqrt(jnp.mean(x * x, axis=-1, keepdims=True) + EPS)
    return y if g is None else y * g


def _prenorm(x, g_pre, shift, scale, g, rt):
    h = _rms(x, g_pre)
    if g == 1:
        return h * (1.0 + scale[0]) + shift[0]
    h3 = h.reshape(g, rt, D)
    return (h3 * (1.0 + scale) + shift).reshape(g * rt, D)


def _gated(y, gate, g, rt):
    if g == 1:
        return y * gate[0]
    return (y.reshape(g, rt, D) * gate).reshape(g * rt, D)


class _Tiling:
    def __init__(self, G, R, g, rt):
        assert G % g == 0 and R % rt == 0 and (g == 1 or rt == R)
        self.G, self.R, self.g, self.rt = G, R, g, rt
        self.nr = R // rt
        self.rows = g * rt
        self.grid = (G // g, self.nr)
        self.n_tok = G * R

    def tok(self, width):
        nr = self.nr
        return pl.BlockSpec((self.rows, width), lambda i, j: (i * nr + j, 0))

    def mod(self, c):
        return pl.BlockSpec((self.g, 1, D), lambda i, j: (i, 0, c))

    def sel(self, block, idx):
        return pl.BlockSpec(block, lambda i, j: idx, pipeline_mode=pl.Buffered(1))


def _ada_body(c_ref, w_ref, b_ref, o_ref):
    c = _silu(c_ref[...]).astype(BF16)
    o_ref[0] = _dot(c, w_ref[0].astype(BF16)) + b_ref[0]


def _ada(c_all, w_ada, b_ada):
    n = c_all.shape[0]
    return pl.pallas_call(
        _ada_body,
        out_shape=jax.ShapeDtypeStruct((DEPTH, n, N_MOD * D), F32),
        grid=(DEPTH, N_MOD),
        in_specs=[pl.BlockSpec((n, D), lambda l, j: (0, 0)),
                  pl.BlockSpec((1, D, D), lambda l, j: (l, 0, j)),
                  pl.BlockSpec((1, 1, D), lambda l, j: (l, 0, j))],
        out_specs=pl.BlockSpec((1, n, D), lambda l, j: (l, 0, j)),
        compiler_params=_cparams(2),
        name="ada",
    )(c_all, w_ada, b_ada.reshape(DEPTH, 1, N_MOD * D))


def _ffn_body(x_ref, sh_ref, sc_ref, gt_ref, gpre_ref, gpost_ref, wg_ref, wu_ref, wd_ref, o_ref, *, g, rt):
    x = x_ref[...]
    h = _prenorm(x, gpre_ref[...], sh_ref[...], sc_ref[...], g, rt).astype(BF16)
    a = _dot(h, wg_ref[...])
    u = _dot(h, wu_ref[...])
    act = (_silu(a) * u).astype(BF16)
    f = _dot(act, wd_ref[...])
    y = _rms(f, gpost_ref[...])
    o_ref[...] = x + 0.5 * _gated(y, gt_ref[...], g, rt)


def _ffn(t, x, mod, mcol, l, s, n, p):
    return pl.pallas_call(
        functools.partial(_ffn_body, g=t.g, rt=t.rt),
        out_shape=jax.ShapeDtypeStruct((t.n_tok, D), F32),
        grid=t.grid,
        in_specs=[t.tok(D), t.mod(mcol), t.mod(mcol + 1), t.mod(mcol + 2),
                  t.sel((None, None, 1, D), (l, n, 0, 0)), t.sel((None, None, 1, D), (l, n, 0, 0)),
                  t.sel((None, None, D, D_FF), (l, s, 0, 0)), t.sel((None, None, D, D_FF), (l, s, 0, 0)),
                  t.sel((None, None, D_FF, D), (l, s, 0, 0))],
        out_specs=t.tok(D),
        compiler_params=_cparams(2),
        name="ffn",
    )(x, mod, mod, mod, p["g_pre"], p["g_post"], p["wg"], p["wu"], p["wd"])


def _even_in_body(*refs, g, rt, prompt):
    x_ref, sh_ref, sc_ref, gpre_ref, w_ref, ra_ref, rr_ref = refs[:7]
    q_ref, k_ref, v_ref, qi_ref, kw_ref = refs[7:12]
    if prompt:
        kit_ref, rq_ref, rk_ref, rv_ref, rg_ref = refs[12:]
    else:
        rq_ref, rk_ref, rv_ref, rg_ref = refs[12:]
    h = _prenorm(x_ref[...], gpre_ref[...], sh_ref[...], sc_ref[...], g, rt).astype(BF16)
    proj = _dot(h, w_ref[...])

    def sec(c):
        return proj[:, c * LANES:(c + 1) * LANES]

    def rope_a(t, v):
        return (t * ra_ref[3 * v] + pltpu.roll(t, LANES - ROT_DIM // 2, 1) * ra_ref[3 * v + 1]
                + pltpu.roll(t, ROT_DIM // 2, 1) * ra_ref[3 * v + 2])

    def rope_r(t):
        return t * rr_ref[0] + pltpu.roll(t, RET_DH // 2, 1) * rr_ref[1]

    for c in range(4):
        q_ref[:, c * LANES:(c + 1) * LANES] = rope_a(sec(c), 0) * (A_DH ** -0.5)
    k = rope_a(sec(4), 0)
    v_ref[...] = sec(5)
    for c in range(2):
        qi_ref[:, c * LANES:(c + 1) * LANES] = rope_a(sec(6 + c), 0) * (IDX_DIM ** -0.5)
    kw = rope_a(sec(8), 1)
    kw_ref[...] = kw
    if prompt:
        k_ref[...] = k.T
        kit_ref[...] = kw.T[0:IDX_DIM]
    else:
        k_ref[...] = k
    for c in range(4):
        rq_ref[:, c * LANES:(c + 1) * LANES] = rope_r(sec(9 + c))
        rk_ref[:, c * LANES:(c + 1) * LANES] = rope_r(sec(13 + c) * (RET_DH ** -0.5))
        rv_ref[:, c * LANES:(c + 1) * LANES] = sec(17 + c)
        rg_ref[:, c * LANES:(c + 1) * LANES] = sec(21 + c)


def _even_in(t, prompt, x, mod, mcol, l, e, p, rope_a, rope_r, rope_map):
    tokw = lambda w: (jax.ShapeDtypeStruct((t.n_tok, w), F32), t.tok(w))
    seq_t = lambda w: (jax.ShapeDtypeStruct((t.G, w, t.R), F32),
                       pl.BlockSpec((None, w, t.rt), lambda i, j: (i, 0, j)))
    outs = [tokw(512), seq_t(LANES) if prompt else tokw(LANES), tokw(LANES), tokw(256), tokw(LANES)]
    if prompt:
        outs.append(seq_t(IDX_DIM))
    outs += [tokw(512)] * 4
    return pl.pallas_call(
        functools.partial(_even_in_body, g=t.g, rt=t.rt, prompt=prompt),
        out_shape=[o[0] for o in outs],
        grid=t.grid,
        in_specs=[t.tok(D), t.mod(mcol), t.mod(mcol + 1), t.sel((None, None, 1, D), (l, 1, 0, 0)),
                  t.sel((None, D, P_IN), (e, 0, 0)),
                  pl.BlockSpec((6, t.rows, LANES), rope_map),
                  pl.BlockSpec((2, t.rows, LANES), rope_map)],
        out_specs=[o[1] for o in outs],
        compiler_params=_cparams(2),
        name="even_in_p" if prompt else "even_in_s",
    )(x, mod, mod, p["g_pre"], p["w_in_even"], rope_a, rope_r)


def _sortable(s):
    s = jnp.where(s == 0.0, 0.0, s)
    b = lax.bitcast_convert_type(s, jnp.int32)
    return b ^ ((b >> 31) & 0x7FFFFFFF)


def _topk_bias_t(scores_t, visible_t, k):
    s_len, n_q = scores_t.shape
    key = _sortable(scores_t)
    fold = 8 * SUBLANES
    assert s_len % fold == 0

    def count(mask):
        part = jnp.sum(jnp.where(mask, 1.0, 0.0).reshape(s_len // fold, fold, n_q), axis=0)
        return jnp.sum(part, axis=0, keepdims=True)

    def search(it, thr):
        cand = thr + jnp.left_shift(jnp.int32(1), 31 - it)
        return jnp.where(count(key >= cand) >= k, cand, thr)

    thr = lax.fori_loop(0, 32, search, jnp.full((1, n_q), INT_MIN, jnp.int32))
    ge = key >= thr
    cnt_ge = count(ge)

    def no_ties():
        return jnp.where(visible_t, jnp.where(ge, 0.0, NEG_INF), NEG_INF)

    def ties():
        need = k - count(key > thr)
        tri = jnp.where(lax.broadcasted_iota(jnp.int32, (LANES, LANES), 0)
                        >= lax.broadcasted_iota(jnp.int32, (LANES, LANES), 1), 1.0, 0.0).astype(BF16)
        off = jnp.zeros((1, n_q), F32)
        pieces = []
        for c in range(s_len // LANES):
            sl = slice(c * LANES, (c + 1) * LANES)
            kc = key[sl, :]
            eq = kc == thr
            pc = _dot(tri, jnp.where(eq, 1.0, 0.0).astype(BF16)) + off
            off = pc[LANES - 1:LANES, :]
            tie_ok = jnp.where(eq, jnp.where(pc <= need, 0.0, NEG_INF), NEG_INF)
            b = jnp.where(kc > thr, 0.0, tie_ok)
            pieces.append(jnp.where(visible_t[sl, :], b, NEG_INF))
        return jnp.concatenate(pieces, axis=0)

    return lax.cond(jnp.max(cnt_ge) > k, ties, no_ties)


def _softmax(logits):
    m = jnp.max(logits, axis=-1, keepdims=True)
    p = jnp.exp(logits - m)
    return p, jnp.sum(p, axis=-1, keepdims=True)


def _dsa_p_body(q_ref, qi_ref, kw_ref, kt_ref, v_ref, o_ref, *, tq, topk):
    s_len = q_ref.shape[0]
    rep = A_HEADS // A_KV
    kt = kt_ref[...].astype(BF16)
    ki = kw_ref[:, 0:IDX_DIM].astype(BF16)
    v = v_ref[...].astype(BF16)
    for i in range(s_len // tq):
        rows = slice(i * tq, (i + 1) * tq)
        s_vis = (i + 1) * tq
        qit = qi_ref[rows, :].T.astype(BF16)
        wit = kw_ref[rows, :].T
        sc = jnp.zeros((s_vis, tq), F32)
        for h in range(IDX_HEADS):
            raw = _dot(ki[:s_vis], qit[h * IDX_DIM:(h + 1) * IDX_DIM, :])
            sc = sc + jnp.maximum(raw, 0.0) * wit[IDX_DIM + h:IDX_DIM + h + 1, :]
        kpos = lax.broadcasted_iota(jnp.int32, (s_vis, 1), 0)
        qpos = i * tq + lax.broadcasted_iota(jnp.int32, (1, tq), 1)
        visible = kpos <= qpos
        bias = _topk_bias_t(jnp.where(visible, sc, NEG_INF), visible, topk).T
        q = q_ref[rows, :].astype(BF16)
        for hp in range(A_HEADS // 2):
            outs = []
            for h in (2 * hp, 2 * hp + 1):
                gsl = slice((h // rep) * A_DH, (h // rep + 1) * A_DH)
                p, l = _softmax(_dot(q[:, h * A_DH:(h + 1) * A_DH], kt[gsl, :s_vis]) + bias)
                outs.append(_dot(p.astype(BF16), v[:s_vis, gsl]) / l)
            o_ref[rows, hp * LANES:(hp + 1) * LANES] = jnp.concatenate(outs, axis=-1)


def _dsa_prompt(B, L, q, qi, kw, kt, v, tq=256):
    tok = lambda w: pl.BlockSpec((L, w), lambda b: (b, 0))
    seq_t = lambda w: pl.BlockSpec((None, w, L), lambda b: (b, 0, 0))
    return pl.pallas_call(
        functools.partial(_dsa_p_body, tq=tq, topk=min(TOPK_MAX, L // 4)),
        out_shape=jax.ShapeDtypeStruct((B * L, A_HEADS * A_DH), F32),
        grid=(B,),
        in_specs=[tok(512), tok(256), tok(LANES), seq_t(LANES), tok(LANES)],
        out_specs=tok(512),
        compiler_params=_cparams(1),
        name="dsa_prompt",
    )(q, qi, kw, kt, v)


SEQ_PER_STEP = 4


def _page_gather(pt_ref, pool_ref, buf_ref, sem_ref, e, n_pages):
    b = pl.program_id(0)

    def copies(step, slot):
        return [pltpu.make_async_copy(pool_ref.at[e, pt_ref[step * SEQ_PER_STEP + s, p]],
                                      buf_ref.at[slot, s, :, pl.ds(p * PAGE, PAGE)],
                                      sem_ref.at[slot])
                for s in range(SEQ_PER_STEP) for p in range(n_pages)]

    @pl.when(b == 0)
    def _():
        for c in copies(0, 0):
            c.start()

    @pl.when(b + 1 < pl.num_programs(0))
    def _():
        for c in copies(b + 1, (b + 1) % 2):
            c.start()

    slot = b % 2
    for c in copies(b, slot):
        c.wait()
    return slot


def _dsa_s_scores_body(pt_ref, qi_ref, kw_ref, pool_ref, o_ref, buf_ref, sem_ref, *, e, n_pages, t):
    slot = _page_gather(pt_ref, pool_ref, buf_ref, sem_ref, e, n_pages)
    past = n_pages * PAGE
    lane = lax.broadcasted_iota(jnp.int32, (t, PAGE), 1)
    row = lax.broadcasted_iota(jnp.int32, (t, PAGE), 0)
    for s in range(SEQ_PER_STEP):
        rows = slice(s * t, (s + 1) * t)
        qi = qi_ref[rows, :]
        qs = jnp.concatenate([qi[:, h * IDX_DIM:(h + 1) * IDX_DIM] for h in range(IDX_HEADS)], axis=0).astype(BF16)
        kw = kw_ref[rows, :]
        wi = kw[:, IDX_DIM:LANES]
        ki_new = jnp.concatenate([kw[:, 0:IDX_DIM], jnp.zeros((PAGE - t, IDX_DIM), F32)], axis=0).astype(BF16)
        kit = buf_ref[slot, s].astype(BF16)

        def weigh(raw):
            raw = jnp.maximum(raw, 0.0)
            sc = raw[0:t] * wi[:, 0:1]
            for h in range(1, IDX_HEADS):
                sc = sc + raw[h * t:(h + 1) * t] * wi[:, h:h + 1]
            return sc

        o_ref[rows, :past] = weigh(_dot(qs, kit))
        o_ref[rows, past:] = jnp.where(lane <= row, weigh(_dot_nt(qs, ki_new)), NEG_INF)


def _dsa_s_scores(Bs, T, e, page_table, qi, kw, kidx_t):
    n_pages = page_table.shape[1]
    tok = lambda w: pl.BlockSpec((SEQ_PER_STEP * T, w), lambda b, pt: (b, 0))
    s_pad = (n_pages + 1) * PAGE
    return pl.pallas_call(
        functools.partial(_dsa_s_scores_body, e=e, n_pages=n_pages, t=T),
        out_shape=jax.ShapeDtypeStruct((Bs * T, s_pad), F32),
        grid_spec=pltpu.PrefetchScalarGridSpec(
            num_scalar_prefetch=1, grid=(Bs // SEQ_PER_STEP,),
            in_specs=[tok(256), tok(LANES), pl.BlockSpec(memory_space=pl.ANY)],
            out_specs=tok(s_pad),
            scratch_shapes=[pltpu.VMEM((2, SEQ_PER_STEP, IDX_DIM, n_pages * PAGE), F32),
                            pltpu.SemaphoreType.DMA((2,))]),
        compiler_params=_cparams(1),
        name="dsa_s_scores",
    )(page_table, qi, kw, kidx_t)


def _dsa_s_select_body(sc_ref, o_ref, *, t, past, topk):
    rows, s_pad = sc_ref.shape
    tpos = jnp.bitwise_and(lax.broadcasted_iota(jnp.int32, (1, rows), 1), t - 1)
    kpos = lax.broadcasted_iota(jnp.int32, (s_pad, 1), 0)
    visible = kpos <= past + tpos
    o_ref[...] = _topk_bias_t(sc_ref[...].T, visible, topk).T


def _dsa_s_select(n_rows, T, past, scores, rows=256):
    s_pad = scores.shape[1]
    spec = pl.BlockSpec((rows, s_pad), lambda i: (i, 0))
    return pl.pallas_call(
        functools.partial(_dsa_s_select_body, t=T, past=past, topk=min(TOPK_MAX, (past + T) // 4)),
        out_shape=jax.ShapeDtypeStruct((n_rows, s_pad), F32),
        grid=(n_rows // rows,),
        in_specs=[spec], out_specs=spec,
        compiler_params=_cparams(1),
        name="dsa_s_select",
    )(scores)


def _dsa_s_attn_body(pt_ref, q_ref, kn_ref, vn_ref, b_ref, kpool_ref, vpool_ref, o_ref,
                     kbuf_ref, vbuf_ref, ksem_ref, vsem_ref, *, e, n_pages, t):
    slot = _page_gather(pt_ref, kpool_ref, kbuf_ref, ksem_ref, e, n_pages)
    _page_gather(pt_ref, vpool_ref, vbuf_ref, vsem_ref, e, n_pages)
    past = n_pages * PAGE
    rep = A_HEADS // A_KV
    pad = jnp.zeros((PAGE - t, LANES), F32)
    for s in range(SEQ_PER_STEP):
        rows = slice(s * t, (s + 1) * t)
        kt = kbuf_ref[slot, s].astype(BF16)
        vt = vbuf_ref[slot, s].astype(BF16)
        k_new = jnp.concatenate([kn_ref[rows, :], pad], axis=0).astype(BF16)
        v_new = jnp.concatenate([vn_ref[rows, :], pad], axis=0).astype(BF16)
        q = q_ref[rows, :].astype(BF16)
        bias = jnp.concatenate([b_ref[rows, :]] * rep, axis=0)
        outs = []
        for gi in range(A_KV):
            gsl = slice(gi * A_DH, (gi + 1) * A_DH)
            qg = jnp.concatenate([q[:, (gi * rep + r) * A_DH:(gi * rep + r + 1) * A_DH] for r in range(rep)], axis=0)
            logits = jnp.concatenate([_dot(qg, kt[gsl, :]), _dot_nt(qg, k_new[:, gsl])], axis=-1) + bias
            p, l = _softmax(logits)
            p = p.astype(BF16)
            og = (_dot_nt(p[:, :past], vt[gsl, :]) + _dot(p[:, past:], v_new[:, gsl])) / l
            outs += [og[r * t:(r + 1) * t] for r in range(rep)]
        for hp in range(A_HEADS // 2):
            o_ref[rows, hp * LANES:(hp + 1) * LANES] = jnp.concatenate(outs[2 * hp:2 * hp + 2], axis=-1)


def _dsa_s_attn(Bs, T, e, page_table, q, k_new, v_new, bias, k_t, v_t):
    n_pages = page_table.shape[1]
    tok = lambda w: pl.BlockSpec((SEQ_PER_STEP * T, w), lambda b, pt: (b, 0))
    pool = pl.BlockSpec(memory_space=pl.ANY)
    buf = pltpu.VMEM((2, SEQ_PER_STEP, LANES, n_pages * PAGE), F32)
    return pl.pallas_call(
        functools.partial(_dsa_s_attn_body, e=e, n_pages=n_pages, t=T),
        out_shape=jax.ShapeDtypeStruct((Bs * T, A_HEADS * A_DH), F32),
        grid_spec=pltpu.PrefetchScalarGridSpec(
            num_scalar_prefetch=1, grid=(Bs // SEQ_PER_STEP,),
            in_specs=[tok(512), tok(LANES), tok(LANES), tok(bias.shape[1]), pool, pool],
            out_specs=tok(512),
            scratch_shapes=[buf, buf, pltpu.SemaphoreType.DMA((2,)), pltpu.SemaphoreType.DMA((2,))]),
        compiler_params=_cparams(1),
        name="dsa_s_attn",
    )(page_table, q, k_new, v_new, bias, k_t, v_t)


def _ret_log_gamma(h):
    return math.log1p(-(2.0 ** (-5.0 - h)))


def _ret_chunk(q, k, v, rg, s, h, c):
    lg = _ret_log_gamma(h)
    ri = lax.broadcasted_iota(jnp.int32, (c, c), 0)
    ci = lax.broadcasted_iota(jnp.int32, (c, c), 1)
    diff = (ri - ci).astype(F32)
    decay = jnp.where(diff >= 0, jnp.exp(jnp.maximum(diff, 0.0) * lg), 0.0)
    idx = lax.broadcasted_iota(jnp.int32, (c, 1), 0).astype(F32)
    q_dec = jnp.exp((idx + 1.0) * lg)
    k_dec = jnp.exp((c - 1.0 - idx) * lg)
    c_dec = math.exp(c * lg)
    qb = q.astype(BF16)
    att = _dot_nt(qb, k.astype(BF16)) * decay
    inner = _dot(att.astype(BF16), v.astype(BF16))
    cross = _dot(qb, s.astype(BF16)) * q_dec
    s_new = s * c_dec + _dot_tn((k * k_dec).astype(BF16), v.astype(BF16))
    return _rms(inner + cross) * _silu(rg), s_new


def _ret_p_body(rq_ref, rk_ref, rv_ref, rg_ref, o_ref, so_ref, s_ref, *, c):
    j = pl.program_id(1)

    @pl.when(j == 0)
    def _():
        s_ref[...] = jnp.zeros_like(s_ref)

    for h in range(RET_HEADS):
        sl = slice(h * RET_DH, (h + 1) * RET_DH)
        o, s_new = _ret_chunk(rq_ref[:, sl], rk_ref[:, sl], rv_ref[:, sl], rg_ref[:, sl], s_ref[h], h, c)
        o_ref[:, sl] = o
        s_ref[h] = s_new
    so_ref[0] = s_ref[...]


def _ret_prompt(B, L, rq, rk, rv, rg, c=256):
    nc = L // c
    tok = pl.BlockSpec((c, 512), lambda b, j: (b * nc + j, 0))
    return pl.pallas_call(
        functools.partial(_ret_p_body, c=c),
        out_shape=[jax.ShapeDtypeStruct((B * L, 512), F32),
                   jax.ShapeDtypeStruct((B, RET_HEADS, RET_DH, RET_DH), F32)],
        grid=(B, nc),
        in_specs=[tok, tok, tok, tok],
        out_specs=[tok, pl.BlockSpec((1, RET_HEADS, RET_DH, RET_DH), lambda b, j: (b, 0, 0, 0))],
        scratch_shapes=[pltpu.VMEM((RET_HEADS, RET_DH, RET_DH), F32)],
        compiler_params=_cparams(2),
        name="ret_prompt",
    )(rq, rk, rv, rg)


def _ret_s_body(rq_ref, rk_ref, rv_ref, rg_ref, s0_ref, o_ref, so_ref, *, t):
    for s in range(SEQ_PER_STEP):
        rows = slice(s * t, (s + 1) * t)
        for h in range(RET_HEADS):
            sl = slice(h * RET_DH, (h + 1) * RET_DH)
            o, s_new = _ret_chunk(rq_ref[rows, sl], rk_ref[rows, sl], rv_ref[rows, sl], rg_ref[rows, sl],
                                  s0_ref[s, h], h, t)
            o_ref[rows, sl] = o
            so_ref[s, h] = s_new


def _ret_sample(Bs, T, e, rq, rk, rv, rg, state_ret):
    tok = pl.BlockSpec((SEQ_PER_STEP * T, 512), lambda b: (b, 0))
    return pl.pallas_call(
        functools.partial(_ret_s_body, t=T),
        out_shape=[jax.ShapeDtypeStruct((Bs * T, 512), F32),
                   jax.ShapeDtypeStruct((Bs, RET_HEADS, RET_DH, RET_DH), F32)],
        grid=(Bs // SEQ_PER_STEP,),
        in_specs=[tok, tok, tok, tok,
                  pl.BlockSpec((None, SEQ_PER_STEP, RET_HEADS, RET_DH, RET_DH), lambda b: (e, b, 0, 0, 0))],
        out_specs=[tok, pl.BlockSpec((SEQ_PER_STEP, RET_HEADS, RET_DH, RET_DH), lambda b: (b, 0, 0, 0))],
        compiler_params=_cparams(1),
        name="ret_sample",
    )(rq, rk, rv, rg, state_ret)


def _even_out_body(x_ref, oa_ref, or_ref, gt_ref, gpost_ref, wa_ref, wr_ref, o_ref, *, g, rt):
    y = _dot(oa_ref[...].astype(BF16), wa_ref[...]) + _dot(or_ref[...].astype(BF16), wr_ref[...])
    o_ref[...] = x_ref[...] + _gated(_rms(y, gpost_ref[...]), gt_ref[...], g, rt)


def _even_out(t, x, o_a, o_r, mod, mcol, l, e, p):
    half = A_HEADS * A_DH
    return pl.pallas_call(
        functools.partial(_even_out_body, g=t.g, rt=t.rt),
        out_shape=jax.ShapeDtypeStruct((t.n_tok, D), F32),
        grid=t.grid,
        in_specs=[t.tok(D), t.tok(half), t.tok(half), t.mod(mcol), t.sel((None, None, 1, D), (l, 1, 0, 0)),
                  t.sel((None, half, D), (e, 0, 0)), t.sel((None, half, D), (e, 1, 0))],
        out_specs=t.tok(D),
        compiler_params=_cparams(2),
        name="even_out",
    )(x, o_a, o_r, mod, p["g_post"], p["w_out_even"], p["w_out_even"])


def _softplus(x):
    return jnp.maximum(x, 0.0) + jnp.log1p(jnp.exp(-jnp.abs(x)))


def _odd_body(*refs, g, rt, prompt):
    (x_ref, sh_ref, sc_ref, gt_ref, gpre_ref, gpost_ref, win_ref, cw_ref, cb_ref,
     wa_ref, ba_ref, wx_ref, bx_ref, lam_ref, wout_ref) = refs[:15]
    if prompt:
        o_ref, tail_out_ref, h_out_ref, tail_ref, hcar_ref = refs[15:]
    else:
        prev_ref, h0_ref, o_ref, xb_out_ref, hs_out_ref = refs[15:]
    rows = g * rt
    x = x_ref[...]
    h = _prenorm(x, gpre_ref[...], sh_ref[...], sc_ref[...], g, rt).astype(BF16)
    proj = _dot(h, win_ref[...])
    gate_br = proj[:, :D]
    xb = proj[:, D:]
    row = lax.broadcasted_iota(jnp.int32, (rows, 1), 0)
    cw = cw_ref[...]

    if prompt:
        j = pl.program_id(1)

        @pl.when(j == 0)
        def _():
            tail_ref[...] = jnp.zeros_like(tail_ref)
            hcar_ref[...] = jnp.zeros_like(hcar_ref)

        pos = row
        xext = jnp.concatenate([tail_ref[...], xb], axis=0)
        conv = xb * cw[CONV_W - 1:CONV_W] + cb_ref[...]
        for jj in range(CONV_W - 1):
            d = CONV_W - 1 - jj
            conv = conv + pltpu.roll(xext, d, 0)[SUBLANES:] * cw[jj:jj + 1]
        tail_ref[...] = xb[rows - SUBLANES:]
        tail_out_ref[0] = xb[rows - SUBLANES:]
    else:
        pos = jnp.bitwise_and(row, rt - 1)
        prev = prev_ref[...]
        conv = xb * cw[CONV_W - 1:CONV_W] + cb_ref[...]
        for jj in range(CONV_W - 1):
            d = CONV_W - 1 - jj
            tap = jnp.where(pos >= d, pltpu.roll(xb, d, 0), pltpu.roll(prev, rows - (SUBLANES - d), 0))
            conv = conv + tap * cw[jj:jj + 1]
        xb_out_ref[...] = xb

    convb = conv.astype(BF16)
    ra = jnp.concatenate([_dot(convb[:, n * RG_BW:(n + 1) * RG_BW], wa_ref[n]) for n in range(RG_BLOCKS)], axis=-1)
    rx = jnp.concatenate([_dot(convb[:, n * RG_BW:(n + 1) * RG_BW], wx_ref[n]) for n in range(RG_BLOCKS)], axis=-1)
    r = _sigmoid(ra + ba_ref[...])
    ig = _sigmoid(rx + bx_ref[...])
    log_a = (-RG_C) * r * _softplus(-lam_ref[...])
    a = jnp.exp(log_a)
    mult = jnp.sqrt(1.0 - jnp.exp(2.0 * log_a))
    if prompt:
        mult = jnp.where(jnp.logical_and(j == 0, row == 0), 1.0, mult)
    b = mult * (ig * conv)
    if not prompt:
        b = b + a * h0_ref[...]

    n_grp = rows // SUBLANES
    a = a.reshape(n_grp, SUBLANES, D)
    b = b.reshape(n_grp, SUBLANES, D)
    sub = lax.broadcasted_iota(jnp.int32, (1, SUBLANES, D), 1)
    d = 1
    while d < SUBLANES:
        keep = sub >= d
        a_sh = jnp.where(keep, pltpu.roll(a, d, 1), 1.0)
        b_sh = jnp.where(keep, pltpu.roll(b, d, 1), 0.0)
        b = b + a * b_sh
        a = a * a_sh
        d *= 2

    if prompt:
        carry = hcar_ref[0:1]
        parts = []
        for r in range(n_grp):
            part = b[r] + a[r] * carry
            carry = part[SUBLANES - 1:SUBLANES]
            parts.append(part)
        hs = jnp.concatenate(parts, axis=0)
        hcar_ref[0:1] = carry
        h_out_ref[0] = parts[-1]
    else:
        hs = b.reshape(rows, D)
        hs_out_ref[...] = hs
    y = (hs * _gelu_tanh(gate_br)).astype(BF16)
    out = _dot(y, wout_ref[...])
    o_ref[...] = x + _gated(_rms(out, gpost_ref[...]), gt_ref[...], g, rt)


def _odd(t, prompt, x, mod, mcol, l, e, p, prev=None, h0=None):
    vec = t.sel((None, 1, D), (e, 0, 0))
    rgw = t.sel((None, RG_BLOCKS, RG_BW, RG_BW), (e, 0, 0, 0))
    in_specs = [t.tok(D), t.mod(mcol), t.mod(mcol + 1), t.mod(mcol + 2),
                t.sel((None, None, 1, D), (l, 1, 0, 0)), t.sel((None, None, 1, D), (l, 1, 0, 0)),
                t.sel((None, D, 2 * D), (e, 0, 0)), t.sel((None, CONV_W, D), (e, 0, 0)), vec,
                rgw, vec, rgw, vec, vec, t.sel((None, D, D), (e, 0, 0))]
    args = [x, mod, mod, mod, p["g_pre"], p["g_post"], p["w_in_odd"], p["conv_w"], p["conv_b"],
            p["w_rg_a"], p["b_rg_a"], p["w_rg_x"], p["b_rg_x"], p["rg_lambda"], p["w_out_odd"]]
    if prompt:
        last = pl.BlockSpec((1, SUBLANES, D), lambda i, j: (i, 0, 0))
        out_shape = [jax.ShapeDtypeStruct((t.n_tok, D), F32),
                     jax.ShapeDtypeStruct((t.G, SUBLANES, D), F32),
                     jax.ShapeDtypeStruct((t.G, SUBLANES, D), F32)]
        out_specs = [t.tok(D), last, last]
        scratch = [pltpu.VMEM((SUBLANES, D), F32), pltpu.VMEM((SUBLANES, D), F32)]
    else:
        in_specs += [t.tok(D), t.tok(D)]
        args += [prev, h0]
        out_shape = [jax.ShapeDtypeStruct((t.n_tok, D), F32)] * 3
        out_specs = [t.tok(D)] * 3
        scratch = []
    return pl.pallas_call(
        functools.partial(_odd_body, g=t.g, rt=t.rt, prompt=prompt),
        out_shape=out_shape, grid=t.grid, in_specs=in_specs, out_specs=out_specs,
        scratch_shapes=scratch, compiler_params=_cparams(2),
        name="odd_prompt" if prompt else "odd_sample",
    )(*args)


def _rope_tables(pos):
    posf = pos.astype(F32)[:, None]
    inv_a = jnp.power(jnp.float32(ROPE_THETA), -jnp.arange(0, ROT_DIM, 2, dtype=F32) / ROT_DIM)
    ang = posf * inv_a[None, :]
    cos, sin = jnp.cos(ang), jnp.sin(ang)
    n = pos.shape[0]
    half = ROT_DIM // 2
    rest = A_DH - ROT_DIM
    one, zero = jnp.ones((n, rest), F32), jnp.zeros((n, rest), F32)
    zh = jnp.zeros((n, half), F32)
    cos_h = jnp.concatenate([cos, cos, one], axis=1)
    s1_h = jnp.concatenate([-sin, zh, zero], axis=1)
    s2_h = jnp.concatenate([zh, sin, zero], axis=1)
    both = lambda a: jnp.concatenate([a, a], axis=1)
    wi_scale = jnp.full((n, A_DH), IDX_HEADS ** -0.5, F32)
    z64 = jnp.zeros((n, A_DH), F32)
    rope_a = jnp.stack([both(cos_h), both(s1_h), both(s2_h),
                        jnp.concatenate([cos_h, wi_scale], axis=1),
                        jnp.concatenate([s1_h, z64], axis=1),
                        jnp.concatenate([s2_h, z64], axis=1)])
    inv_r = jnp.power(jnp.float32(RET_THETA), -jnp.linspace(0.0, 1.0, RET_DH // 2, dtype=F32))
    ang_r = posf * inv_r[None, :]
    cr, sr = jnp.cos(ang_r), jnp.sin(ang_r)
    rope_r = jnp.stack([jnp.concatenate([cr, cr], axis=1), jnp.concatenate([-sr, sr], axis=1)])
    return rope_a, rope_r


def _pack_w_in_even(w):
    o = np.cumsum((0, 512, 128, 128, 256, IDX_HEADS, IDX_DIM, 512, 512, 512, 512))
    pad = jnp.zeros(w.shape[:2] + (LANES - IDX_DIM - IDX_HEADS,), w.dtype)
    return jnp.concatenate([w[..., o[0]:o[4]], w[..., o[5]:o[6]], w[..., o[4]:o[5]], pad, w[..., o[6]:o[10]]],
                           axis=-1).astype(BF16)


def kernel(x_prompt, x_sample, cache_k, cache_v, cache_kidx, state_ret, state_conv, state_rglru, page_table,
           c_prompt, c_sample, w_ada, b_ada, g_pre, g_post, w_ffn_gate, w_ffn_up, w_ffn_down,
           w_in_even, w_out_even, w_in_odd, w_out_odd, conv_w, conv_b, w_rg_a, b_rg_a, w_rg_x, b_rg_x, rg_lambda):
    B, L, _ = x_prompt.shape
    Bs, T, _ = x_sample.shape
    n_pages = page_table.shape[1]
    past = n_pages * cache_k.shape[2]
    n_phys = cache_k.shape[1]
    assert cache_k.shape[2] == PAGE and T == SUBLANES

    tp = _Tiling(B, L, 1, 512)
    ts = _Tiling(Bs, T, 64, T)
    tp_odd = _Tiling(B, L, 1, 256)
    ts_odd = _Tiling(Bs, T, 32, T)

    vec3 = lambda a: a.reshape(a.shape[0], 1, a.shape[1])
    p = {
        "g_pre": g_pre.reshape(DEPTH, 3, 1, D), "g_post": g_post.reshape(DEPTH, 3, 1, D),
        "wg": w_ffn_gate.astype(BF16), "wu": w_ffn_up.astype(BF16), "wd": w_ffn_down.astype(BF16),
        "w_in_even": _pack_w_in_even(w_in_even), "w_out_even": w_out_even.astype(BF16),
        "w_in_odd": w_in_odd.astype(BF16), "w_out_odd": w_out_odd.astype(BF16),
        "conv_w": conv_w, "conv_b": vec3(conv_b),
        "w_rg_a": w_rg_a.astype(BF16), "b_rg_a": vec3(b_rg_a),
        "w_rg_x": w_rg_x.astype(BF16), "b_rg_x": vec3(b_rg_x), "rg_lambda": vec3(rg_lambda),
    }
    k_t = jnp.transpose(cache_k, (0, 1, 3, 4, 2)).reshape(cache_k.shape[0], n_phys, LANES, PAGE)
    v_t = jnp.transpose(cache_v, (0, 1, 3, 4, 2)).reshape(cache_v.shape[0], n_phys, LANES, PAGE)
    kidx_t = jnp.transpose(cache_kidx, (0, 1, 3, 2))

    mod_all = _ada(jnp.concatenate([c_prompt, c_sample], axis=0), w_ada, b_ada)
    mod_p = mod_all[:, :B].reshape(DEPTH, B, 1, N_MOD * D)
    mod_s = mod_all[:, B:].reshape(DEPTH, Bs, 1, N_MOD * D)

    rope_a_p, rope_r_p = _rope_tables(jnp.arange(L, dtype=jnp.int32))
    rope_a_s, rope_r_s = _rope_tables(jnp.tile(past + jnp.arange(T, dtype=jnp.int32), ts.g))
    rope_map_p = lambda i, j: (0, j, 0)
    rope_map_s = lambda i, j: (0, 0, 0)

    xp = x_prompt.reshape(B * L, D)
    xs = x_sample.reshape(Bs * T, D)
    ks, vs, kis, rets, convs, hs = ([[], []] for _ in range(6))

    for l in range(DEPTH):
        e = l // 2
        groups = ((0, tp, xp, mod_p[l]), (1, ts, xs, mod_s[l]))
        new_x = []
        for gi, t, x, mod in groups:
            x = _ffn(t, x, mod, 0, l, 0, 0, p)
            if l % 2 == 0:
                if gi == 0:
                    q, kt, v, qi, kw, kit, rq, rk, rv, rg = _even_in(
                        t, True, x, mod, 3, l, e, p, rope_a_p, rope_r_p, rope_map_p)
                    o_a = _dsa_prompt(B, L, q, qi, kw, kt, v)
                    o_r, s_new = _ret_prompt(B, L, rq, rk, rv, rg)
                    ks[0].append(jnp.transpose(kt.reshape(B, A_KV, A_DH, L), (0, 3, 1, 2)))
                    vs[0].append(v.reshape(B, L, A_KV, A_DH))
                    kis[0].append(jnp.transpose(kit, (0, 2, 1)))
                else:
                    q, k, v, qi, kw, rq, rk, rv, rg = _even_in(
                        t, False, x, mod, 3, l, e, p, rope_a_s, rope_r_s, rope_map_s)
                    scores = _dsa_s_scores(Bs, T, e, page_table, qi, kw, kidx_t)
                    bias = _dsa_s_select(Bs * T, T, past, scores)
                    o_a = _dsa_s_attn(Bs, T, e, page_table, q, k, v, bias, k_t, v_t)
                    o_r, s_new = _ret_sample(Bs, T, e, rq, rk, rv, rg, state_ret)
                    ks[1].append(k.reshape(Bs, T, A_KV, A_DH))
                    vs[1].append(v.reshape(Bs, T, A_KV, A_DH))
                    kis[1].append(kw[:, :IDX_DIM].reshape(Bs, T, IDX_DIM))
                rets[gi].append(s_new)
                x = _even_out(t, x, o_a, o_r, mod, 5, l, e, p)
            else:
                if gi == 0:
                    x, tail, hlast = _odd(tp_odd, True, x, mod, 3, l, e, p)
                    convs[0].append(tail[:, SUBLANES - (CONV_W - 1):])
                    hs[0].append(hlast[:, SUBLANES - 1])
                else:
                    prev = jnp.pad(state_conv[e], ((0, 0), (SUBLANES - (CONV_W - 1), 0), (0, 0))).reshape(Bs * T, D)
                    h0 = jnp.pad(state_rglru[e][:, None, :], ((0, 0), (0, T - 1), (0, 0))).reshape(Bs * T, D)
                    x, xb, hseq = _odd(ts_odd, False, x, mod, 3, l, e, p, prev=prev, h0=h0)
                    convs[1].append(xb.reshape(Bs, T, D)[:, T - (CONV_W - 1):])
                    hs[1].append(hseq.reshape(Bs, T, D)[:, T - 1])
            x = _ffn(t, x, mod, 6, l, 1, 2, p)
            new_x.append(x)
        xp, xs = new_x

    st = lambda lists, gi: jnp.stack(lists[gi])
    return (xp.reshape(B, L, D), xs.reshape(Bs, T, D),
            st(ks, 0), st(vs, 0), st(kis, 0), st(rets, 0), st(convs, 0), st(hs, 0),
            st(ks, 1), st(vs, 1), st(kis, 1), st(rets, 1), st(convs, 1), st(hs, 1))
```

```python
import functools
import math

import jax
import jax.numpy as jnp
import numpy as np
from jax import lax
from jax.experimental import pallas as pl
from jax.experimental.pallas import tpu as pltpu

F32 = jnp.float32
BF16 = jnp.bfloat16

D = 1024
DEPTH = 4
N_MOD = 9
A_HEADS = 8
A_KV = 2
A_DH = 64
ROT_DIM = 16
ROPE_THETA = 500000.0
IDX_HEADS = 4
IDX_DIM = 64
TOPK_MAX = 256
RET_HEADS = 4
RET_DH = 128
RET_THETA = 10000.0
RG_BLOCKS = 8
RG_BW = 128
CONV_W = 4
RG_C = 8.0
D_FF = 2816
EPS = 1e-6
P_IN = 3200
PAGE = 128

LANES = 128
SUBLANES = 8
VMEM_LIMIT = 56 * 1024 * 1024

NEG_INF = float("-inf")


def _cparams(n_axes, vmem=VMEM_LIMIT):
    return pltpu.CompilerParams(dimension_semantics=("arbitrary",) * n_axes, vmem_limit_bytes=vmem)


def _dot(a, b):
    return jnp.dot(a, b, preferred_element_type=F32)


def _dot_nt(a, b):
    return lax.dot_general(a, b, (((1,), (1,)), ((), ())), preferred_element_type=F32)


def _dot_tn(a, b):
    return lax.dot_general(a, b, (((0,), (0,)), ((), ())), preferred_element_type=F32)


def _sigmoid(x):
    return 1.0 / (1.0 + jnp.exp(-x))


def _silu(x):
    return x * _sigmoid(x)


def _gelu_tanh(x):
    return 0.5 * x * (1.0 + jnp.tanh(math.sqrt(2.0 / math.pi) * (x + 0.044715 * (x * x * x))))


def _rms(x, g=None):
    y = x * lax.rsqrt(jnp.mean(x * x, axis=-1, keepdims=True) + EPS)
    return y if g is None else y * g


def _prenorm(x, g_pre, shift, scale, g, rt):
    h = _rms(x, g_pre)
    if g == 1:
        return h * (1.0 + scale[0]) + shift[0]
    h3 = h.reshape(g, rt, D)
    return (h3 * (1.0 + scale) + shift).reshape(g * rt, D)


def _gated(y, gate, g, rt):
    if g == 1:
        return y * gate[0]
    return (y.reshape(g, rt, D) * gate).reshape(g * rt, D)


class _Tiling:
    def __init__(self, G, R, g, rt):
        assert G % g == 0 and R % rt == 0 and (g == 1 or rt == R)
        self.G, self.R, self.g, self.rt = G, R, g, rt
        self.nr = R // rt
        self.rows = g * rt
        self.grid = (G // g, self.nr)
        self.n_tok = G * R

    def tok(self, width):
        nr = self.nr
        return pl.BlockSpec((self.rows, width), lambda i, j: (i * nr + j, 0))

    def mod(self, c):
        return pl.BlockSpec((self.g, 1, D), lambda i, j: (i, 0, c))

    def sel(self, block, idx):
        return pl.BlockSpec(block, lambda i, j: idx, pipeline_mode=pl.Buffered(1))


def _ada_body(c_ref, w_ref, b_ref, o_ref):
    c = _silu(c_ref[...]).astype(BF16)
    o_ref[0] = _dot(c, w_ref[0].astype(BF16)) + b_ref[0]


def _ada(c_all, w_ada, b_ada):
    n = c_all.shape[0]
    return pl.pallas_call(
        _ada_body,
        out_shape=jax.ShapeDtypeStruct((DEPTH, n, N_MOD * D), F32),
        grid=(DEPTH, N_MOD),
        in_specs=[pl.BlockSpec((n, D), lambda l, j: (0, 0)),
                  pl.BlockSpec((1, D, D), lambda l, j: (l, 0, j)),
                  pl.BlockSpec((1, 1, D), lambda l, j: (l, 0, j))],
        out_specs=pl.BlockSpec((1, n, D), lambda l, j: (l, 0, j)),
        compiler_params=_cparams(2),
        name="ada",
    )(c_all, w_ada, b_ada.reshape(DEPTH, 1, N_MOD * D))


def _ffn_body(x_ref, sh_ref, sc_ref, gt_ref, gpre_ref, gpost_ref, wg_ref, wu_ref, wd_ref, o_ref, *, g, rt):
    x = x_ref[...]
    h = _prenorm(x, gpre_ref[...], sh_ref[...], sc_ref[...], g, rt).astype(BF16)
    a = _dot(h, wg_ref[...])
    u = _dot(h, wu_ref[...])
    act = (_silu(a) * u).astype(BF16)
    f = _dot(act, wd_ref[...])
    y = _rms(f, gpost_ref[...])
    o_ref[...] = x + 0.5 * _gated(y, gt_ref[...], g, rt)


def _ffn(t, x, mod, mcol, l, s, n, p):
    return pl.pallas_call(
        functools.partial(_ffn_body, g=t.g, rt=t.rt),
        out_shape=jax.ShapeDtypeStruct((t.n_tok, D), F32),
        grid=t.grid,
        in_specs=[t.tok(D), t.mod(mcol), t.mod(mcol + 1), t.mod(mcol + 2),
                  t.sel((None, None, 1, D), (l, n, 0, 0)), t.sel((None, None, 1, D), (l, n, 0, 0)),
                  t.sel((None, None, D, D_FF), (l, s, 0, 0)), t.sel((None, None, D, D_FF), (l, s, 0, 0)),
                  t.sel((None, None, D_FF, D), (l, s, 0, 0))],
        out_specs=t.tok(D),
        compiler_params=_cparams(2),
        name="ffn",
    )(x, mod, mod, mod, p["g_pre"], p["g_post"], p["wg"], p["wu"], p["wd"])


def _even_in_body(*refs, g, rt, prompt):
    x_ref, sh_ref, sc_ref, gpre_ref, w_ref, ra_ref, rr_ref = refs[:7]
    q_ref, k_ref, v_ref, qi_ref, kw_ref = refs[7:12]
    if prompt:
        kit_ref, rq_ref, rk_ref, rv_ref, rg_ref = refs[12:]
    else:
        rq_ref, rk_ref, rv_ref, rg_ref = refs[12:]
    h = _prenorm(x_ref[...], gpre_ref[...], sh_ref[...], sc_ref[...], g, rt).astype(BF16)
    proj = _dot(h, w_ref[...])

    def sec(c):
        return proj[:, c * LANES:(c + 1) * LANES]

    def rope_a(t, v):
        return (t * ra_ref[3 * v] + pltpu.roll(t, LANES - ROT_DIM // 2, 1) * ra_ref[3 * v + 1]
                + pltpu.roll(t, ROT_DIM // 2, 1) * ra_ref[3 * v + 2])

    def rope_r(t):
        return t * rr_ref[0] + pltpu.roll(t, RET_DH // 2, 1) * rr_ref[1]

    for c in range(4):
        q_ref[:, c * LANES:(c + 1) * LANES] = rope_a(sec(c), 0) * (A_DH ** -0.5)
    k = rope_a(sec(4), 0)
    v_ref[...] = sec(5)
    for c in range(2):
        qi_ref[:, c * LANES:(c + 1) * LANES] = rope_a(sec(6 + c), 0) * (IDX_DIM ** -0.5)
    kw = rope_a(sec(8), 1)
    kw_ref[...] = kw
    if prompt:
        k_ref[...] = k.T
        kit_ref[...] = kw.T[0:IDX_DIM]
    else:
        k_ref[...] = k
    for c in range(4):
        rq_ref[:, c * LANES:(c + 1) * LANES] = rope_r(sec(9 + c))
        rk_ref[:, c * LANES:(c + 1) * LANES] = rope_r(sec(13 + c) * (RET_DH ** -0.5))
        rv_ref[:, c * LANES:(c + 1) * LANES] = sec(17 + c)
        rg_ref[:, c * LANES:(c + 1) * LANES] = sec(21 + c)


def _even_in(t, prompt, x, mod, mcol, l, e, p, rope_a, rope_r, rope_map):
    tokw = lambda w: (jax.ShapeDtypeStruct((t.n_tok, w), F32), t.tok(w))
    seq_t = lambda w: (jax.ShapeDtypeStruct((t.G, w, t.R), F32),
                       pl.BlockSpec((None, w, t.rt), lambda i, j: (i, 0, j)))
    outs = [tokw(512), seq_t(LANES) if prompt else tokw(LANES), tokw(LANES), tokw(256), tokw(LANES)]
    if prompt:
        outs.append(seq_t(IDX_DIM))
    outs += [tokw(512)] * 4
    return pl.pallas_call(
        functools.partial(_even_in_body, g=t.g, rt=t.rt, prompt=prompt),
        out_shape=[o[0] for o in outs],
        grid=t.grid,
        in_specs=[t.tok(D), t.mod(mcol), t.mod(mcol + 1), t.sel((None, None, 1, D), (l, 1, 0, 0)),
                  t.sel((None, D, P_IN), (e, 0, 0)),
                  pl.BlockSpec((6, t.rows, LANES), rope_map),
                  pl.BlockSpec((2, t.rows, LANES), rope_map)],
        out_specs=[o[1] for o in outs],
        compiler_params=_cparams(2),
        name="even_in_p" if prompt else "even_in_s",
    )(x, mod, mod, p["g_pre"], p["w_in_even"], rope_a, rope_r)


def _sortable(s):
    s = jnp.where(s == 0.0, 0.0, s)
    b = lax.bitcast_convert_type(s, jnp.int32)
    return b ^ ((b >> 31) & 0x7FFFFFFF)


def _topk_bias_t(scores_t, visible_t, k):
    s_len, n_q = scores_t.shape
    key = _sortable(scores_t)
    fold = 8 * SUBLANES
    assert s_len % fold == 0 and s_len % LANES == 0
    i16 = jnp.int16

    def count(mask):
        part = jnp.sum(jnp.where(mask, 1.0, 0.0).reshape(s_len // fold, fold, n_q), axis=0)
        return jnp.sum(part, axis=0, keepdims=True)

    def count16(mask):
        ind = jnp.where(mask, i16(1), i16(0))
        part = ind[0:LANES]
        for r in range(1, s_len // LANES):
            part = part + ind[r * LANES:(r + 1) * LANES]
        return jnp.sum(part.astype(jnp.int32).astype(F32), axis=0, keepdims=True)

    def kth16(vals, kk):
        def step(it, t):
            cand = t + jnp.left_shift(jnp.int32(1), 15 - it)
            return jnp.where(count16(vals >= cand.astype(i16)) >= kk, cand, t)
        return lax.fori_loop(0, 16, step, jnp.full((1, n_q), -(2 ** 15), jnp.int32))

    hi = (key >> 16).astype(i16)
    t_hi = kth16(hi, k)
    t_hi16 = t_hi.astype(i16)
    lo = jnp.where(hi == t_hi16, ((key & 0xFFFF) - 2 ** 15).astype(i16), i16(-(2 ** 15)))
    t_lo = kth16(lo, k - count16(hi > t_hi16))
    thr = jnp.left_shift(t_hi, 16) + (t_lo + 2 ** 15)
    ge = key >= thr
    cnt_ge = count(ge)

    def no_ties():
        return jnp.where(visible_t, jnp.where(ge, 0.0, NEG_INF), NEG_INF)

    def ties():
        need = k - count(key > thr)
        tri = jnp.where(lax.broadcasted_iota(jnp.int32, (LANES, LANES), 0)
                        >= lax.broadcasted_iota(jnp.int32, (LANES, LANES), 1), 1.0, 0.0).astype(BF16)
        off = jnp.zeros((1, n_q), F32)
        pieces = []
        for c in range(s_len // LANES):
            sl = slice(c * LANES, (c + 1) * LANES)
            kc = key[sl, :]
            eq = kc == thr
            pc = _dot(tri, jnp.where(eq, 1.0, 0.0).astype(BF16)) + off
            off = pc[LANES - 1:LANES, :]
            tie_ok = jnp.where(eq, jnp.where(pc <= need, 0.0, NEG_INF), NEG_INF)
            b = jnp.where(kc > thr, 0.0, tie_ok)
            pieces.append(jnp.where(visible_t[sl, :], b, NEG_INF))
        return jnp.concatenate(pieces, axis=0)

    return lax.cond(jnp.max(cnt_ge) > k, ties, no_ties)


def _softmax(logits):
    m = jnp.max(logits, axis=-1, keepdims=True)
    p = jnp.exp(logits - m)
    return p, jnp.sum(p, axis=-1, keepdims=True)


def _dsa_p_body(q_ref, qi_ref, kw_ref, kt_ref, v_ref, o_ref, *, tq, topk):
    s_len = q_ref.shape[0]
    rep = A_HEADS // A_KV
    kt = kt_ref[...].astype(BF16)
    ki = kw_ref[:, 0:IDX_DIM].astype(BF16)
    v = v_ref[...].astype(BF16)
    for i in range(s_len // tq):
        rows = slice(i * tq, (i + 1) * tq)
        s_vis = (i + 1) * tq
        qit = qi_ref[rows, :].T.astype(BF16)
        wit = kw_ref[rows, :].T
        sc = jnp.zeros((s_vis, tq), F32)
        for h in range(IDX_HEADS):
            raw = _dot(ki[:s_vis], qit[h * IDX_DIM:(h + 1) * IDX_DIM, :])
            sc = sc + jnp.maximum(raw, 0.0) * wit[IDX_DIM + h:IDX_DIM + h + 1, :]
        kpos = lax.broadcasted_iota(jnp.int32, (s_vis, 1), 0)
        qpos = i * tq + lax.broadcasted_iota(jnp.int32, (1, tq), 1)
        visible = kpos <= qpos
        bias = _topk_bias_t(jnp.where(visible, sc, NEG_INF), visible, topk).T
        q = q_ref[rows, :].astype(BF16)
        for hp in range(A_HEADS // 2):
            outs = []
            for h in (2 * hp, 2 * hp + 1):
                gsl = slice((h // rep) * A_DH, (h // rep + 1) * A_DH)
                p, l = _softmax(_dot(q[:, h * A_DH:(h + 1) * A_DH], kt[gsl, :s_vis]) + bias)
                outs.append(_dot(p.astype(BF16), v[:s_vis, gsl]) / l)
            o_ref[rows, hp * LANES:(hp + 1) * LANES] = jnp.concatenate(outs, axis=-1)


def _dsa_prompt(B, L, q, qi, kw, kt, v, tq=256):
    tok = lambda w: pl.BlockSpec((L, w), lambda b: (b, 0))
    seq_t = lambda w: pl.BlockSpec((None, w, L), lambda b: (b, 0, 0))
    return pl.pallas_call(
        functools.partial(_dsa_p_body, tq=tq, topk=min(TOPK_MAX, L // 4)),
        out_shape=jax.ShapeDtypeStruct((B * L, A_HEADS * A_DH), F32),
        grid=(B,),
        in_specs=[tok(512), tok(256), tok(LANES), seq_t(LANES), tok(LANES)],
        out_specs=tok(512),
        compiler_params=_cparams(1),
        name="dsa_prompt",
    )(q, qi, kw, kt, v)


SEQ_PER_STEP = 4


def _page_gather(pt_ref, pool_ref, buf_ref, sem_ref, e, n_pages):
    b = pl.program_id(0)

    def copies(step, slot):
        return [pltpu.make_async_copy(pool_ref.at[e, pt_ref[step * SEQ_PER_STEP + s, p]],
                                      buf_ref.at[slot, s, :, pl.ds(p * PAGE, PAGE)],
                                      sem_ref.at[slot])
                for s in range(SEQ_PER_STEP) for p in range(n_pages)]

    @pl.when(b == 0)
    def _():
        for c in copies(0, 0):
            c.start()

    @pl.when(b + 1 < pl.num_programs(0))
    def _():
        for c in copies(b + 1, (b + 1) % 2):
            c.start()

    slot = b % 2
    for c in copies(b, slot):
        c.wait()
    return slot


def _dsa_s_scores_body(pt_ref, qi_ref, kw_ref, pool_ref, o_ref, buf_ref, sem_ref, *, e, n_pages, t):
    slot = _page_gather(pt_ref, pool_ref, buf_ref, sem_ref, e, n_pages)
    past = n_pages * PAGE
    lane = lax.broadcasted_iota(jnp.int32, (t, PAGE), 1)
    row = lax.broadcasted_iota(jnp.int32, (t, PAGE), 0)
    for s in range(SEQ_PER_STEP):
        rows = slice(s * t, (s + 1) * t)
        qi = qi_ref[rows, :]
        qs = jnp.concatenate([qi[:, h * IDX_DIM:(h + 1) * IDX_DIM] for h in range(IDX_HEADS)], axis=0).astype(BF16)
        kw = kw_ref[rows, :]
        wi = kw[:, IDX_DIM:LANES]
        ki_new = jnp.concatenate([kw[:, 0:IDX_DIM], jnp.zeros((PAGE - t, IDX_DIM), F32)], axis=0).astype(BF16)
        kit = buf_ref[slot, s].astype(BF16)

        def weigh(raw):
            raw = jnp.maximum(raw, 0.0)
            sc = raw[0:t] * wi[:, 0:1]
            for h in range(1, IDX_HEADS):
                sc = sc + raw[h * t:(h + 1) * t] * wi[:, h:h + 1]
            return sc

        o_ref[rows, :past] = weigh(_dot(qs, kit))
        o_ref[rows, past:] = jnp.where(lane <= row, weigh(_dot_nt(qs, ki_new)), NEG_INF)


def _dsa_s_scores(Bs, T, e, page_table, qi, kw, kidx_t):
    n_pages = page_table.shape[1]
    tok = lambda w: pl.BlockSpec((SEQ_PER_STEP * T, w), lambda b, pt: (b, 0))
    s_pad = (n_pages + 1) * PAGE
    return pl.pallas_call(
        functools.partial(_dsa_s_scores_body, e=e, n_pages=n_pages, t=T),
        out_shape=jax.ShapeDtypeStruct((Bs * T, s_pad), F32),
        grid_spec=pltpu.PrefetchScalarGridSpec(
            num_scalar_prefetch=1, grid=(Bs // SEQ_PER_STEP,),
            in_specs=[tok(256), tok(LANES), pl.BlockSpec(memory_space=pl.ANY)],
            out_specs=tok(s_pad),
            scratch_shapes=[pltpu.VMEM((2, SEQ_PER_STEP, IDX_DIM, n_pages * PAGE), F32),
                            pltpu.SemaphoreType.DMA((2,))]),
        compiler_params=_cparams(1),
        name="dsa_s_scores",
    )(page_table, qi, kw, kidx_t)


def _dsa_s_select_body(sc_ref, o_ref, *, t, past, topk):
    rows, s_pad = sc_ref.shape
    tpos = jnp.bitwise_and(lax.broadcasted_iota(jnp.int32, (1, rows), 1), t - 1)
    kpos = lax.broadcasted_iota(jnp.int32, (s_pad, 1), 0)
    visible = kpos <= past + tpos
    o_ref[...] = _topk_bias_t(sc_ref[...].T, visible, topk).T


def _dsa_s_select(n_rows, T, past, scores, rows=256):
    s_pad = scores.shape[1]
    spec = pl.BlockSpec((rows, s_pad), lambda i: (i, 0))
    return pl.pallas_call(
        functools.partial(_dsa_s_select_body, t=T, past=past, topk=min(TOPK_MAX, (past + T) // 4)),
        out_shape=jax.ShapeDtypeStruct((n_rows, s_pad), F32),
        grid=(n_rows // rows,),
        in_specs=[spec], out_specs=spec,
        compiler_params=_cparams(1),
        name="dsa_s_select",
    )(scores)


def _dsa_s_attn_body(pt_ref, q_ref, kn_ref, vn_ref, b_ref, kpool_ref, vpool_ref, o_ref,
                     kbuf_ref, vbuf_ref, ksem_ref, vsem_ref, *, e, n_pages, t):
    slot = _page_gather(pt_ref, kpool_ref, kbuf_ref, ksem_ref, e, n_pages)
    _page_gather(pt_ref, vpool_ref, vbuf_ref, vsem_ref, e, n_pages)
    past = n_pages * PAGE
    rep = A_HEADS // A_KV
    pad = jnp.zeros((PAGE - t, LANES), F32)
    for s in range(SEQ_PER_STEP):
        rows = slice(s * t, (s + 1) * t)
        kt = kbuf_ref[slot, s].astype(BF16)
        vt = vbuf_ref[slot, s].astype(BF16)
        k_new = jnp.concatenate([kn_ref[rows, :], pad], axis=0).astype(BF16)
        v_new = jnp.concatenate([vn_ref[rows, :], pad], axis=0).astype(BF16)
        q = q_ref[rows, :].astype(BF16)
        bias = jnp.concatenate([b_ref[rows, :]] * rep, axis=0)
        outs = []
        for gi in range(A_KV):
            gsl = slice(gi * A_DH, (gi + 1) * A_DH)
            qg = jnp.concatenate([q[:, (gi * rep + r) * A_DH:(gi * rep + r + 1) * A_DH] for r in range(rep)], axis=0)
            logits = jnp.concatenate([_dot(qg, kt[gsl, :]), _dot_nt(qg, k_new[:, gsl])], axis=-1) + bias
            p, l = _softmax(logits)
            p = p.astype(BF16)
            og = (_dot_nt(p[:, :past], vt[gsl, :]) + _dot(p[:, past:], v_new[:, gsl])) / l
            outs += [og[r * t:(r + 1) * t] for r in range(rep)]
        for hp in range(A_HEADS // 2):
            o_ref[rows, hp * LANES:(hp + 1) * LANES] = jnp.concatenate(outs[2 * hp:2 * hp + 2], axis=-1)


def _dsa_s_attn(Bs, T, e, page_table, q, k_new, v_new, bias, k_t, v_t):
    n_pages = page_table.shape[1]
    tok = lambda w: pl.BlockSpec((SEQ_PER_STEP * T, w), lambda b, pt: (b, 0))
    pool = pl.BlockSpec(memory_space=pl.ANY)
    buf = pltpu.VMEM((2, SEQ_PER_STEP, LANES, n_pages * PAGE), F32)
    return pl.pallas_call(
        functools.partial(_dsa_s_attn_body, e=e, n_pages=n_pages, t=T),
        out_shape=jax.ShapeDtypeStruct((Bs * T, A_HEADS * A_DH), F32),
        grid_spec=pltpu.PrefetchScalarGridSpec(
            num_scalar_prefetch=1, grid=(Bs // SEQ_PER_STEP,),
            in_specs=[tok(512), tok(LANES), tok(LANES), tok(bias.shape[1]), pool, pool],
            out_specs=tok(512),
            scratch_shapes=[buf, buf, pltpu.SemaphoreType.DMA((2,)), pltpu.SemaphoreType.DMA((2,))]),
        compiler_params=_cparams(1),
        name="dsa_s_attn",
    )(page_table, q, k_new, v_new, bias, k_t, v_t)


def _ret_log_gamma(h):
    return math.log1p(-(2.0 ** (-5.0 - h)))


def _ret_chunk(q, k, v, rg, s, h, c):
    lg = _ret_log_gamma(h)
    ri = lax.broadcasted_iota(jnp.int32, (c, c), 0)
    ci = lax.broadcasted_iota(jnp.int32, (c, c), 1)
    diff = (ri - ci).astype(F32)
    decay = jnp.where(diff >= 0, jnp.exp(jnp.maximum(diff, 0.0) * lg), 0.0)
    idx = lax.broadcasted_iota(jnp.int32, (c, 1), 0).astype(F32)
    q_dec = jnp.exp((idx + 1.0) * lg)
    k_dec = jnp.exp((c - 1.0 - idx) * lg)
    c_dec = math.exp(c * lg)
    qb = q.astype(BF16)
    att = _dot_nt(qb, k.astype(BF16)) * decay
    inner = _dot(att.astype(BF16), v.astype(BF16))
    cross = _dot(qb, s.astype(BF16)) * q_dec
    s_new = s * c_dec + _dot_tn((k * k_dec).astype(BF16), v.astype(BF16))
    return _rms(inner + cross) * _silu(rg), s_new


def _ret_p_body(rq_ref, rk_ref, rv_ref, rg_ref, o_ref, so_ref, s_ref, *, c):
    j = pl.program_id(1)

    @pl.when(j == 0)
    def _():
        s_ref[...] = jnp.zeros_like(s_ref)

    for h in range(RET_HEADS):
        sl = slice(h * RET_DH, (h + 1) * RET_DH)
        o, s_new = _ret_chunk(rq_ref[:, sl], rk_ref[:, sl], rv_ref[:, sl], rg_ref[:, sl], s_ref[h], h, c)
        o_ref[:, sl] = o
        s_ref[h] = s_new
    so_ref[0] = s_ref[...]


def _ret_prompt(B, L, rq, rk, rv, rg, c=256):
    nc = L // c
    tok = pl.BlockSpec((c, 512), lambda b, j: (b * nc + j, 0))
    return pl.pallas_call(
        functools.partial(_ret_p_body, c=c),
        out_shape=[jax.ShapeDtypeStruct((B * L, 512), F32),
                   jax.ShapeDtypeStruct((B, RET_HEADS, RET_DH, RET_DH), F32)],
        grid=(B, nc),
        in_specs=[tok, tok, tok, tok],
        out_specs=[tok, pl.BlockSpec((1, RET_HEADS, RET_DH, RET_DH), lambda b, j: (b, 0, 0, 0))],
        scratch_shapes=[pltpu.VMEM((RET_HEADS, RET_DH, RET_DH), F32)],
        compiler_params=_cparams(2),
        name="ret_prompt",
    )(rq, rk, rv, rg)


def _ret_s_body(rq_ref, rk_ref, rv_ref, rg_ref, s0_ref, o_ref, so_ref, *, t):
    for s in range(SEQ_PER_STEP):
        rows = slice(s * t, (s + 1) * t)
        for h in range(RET_HEADS):
            sl = slice(h * RET_DH, (h + 1) * RET_DH)
            o, s_new = _ret_chunk(rq_ref[rows, sl], rk_ref[rows, sl], rv_ref[rows, sl], rg_ref[rows, sl],
                                  s0_ref[s, h], h, t)
            o_ref[rows, sl] = o
            so_ref[s, h] = s_new


def _ret_sample(Bs, T, e, rq, rk, rv, rg, state_ret):
    tok = pl.BlockSpec((SEQ_PER_STEP * T, 512), lambda b: (b, 0))
    return pl.pallas_call(
        functools.partial(_ret_s_body, t=T),
        out_shape=[jax.ShapeDtypeStruct((Bs * T, 512), F32),
                   jax.ShapeDtypeStruct((Bs, RET_HEADS, RET_DH, RET_DH), F32)],
        grid=(Bs // SEQ_PER_STEP,),
        in_specs=[tok, tok, tok, tok,
                  pl.BlockSpec((None, SEQ_PER_STEP, RET_HEADS, RET_DH, RET_DH), lambda b: (e, b, 0, 0, 0))],
        out_specs=[tok, pl.BlockSpec((SEQ_PER_STEP, RET_HEADS, RET_DH, RET_DH), lambda b: (b, 0, 0, 0))],
        compiler_params=_cparams(1),
        name="ret_sample",
    )(rq, rk, rv, rg, state_ret)


def _even_out_body(x_ref, oa_ref, or_ref, gt_ref, gpost_ref, wa_ref, wr_ref, o_ref, *, g, rt):
    y = _dot(oa_ref[...].astype(BF16), wa_ref[...]) + _dot(or_ref[...].astype(BF16), wr_ref[...])
    o_ref[...] = x_ref[...] + _gated(_rms(y, gpost_ref[...]), gt_ref[...], g, rt)


def _even_out(t, x, o_a, o_r, mod, mcol, l, e, p):
    half = A_HEADS * A_DH
    return pl.pallas_call(
        functools.partial(_even_out_body, g=t.g, rt=t.rt),
        out_shape=jax.ShapeDtypeStruct((t.n_tok, D), F32),
        grid=t.grid,
        in_specs=[t.tok(D), t.tok(half), t.tok(half), t.mod(mcol), t.sel((None, None, 1, D), (l, 1, 0, 0)),
                  t.sel((None, half, D), (e, 0, 0)), t.sel((None, half, D), (e, 1, 0))],
        out_specs=t.tok(D),
        compiler_params=_cparams(2),
        name="even_out",
    )(x, o_a, o_r, mod, p["g_post"], p["w_out_even"], p["w_out_even"])


def _softplus(x):
    return jnp.maximum(x, 0.0) + jnp.log1p(jnp.exp(-jnp.abs(x)))


def _scan_sublanes(a, b):
    sub = lax.broadcasted_iota(jnp.int32, (1, SUBLANES, a.shape[-1]), 1)
    d = 1
    while d < SUBLANES:
        keep = sub >= d
        a_sh = jnp.where(keep, pltpu.roll(a, d, 1), 1.0)
        b_sh = jnp.where(keep, pltpu.roll(b, d, 1), 0.0)
        b = b + a * b_sh
        a = a * a_sh
        d *= 2
    return a, b


def _odd_body(*refs, g, rt, prompt):
    (x_ref, sh_ref, sc_ref, gt_ref, gpre_ref, gpost_ref, win_ref, cw_ref, cb_ref,
     wa_ref, ba_ref, wx_ref, bx_ref, lam_ref, wout_ref) = refs[:15]
    if prompt:
        o_ref, tail_out_ref, h_out_ref, tail_ref, hcar_ref = refs[15:]
    else:
        prev_ref, h0_ref, o_ref, xb_out_ref, hs_out_ref = refs[15:]
    rows = g * rt
    n_grp = rows // SUBLANES
    x = x_ref[...]
    h = _prenorm(x, gpre_ref[...], sh_ref[...], sc_ref[...], g, rt).astype(BF16)
    proj = _dot(h, win_ref[...])
    gate_br = proj[:, :D]
    xb = proj[:, D:]
    row = lax.broadcasted_iota(jnp.int32, (rows, 1), 0)
    cw = cw_ref[...]

    if prompt:
        j = pl.program_id(1)

        @pl.when(j == 0)
        def _():
            tail_ref[...] = jnp.zeros_like(tail_ref)
            hcar_ref[...] = jnp.zeros_like(hcar_ref)

        xext = jnp.concatenate([tail_ref[...], xb], axis=0)
        conv = xb * cw[CONV_W - 1:CONV_W] + cb_ref[...]
        for jj in range(CONV_W - 1):
            d = CONV_W - 1 - jj
            conv = conv + pltpu.roll(xext, d, 0)[SUBLANES:] * cw[jj:jj + 1]
        tail_ref[...] = xb[rows - SUBLANES:]
        tail_out_ref[0] = xb[rows - SUBLANES:]
    else:
        pos = jnp.bitwise_and(row, rt - 1)
        prev = prev_ref[...]
        conv = xb * cw[CONV_W - 1:CONV_W] + cb_ref[...]
        for jj in range(CONV_W - 1):
            d = CONV_W - 1 - jj
            tap = jnp.where(pos >= d, pltpu.roll(xb, d, 0), pltpu.roll(prev, rows - (SUBLANES - d), 0))
            conv = conv + tap * cw[jj:jj + 1]
        xb_out_ref[...] = xb

    convb = conv.astype(BF16)
    ra = jnp.concatenate([_dot(convb[:, n * RG_BW:(n + 1) * RG_BW], wa_ref[n]) for n in range(RG_BLOCKS)], axis=-1)
    rx = jnp.concatenate([_dot(convb[:, n * RG_BW:(n + 1) * RG_BW], wx_ref[n]) for n in range(RG_BLOCKS)], axis=-1)
    r = _sigmoid(ra + ba_ref[...])
    ig = _sigmoid(rx + bx_ref[...])
    log_a = (-RG_C) * r * _softplus(-lam_ref[...])
    a = jnp.exp(log_a)
    mult = jnp.sqrt(1.0 - a * a)
    if prompt:
        mult = jnp.where(jnp.logical_and(j == 0, row == 0), 1.0, mult)
    b = mult * (ig * conv)
    if not prompt:
        b = b + a * h0_ref[...]
    a, b = _scan_sublanes(a.reshape(n_grp, SUBLANES, D), b.reshape(n_grp, SUBLANES, D))
    if prompt:
        carry = hcar_ref[0:1]
        parts = []
        for r in range(n_grp):
            part = b[r] + a[r] * carry
            carry = part[SUBLANES - 1:SUBLANES]
            parts.append(part)
        hs = jnp.concatenate(parts, axis=0)
        hcar_ref[0:1] = carry
        h_out_ref[0] = parts[-1]
    else:
        hs = b.reshape(rows, D)
        hs_out_ref[...] = hs
    y = (hs * _gelu_tanh(gate_br)).astype(BF16)
    out = _dot(y, wout_ref[...])
    o_ref[...] = x + _gated(_rms(out, gpost_ref[...]), gt_ref[...], g, rt)


def _odd(t, prompt, x, mod, mcol, l, e, p, prev=None, h0=None):
    vec = t.sel((None, 1, D), (e, 0, 0))
    rgw = t.sel((None, RG_BLOCKS, RG_BW, RG_BW), (e, 0, 0, 0))
    in_specs = [t.tok(D), t.mod(mcol), t.mod(mcol + 1), t.mod(mcol + 2),
                t.sel((None, None, 1, D), (l, 1, 0, 0)), t.sel((None, None, 1, D), (l, 1, 0, 0)),
                t.sel((None, D, 2 * D), (e, 0, 0)), t.sel((None, CONV_W, D), (e, 0, 0)), vec,
                rgw, vec, rgw, vec, vec, t.sel((None, D, D), (e, 0, 0))]
    args = [x, mod, mod, mod, p["g_pre"], p["g_post"], p["w_in_odd"], p["conv_w"], p["conv_b"],
            p["w_rg_a"], p["b_rg_a"], p["w_rg_x"], p["b_rg_x"], p["rg_lambda"], p["w_out_odd"]]
    if prompt:
        last = pl.BlockSpec((1, SUBLANES, D), lambda i, j: (i, 0, 0))
        out_shape = [jax.ShapeDtypeStruct((t.n_tok, D), F32),
                     jax.ShapeDtypeStruct((t.G, SUBLANES, D), F32),
                     jax.ShapeDtypeStruct((t.G, SUBLANES, D), F32)]
        out_specs = [t.tok(D), last, last]
        scratch = [pltpu.VMEM((SUBLANES, D), F32), pltpu.VMEM((SUBLANES, D), F32)]
    else:
        in_specs += [t.tok(D), t.tok(D)]
        args += [prev, h0]
        out_shape = [jax.ShapeDtypeStruct((t.n_tok, D), F32)] * 3
        out_specs = [t.tok(D)] * 3
        scratch = []
    return pl.pallas_call(
        functools.partial(_odd_body, g=t.g, rt=t.rt, prompt=prompt),
        out_shape=out_shape, grid=t.grid, in_specs=in_specs, out_specs=out_specs,
        scratch_shapes=scratch, compiler_params=_cparams(2),
        name="odd_prompt" if prompt else "odd_sample",
    )(*args)


def _rope_tables(pos):
    posf = pos.astype(F32)[:, None]
    inv_a = jnp.power(jnp.float32(ROPE_THETA), -jnp.arange(0, ROT_DIM, 2, dtype=F32) / ROT_DIM)
    ang = posf * inv_a[None, :]
    cos, sin = jnp.cos(ang), jnp.sin(ang)
    n = pos.shape[0]
    half = ROT_DIM // 2
    rest = A_DH - ROT_DIM
    one, zero = jnp.ones((n, rest), F32), jnp.zeros((n, rest), F32)
    zh = jnp.zeros((n, half), F32)
    cos_h = jnp.concatenate([cos, cos, one], axis=1)
    s1_h = jnp.concatenate([-sin, zh, zero], axis=1)
    s2_h = jnp.concatenate([zh, sin, zero], axis=1)
    both = lambda a: jnp.concatenate([a, a], axis=1)
    wi_scale = jnp.full((n, A_DH), IDX_HEADS ** -0.5, F32)
    z64 = jnp.zeros((n, A_DH), F32)
    rope_a = jnp.stack([both(cos_h), both(s1_h), both(s2_h),
                        jnp.concatenate([cos_h, wi_scale], axis=1),
                        jnp.concatenate([s1_h, z64], axis=1),
                        jnp.concatenate([s2_h, z64], axis=1)])
    inv_r = jnp.power(jnp.float32(RET_THETA), -jnp.linspace(0.0, 1.0, RET_DH // 2, dtype=F32))
    ang_r = posf * inv_r[None, :]
    cr, sr = jnp.cos(ang_r), jnp.sin(ang_r)
    rope_r = jnp.stack([jnp.concatenate([cr, cr], axis=1), jnp.concatenate([-sr, sr], axis=1)])
    return rope_a, rope_r


def _pack_w_in_even(w):
    o = np.cumsum((0, 512, 128, 128, 256, IDX_HEADS, IDX_DIM, 512, 512, 512, 512))
    pad = jnp.zeros(w.shape[:2] + (LANES - IDX_DIM - IDX_HEADS,), w.dtype)
    return jnp.concatenate([w[..., o[0]:o[4]], w[..., o[5]:o[6]], w[..., o[4]:o[5]], pad, w[..., o[6]:o[10]]],
                           axis=-1).astype(BF16)


def kernel(x_prompt, x_sample, cache_k, cache_v, cache_kidx, state_ret, state_conv, state_rglru, page_table,
           c_prompt, c_sample, w_ada, b_ada, g_pre, g_post, w_ffn_gate, w_ffn_up, w_ffn_down,
           w_in_even, w_out_even, w_in_odd, w_out_odd, conv_w, conv_b, w_rg_a, b_rg_a, w_rg_x, b_rg_x, rg_lambda):
    B, L, _ = x_prompt.shape
    Bs, T, _ = x_sample.shape
    n_pages = page_table.shape[1]
    past = n_pages * cache_k.shape[2]
    n_phys = cache_k.shape[1]
    assert cache_k.shape[2] == PAGE and T == SUBLANES

    tp = _Tiling(B, L, 1, 512)
    ts = _Tiling(Bs, T, 64, T)
    tp_odd = _Tiling(B, L, 1, 256)
    ts_odd = _Tiling(Bs, T, 32, T)

    vec3 = lambda a: a.reshape(a.shape[0], 1, a.shape[1])
    p = {
        "g_pre": g_pre.reshape(DEPTH, 3, 1, D), "g_post": g_post.reshape(DEPTH, 3, 1, D),
        "wg": w_ffn_gate.astype(BF16), "wu": w_ffn_up.astype(BF16), "wd": w_ffn_down.astype(BF16),
        "w_in_even": _pack_w_in_even(w_in_even), "w_out_even": w_out_even.astype(BF16),
        "w_in_odd": w_in_odd.astype(BF16), "w_out_odd": w_out_odd.astype(BF16),
        "conv_w": conv_w, "conv_b": vec3(conv_b),
        "w_rg_a": w_rg_a.astype(BF16), "b_rg_a": vec3(b_rg_a),
        "w_rg_x": w_rg_x.astype(BF16), "b_rg_x": vec3(b_rg_x), "rg_lambda": vec3(rg_lambda),
    }
    k_t = jnp.transpose(cache_k, (0, 1, 3, 4, 2)).reshape(cache_k.shape[0], n_phys, LANES, PAGE)
    v_t = jnp.transpose(cache_v, (0, 1, 3, 4, 2)).reshape(cache_v.shape[0], n_phys, LANES, PAGE)
    kidx_t = jnp.transpose(cache_kidx, (0, 1, 3, 2))

    mod_all = _ada(jnp.concatenate([c_prompt, c_sample], axis=0), w_ada, b_ada)
    mod_p = mod_all[:, :B].reshape(DEPTH, B, 1, N_MOD * D)
    mod_s = mod_all[:, B:].reshape(DEPTH, Bs, 1, N_MOD * D)

    rope_a_p, rope_r_p = _rope_tables(jnp.arange(L, dtype=jnp.int32))
    rope_a_s, rope_r_s = _rope_tables(jnp.tile(past + jnp.arange(T, dtype=jnp.int32), ts.g))
    rope_map_p = lambda i, j: (0, j, 0)
    rope_map_s = lambda i, j: (0, 0, 0)

    xp = x_prompt.reshape(B * L, D)
    xs = x_sample.reshape(Bs * T, D)
    ks, vs, kis, rets, convs, hs = ([[], []] for _ in range(6))

    for l in range(DEPTH):
        e = l // 2
        groups = ((0, tp, xp, mod_p[l]), (1, ts, xs, mod_s[l]))
        new_x = []
        for gi, t, x, mod in groups:
            x = _ffn(t, x, mod, 0, l, 0, 0, p)
            if l % 2 == 0:
                if gi == 0:
                    q, kt, v, qi, kw, kit, rq, rk, rv, rg = _even_in(
                        t, True, x, mod, 3, l, e, p, rope_a_p, rope_r_p, rope_map_p)
                    o_a = _dsa_prompt(B, L, q, qi, kw, kt, v)
                    o_r, s_new = _ret_prompt(B, L, rq, rk, rv, rg)
                    ks[0].append(jnp.transpose(kt.reshape(B, A_KV, A_DH, L), (0, 3, 1, 2)))
                    vs[0].append(v.reshape(B, L, A_KV, A_DH))
                    kis[0].append(jnp.transpose(kit, (0, 2, 1)))
                else:
                    q, k, v, qi, kw, rq, rk, rv, rg = _even_in(
                        t, False, x, mod, 3, l, e, p, rope_a_s, rope_r_s, rope_map_s)
                    scores = _dsa_s_scores(Bs, T, e, page_table, qi, kw, kidx_t)
                    bias = _dsa_s_select(Bs * T, T, past, scores)
                    o_a = _dsa_s_attn(Bs, T, e, page_table, q, k, v, bias, k_t, v_t)
                    o_r, s_new = _ret_sample(Bs, T, e, rq, rk, rv, rg, state_ret)
                    ks[1].append(k.reshape(Bs, T, A_KV, A_DH))
                    vs[1].append(v.reshape(Bs, T, A_KV, A_DH))
                    kis[1].append(kw[:, :IDX_DIM].reshape(Bs, T, IDX_DIM))
                rets[gi].append(s_new)
                x = _even_out(t, x, o_a, o_r, mod, 5, l, e, p)
            else:
                if gi == 0:
                    x, tail, hlast = _odd(tp_odd, True, x, mod, 3, l, e, p)
                    convs[0].append(tail[:, SUBLANES - (CONV_W - 1):])
                    hs[0].append(hlast[:, SUBLANES - 1])
                else:
                    prev = jnp.pad(state_conv[e], ((0, 0), (SUBLANES - (CONV_W - 1), 0), (0, 0))).reshape(Bs * T, D)
                    h0 = jnp.pad(state_rglru[e][:, None, :], ((0, 0), (0, T - 1), (0, 0))).reshape(Bs * T, D)
                    x, xb, hseq = _odd(ts_odd, False, x, mod, 3, l, e, p, prev=prev, h0=h0)
                    convs[1].append(xb.reshape(Bs, T, D)[:, T - (CONV_W - 1):])
                    hs[1].append(hseq.reshape(Bs, T, D)[:, T - 1])
            x = _ffn(t, x, mod, 6, l, 1, 2, p)
            new_x.append(x)
        xp, xs = new_x

    st = lambda lists, gi: jnp.stack(lists[gi])
    return (xp.reshape(B, L, D), xs.reshape(Bs, T, D),
            st(ks, 0), st(vs, 0), st(kis, 0), st(rets, 0), st(convs, 0), st(hs, 0),
            st(ks, 1), st(vs, 1), st(kis, 1), st(rets, 1), st(convs, 1), st(hs, 1))
```

```python
import functools
import math

import jax
import jax.numpy as jnp
import numpy as np
from jax import lax
from jax.experimental import pallas as pl
from jax.experimental.pallas import tpu as pltpu

F32 = jnp.float32
BF16 = jnp.bfloat16

D = 1024
DEPTH = 4
N_MOD = 9
A_HEADS = 8
A_KV = 2
A_DH = 64
ROT_DIM = 16
ROPE_THETA = 500000.0
IDX_HEADS = 4
IDX_DIM = 64
TOPK_MAX = 256
RET_HEADS = 4
RET_DH = 128
RET_THETA = 10000.0
RG_BLOCKS = 8
RG_BW = 128
CONV_W = 4
RG_C = 8.0
D_FF = 2816
EPS = 1e-6
P_IN = 3200
PAGE = 128

LANES = 128
SUBLANES = 8
VMEM_LIMIT = 56 * 1024 * 1024

NEG_INF = float("-inf")


def _cparams(n_axes, vmem=VMEM_LIMIT):
    return pltpu.CompilerParams(dimension_semantics=("arbitrary",) * n_axes, vmem_limit_bytes=vmem)


def _dot(a, b):
    return jnp.dot(a, b, preferred_element_type=F32)


def _dot_nt(a, b):
    return lax.dot_general(a, b, (((1,), (1,)), ((), ())), preferred_element_type=F32)


def _dot_tn(a, b):
    return lax.dot_general(a, b, (((0,), (0,)), ((), ())), preferred_element_type=F32)


def _sigmoid(x):
    return 1.0 / (1.0 + jnp.exp(-x))


def _silu(x):
    return x * _sigmoid(x)


def _gelu_tanh(x):
    return 0.5 * x * (1.0 + jnp.tanh(math.sqrt(2.0 / math.pi) * (x + 0.044715 * (x * x * x))))


def _rms(x, g=None):
    y = x * lax.rsqrt(jnp.mean(x * x, axis=-1, keepdims=True) + EPS)
    return y if g is None else y * g


def _prenorm(x, g_pre, shift, scale, g, rt):
    h = _rms(x, g_pre)
    if g == 1:
        return h * (1.0 + scale[0]) + shift[0]
    h3 = h.reshape(g, rt, D)
    return (h3 * (1.0 + scale) + shift).reshape(g * rt, D)


def _gated(y, gate, g, rt):
    if g == 1:
        return y * gate[0]
    return (y.reshape(g, rt, D) * gate).reshape(g * rt, D)


class _Tiling:
    def __init__(self, G, R, g, rt):
        assert G % g == 0 and R % rt == 0 and (g == 1 or rt == R)
        self.G, self.R, self.g, self.rt = G, R, g, rt
        self.nr = R // rt
        self.rows = g * rt
        self.grid = (G // g, self.nr)
        self.n_tok = G * R

    def tok(self, width):
        nr = self.nr
        return pl.BlockSpec((self.rows, width), lambda i, j: (i * nr + j, 0))

    def mod(self, c):
        return pl.BlockSpec((self.g, 1, D), lambda i, j: (i, 0, c))

    def sel(self, block, idx):
        return pl.BlockSpec(block, lambda i, j: idx, pipeline_mode=pl.Buffered(1))


def _ada_body(c_ref, w_ref, b_ref, o_ref):
    c = _silu(c_ref[...]).astype(BF16)
    o_ref[0] = _dot(c, w_ref[0].astype(BF16)) + b_ref[0]


def _ada(c_all, w_ada, b_ada):
    n = c_all.shape[0]
    return pl.pallas_call(
        _ada_body,
        out_shape=jax.ShapeDtypeStruct((DEPTH, n, N_MOD * D), F32),
        grid=(DEPTH, N_MOD),
        in_specs=[pl.BlockSpec((n, D), lambda l, j: (0, 0)),
                  pl.BlockSpec((1, D, D), lambda l, j: (l, 0, j)),
                  pl.BlockSpec((1, 1, D), lambda l, j: (l, 0, j))],
        out_specs=pl.BlockSpec((1, n, D), lambda l, j: (l, 0, j)),
        compiler_params=_cparams(2),
        name="ada",
    )(c_all, w_ada, b_ada.reshape(DEPTH, 1, N_MOD * D))


def _ffn_body(x_ref, sh_ref, sc_ref, gt_ref, gpre_ref, gpost_ref, wg_ref, wu_ref, wd_ref, o_ref, *, g, rt):
    x = x_ref[...]
    h = _prenorm(x, gpre_ref[...], sh_ref[...], sc_ref[...], g, rt).astype(BF16)
    a = _dot(h, wg_ref[...])
    u = _dot(h, wu_ref[...])
    act = (_silu(a) * u).astype(BF16)
    f = _dot(act, wd_ref[...])
    y = _rms(f, gpost_ref[...])
    o_ref[...] = x + 0.5 * _gated(y, gt_ref[...], g, rt)


def _ffn(t, x, mod, mcol, l, s, n, p):
    return pl.pallas_call(
        functools.partial(_ffn_body, g=t.g, rt=t.rt),
        out_shape=jax.ShapeDtypeStruct((t.n_tok, D), F32),
        grid=t.grid,
        in_specs=[t.tok(D), t.mod(mcol), t.mod(mcol + 1), t.mod(mcol + 2),
                  t.sel((None, None, 1, D), (l, n, 0, 0)), t.sel((None, None, 1, D), (l, n, 0, 0)),
                  t.sel((None, None, D, D_FF), (l, s, 0, 0)), t.sel((None, None, D, D_FF), (l, s, 0, 0)),
                  t.sel((None, None, D_FF, D), (l, s, 0, 0))],
        out_specs=t.tok(D),
        compiler_params=_cparams(2),
        name="ffn",
    )(x, mod, mod, mod, p["g_pre"], p["g_post"], p["wg"], p["wu"], p["wd"])


def _even_in_body(*refs, g, rt, prompt):
    x_ref, sh_ref, sc_ref, gpre_ref, w_ref, ra_ref, rr_ref = refs[:7]
    q_ref, k_ref, v_ref, qi_ref, kw_ref = refs[7:12]
    if prompt:
        kit_ref, rq_ref, rk_ref, rv_ref, rg_ref = refs[12:]
    else:
        rq_ref, rk_ref, rv_ref, rg_ref = refs[12:]
    h = _prenorm(x_ref[...], gpre_ref[...], sh_ref[...], sc_ref[...], g, rt).astype(BF16)
    proj = _dot(h, w_ref[...])

    def sec(c):
        return proj[:, c * LANES:(c + 1) * LANES]

    def rope_a(t, v):
        return (t * ra_ref[3 * v] + pltpu.roll(t, LANES - ROT_DIM // 2, 1) * ra_ref[3 * v + 1]
                + pltpu.roll(t, ROT_DIM // 2, 1) * ra_ref[3 * v + 2])

    def rope_r(t):
        return t * rr_ref[0] + pltpu.roll(t, RET_DH // 2, 1) * rr_ref[1]

    for c in range(4):
        q_ref[:, c * LANES:(c + 1) * LANES] = rope_a(sec(c), 0) * (A_DH ** -0.5)
    k = rope_a(sec(4), 0)
    v_ref[...] = sec(5)
    for c in range(2):
        qi_ref[:, c * LANES:(c + 1) * LANES] = rope_a(sec(6 + c), 0) * (IDX_DIM ** -0.5)
    kw = rope_a(sec(8), 1)
    kw_ref[...] = kw
    if prompt:
        k_ref[...] = k.T
        kit_ref[...] = kw.T[0:IDX_DIM]
    else:
        k_ref[...] = k
    for c in range(4):
        rq_ref[:, c * LANES:(c + 1) * LANES] = rope_r(sec(9 + c))
        rk_ref[:, c * LANES:(c + 1) * LANES] = rope_r(sec(13 + c) * (RET_DH ** -0.5))
        rv_ref[:, c * LANES:(c + 1) * LANES] = sec(17 + c)
        rg_ref[:, c * LANES:(c + 1) * LANES] = sec(21 + c)


def _even_in(t, prompt, x, mod, mcol, l, e, p, rope_a, rope_r, rope_map):
    tokw = lambda w: (jax.ShapeDtypeStruct((t.n_tok, w), F32), t.tok(w))
    seq_t = lambda w: (jax.ShapeDtypeStruct((t.G, w, t.R), F32),
                       pl.BlockSpec((None, w, t.rt), lambda i, j: (i, 0, j)))
    outs = [tokw(512), seq_t(LANES) if prompt else tokw(LANES), tokw(LANES), tokw(256), tokw(LANES)]
    if prompt:
        outs.append(seq_t(IDX_DIM))
    outs += [tokw(512)] * 4
    return pl.pallas_call(
        functools.partial(_even_in_body, g=t.g, rt=t.rt, prompt=prompt),
        out_shape=[o[0] for o in outs],
        grid=t.grid,
        in_specs=[t.tok(D), t.mod(mcol), t.mod(mcol + 1), t.sel((None, None, 1, D), (l, 1, 0, 0)),
                  t.sel((None, D, P_IN), (e, 0, 0)),
                  pl.BlockSpec((6, t.rows, LANES), rope_map),
                  pl.BlockSpec((2, t.rows, LANES), rope_map)],
        out_specs=[o[1] for o in outs],
        compiler_params=_cparams(2),
        name="even_in_p" if prompt else "even_in_s",
    )(x, mod, mod, p["g_pre"], p["w_in_even"], rope_a, rope_r)


def _sortable(s):
    s = jnp.where(s == 0.0, 0.0, s)
    b = lax.bitcast_convert_type(s, jnp.int32)
    return b ^ ((b >> 31) & 0x7FFFFFFF)


def _topk_bias_t(scores_t, visible_t, k):
    s_len, n_q = scores_t.shape
    key = _sortable(scores_t)
    fold = 8 * SUBLANES
    assert s_len % fold == 0 and s_len % LANES == 0
    i16 = jnp.int16

    def count(mask):
        part = jnp.sum(jnp.where(mask, 1.0, 0.0).reshape(s_len // fold, fold, n_q), axis=0)
        return jnp.sum(part, axis=0, keepdims=True)

    def count16(mask):
        ind = jnp.where(mask, i16(1), i16(0))
        part = ind[0:LANES]
        for r in range(1, s_len // LANES):
            part = part + ind[r * LANES:(r + 1) * LANES]
        return jnp.sum(part.astype(jnp.int32).astype(F32), axis=0, keepdims=True)

    def kth16(vals, kk):
        def step(it, t):
            cand = t + jnp.left_shift(jnp.int32(1), 15 - it)
            return jnp.where(count16(vals >= cand.astype(i16)) >= kk, cand, t)
        return lax.fori_loop(0, 16, step, jnp.full((1, n_q), -(2 ** 15), jnp.int32))

    hi = (key >> 16).astype(i16)
    t_hi = kth16(hi, k)
    t_hi16 = t_hi.astype(i16)
    lo = jnp.where(hi == t_hi16, ((key & 0xFFFF) - 2 ** 15).astype(i16), i16(-(2 ** 15)))
    t_lo = kth16(lo, k - count16(hi > t_hi16))
    thr = jnp.left_shift(t_hi, 16) + (t_lo + 2 ** 15)
    ge = key >= thr
    cnt_ge = count(ge)

    def no_ties():
        return jnp.where(visible_t, jnp.where(ge, 0.0, NEG_INF), NEG_INF)

    def ties():
        need = k - count(key > thr)
        tri = jnp.where(lax.broadcasted_iota(jnp.int32, (LANES, LANES), 0)
                        >= lax.broadcasted_iota(jnp.int32, (LANES, LANES), 1), 1.0, 0.0).astype(BF16)
        off = jnp.zeros((1, n_q), F32)
        pieces = []
        for c in range(s_len // LANES):
            sl = slice(c * LANES, (c + 1) * LANES)
            kc = key[sl, :]
            eq = kc == thr
            pc = _dot(tri, jnp.where(eq, 1.0, 0.0).astype(BF16)) + off
            off = pc[LANES - 1:LANES, :]
            tie_ok = jnp.where(eq, jnp.where(pc <= need, 0.0, NEG_INF), NEG_INF)
            b = jnp.where(kc > thr, 0.0, tie_ok)
            pieces.append(jnp.where(visible_t[sl, :], b, NEG_INF))
        return jnp.concatenate(pieces, axis=0)

    return lax.cond(jnp.max(cnt_ge) > k, ties, no_ties)


def _softmax(logits):
    m = jnp.max(logits, axis=-1, keepdims=True)
    p = jnp.exp(logits - m)
    return p, jnp.sum(p, axis=-1, keepdims=True)


def _dsa_p_body(q_ref, qi_ref, kw_ref, kt_ref, v_ref, o_ref, *, tq, topk):
    s_len = q_ref.shape[0]
    rep = A_HEADS // A_KV
    kt = kt_ref[...].astype(BF16)
    ki = kw_ref[:, 0:IDX_DIM].astype(BF16)
    v = v_ref[...].astype(BF16)
    for i in range(s_len // tq):
        rows = slice(i * tq, (i + 1) * tq)
        s_vis = (i + 1) * tq
        qit = qi_ref[rows, :].T.astype(BF16)
        wit = kw_ref[rows, :].T
        sc = jnp.zeros((s_vis, tq), F32)
        for h in range(IDX_HEADS):
            raw = _dot(ki[:s_vis], qit[h * IDX_DIM:(h + 1) * IDX_DIM, :])
            sc = sc + jnp.maximum(raw, 0.0) * wit[IDX_DIM + h:IDX_DIM + h + 1, :]
        kpos = lax.broadcasted_iota(jnp.int32, (s_vis, 1), 0)
        qpos = i * tq + lax.broadcasted_iota(jnp.int32, (1, tq), 1)
        visible = kpos <= qpos
        bias = _topk_bias_t(jnp.where(visible, sc, NEG_INF), visible, topk).T
        q = q_ref[rows, :].astype(BF16)
        for hp in range(A_HEADS // 2):
            outs = []
            for h in (2 * hp, 2 * hp + 1):
                gsl = slice((h // rep) * A_DH, (h // rep + 1) * A_DH)
                p, l = _softmax(_dot(q[:, h * A_DH:(h + 1) * A_DH], kt[gsl, :s_vis]) + bias)
                outs.append(_dot(p.astype(BF16), v[:s_vis, gsl]) / l)
            o_ref[rows, hp * LANES:(hp + 1) * LANES] = jnp.concatenate(outs, axis=-1)


def _dsa_prompt(B, L, q, qi, kw, kt, v, tq=256):
    tok = lambda w: pl.BlockSpec((L, w), lambda b: (b, 0))
    seq_t = lambda w: pl.BlockSpec((None, w, L), lambda b: (b, 0, 0))
    return pl.pallas_call(
        functools.partial(_dsa_p_body, tq=tq, topk=min(TOPK_MAX, L // 4)),
        out_shape=jax.ShapeDtypeStruct((B * L, A_HEADS * A_DH), F32),
        grid=(B,),
        in_specs=[tok(512), tok(256), tok(LANES), seq_t(LANES), tok(LANES)],
        out_specs=tok(512),
        compiler_params=_cparams(1),
        name="dsa_prompt",
    )(q, qi, kw, kt, v)


SEQ_PER_STEP = 4


def _page_gather(pt_ref, pool_ref, buf_ref, sem_ref, e, n_pages):
    b = pl.program_id(0)

    def copies(step, slot):
        return [pltpu.make_async_copy(pool_ref.at[e, pt_ref[step * SEQ_PER_STEP + s, p]],
                                      buf_ref.at[slot, s, :, pl.ds(p * PAGE, PAGE)],
                                      sem_ref.at[slot])
                for s in range(SEQ_PER_STEP) for p in range(n_pages)]

    @pl.when(b == 0)
    def _():
        for c in copies(0, 0):
            c.start()

    @pl.when(b + 1 < pl.num_programs(0))
    def _():
        for c in copies(b + 1, (b + 1) % 2):
            c.start()

    slot = b % 2
    for c in copies(b, slot):
        c.wait()
    return slot


def _dsa_s_scores_body(pt_ref, qi_ref, kw_ref, pool_ref, o_ref, buf_ref, sem_ref, *, e, n_pages, t):
    slot = _page_gather(pt_ref, pool_ref, buf_ref, sem_ref, e, n_pages)
    past = n_pages * PAGE
    lane = lax.broadcasted_iota(jnp.int32, (t, PAGE), 1)
    row = lax.broadcasted_iota(jnp.int32, (t, PAGE), 0)
    for s in range(SEQ_PER_STEP):
        rows = slice(s * t, (s + 1) * t)
        qi = qi_ref[rows, :]
        qs = jnp.concatenate([qi[:, h * IDX_DIM:(h + 1) * IDX_DIM] for h in range(IDX_HEADS)], axis=0).astype(BF16)
        kw = kw_ref[rows, :]
        wi = kw[:, IDX_DIM:LANES]
        ki_new = jnp.concatenate([kw[:, 0:IDX_DIM], jnp.zeros((PAGE - t, IDX_DIM), F32)], axis=0).astype(BF16)
        kit = buf_ref[slot, s].astype(BF16)

        def weigh(raw):
            raw = jnp.maximum(raw, 0.0)
            sc = raw[0:t] * wi[:, 0:1]
            for h in range(1, IDX_HEADS):
                sc = sc + raw[h * t:(h + 1) * t] * wi[:, h:h + 1]
            return sc

        o_ref[rows, :past] = weigh(_dot(qs, kit))
        o_ref[rows, past:] = jnp.where(lane <= row, weigh(_dot_nt(qs, ki_new)), NEG_INF)


def _dsa_s_scores(Bs, T, e, page_table, qi, kw, kidx_t):
    n_pages = page_table.shape[1]
    tok = lambda w: pl.BlockSpec((SEQ_PER_STEP * T, w), lambda b, pt: (b, 0))
    s_pad = (n_pages + 1) * PAGE
    return pl.pallas_call(
        functools.partial(_dsa_s_scores_body, e=e, n_pages=n_pages, t=T),
        out_shape=jax.ShapeDtypeStruct((Bs * T, s_pad), F32),
        grid_spec=pltpu.PrefetchScalarGridSpec(
            num_scalar_prefetch=1, grid=(Bs // SEQ_PER_STEP,),
            in_specs=[tok(256), tok(LANES), pl.BlockSpec(memory_space=pl.ANY)],
            out_specs=tok(s_pad),
            scratch_shapes=[pltpu.VMEM((2, SEQ_PER_STEP, IDX_DIM, n_pages * PAGE), F32),
                            pltpu.SemaphoreType.DMA((2,))]),
        compiler_params=_cparams(1),
        name="dsa_s_scores",
    )(page_table, qi, kw, kidx_t)


def _dsa_s_select_body(sc_ref, o_ref, *, t, past, topk):
    rows, s_pad = sc_ref.shape
    tpos = jnp.bitwise_and(lax.broadcasted_iota(jnp.int32, (1, rows), 1), t - 1)
    kpos = lax.broadcasted_iota(jnp.int32, (s_pad, 1), 0)
    visible = kpos <= past + tpos
    o_ref[...] = _topk_bias_t(sc_ref[...].T, visible, topk).T


def _dsa_s_select(n_rows, T, past, scores, rows=256):
    s_pad = scores.shape[1]
    spec = pl.BlockSpec((rows, s_pad), lambda i: (i, 0))
    return pl.pallas_call(
        functools.partial(_dsa_s_select_body, t=T, past=past, topk=min(TOPK_MAX, (past + T) // 4)),
        out_shape=jax.ShapeDtypeStruct((n_rows, s_pad), F32),
        grid=(n_rows // rows,),
        in_specs=[spec], out_specs=spec,
        compiler_params=_cparams(1),
        name="dsa_s_select",
    )(scores)


def _dsa_s_attn_body(pt_ref, q_ref, kn_ref, vn_ref, b_ref, kpool_ref, vpool_ref, o_ref,
                     kbuf_ref, vbuf_ref, ksem_ref, vsem_ref, *, e, n_pages, t):
    slot = _page_gather(pt_ref, kpool_ref, kbuf_ref, ksem_ref, e, n_pages)
    _page_gather(pt_ref, vpool_ref, vbuf_ref, vsem_ref, e, n_pages)
    past = n_pages * PAGE
    rep = A_HEADS // A_KV
    pad = jnp.zeros((PAGE - t, LANES), F32)
    for s in range(SEQ_PER_STEP):
        rows = slice(s * t, (s + 1) * t)
        kt = kbuf_ref[slot, s].astype(BF16)
        vt = vbuf_ref[slot, s].astype(BF16)
        k_new = jnp.concatenate([kn_ref[rows, :], pad], axis=0).astype(BF16)
        v_new = jnp.concatenate([vn_ref[rows, :], pad], axis=0).astype(BF16)
        q = q_ref[rows, :].astype(BF16)
        bias = jnp.concatenate([b_ref[rows, :]] * rep, axis=0)
        outs = []
        for gi in range(A_KV):
            gsl = slice(gi * A_DH, (gi + 1) * A_DH)
            qg = jnp.concatenate([q[:, (gi * rep + r) * A_DH:(gi * rep + r + 1) * A_DH] for r in range(rep)], axis=0)
            logits = jnp.concatenate([_dot(qg, kt[gsl, :]), _dot_nt(qg, k_new[:, gsl])], axis=-1) + bias
            p, l = _softmax(logits)
            p = p.astype(BF16)
            og = (_dot_nt(p[:, :past], vt[gsl, :]) + _dot(p[:, past:], v_new[:, gsl])) / l
            outs += [og[r * t:(r + 1) * t] for r in range(rep)]
        for hp in range(A_HEADS // 2):
            o_ref[rows, hp * LANES:(hp + 1) * LANES] = jnp.concatenate(outs[2 * hp:2 * hp + 2], axis=-1)


def _dsa_s_attn(Bs, T, e, page_table, q, k_new, v_new, bias, k_t, v_t):
    n_pages = page_table.shape[1]
    tok = lambda w: pl.BlockSpec((SEQ_PER_STEP * T, w), lambda b, pt: (b, 0))
    pool = pl.BlockSpec(memory_space=pl.ANY)
    buf = pltpu.VMEM((2, SEQ_PER_STEP, LANES, n_pages * PAGE), F32)
    return pl.pallas_call(
        functools.partial(_dsa_s_attn_body, e=e, n_pages=n_pages, t=T),
        out_shape=jax.ShapeDtypeStruct((Bs * T, A_HEADS * A_DH), F32),
        grid_spec=pltpu.PrefetchScalarGridSpec(
            num_scalar_prefetch=1, grid=(Bs // SEQ_PER_STEP,),
            in_specs=[tok(512), tok(LANES), tok(LANES), tok(bias.shape[1]), pool, pool],
            out_specs=tok(512),
            scratch_shapes=[buf, buf, pltpu.SemaphoreType.DMA((2,)), pltpu.SemaphoreType.DMA((2,))]),
        compiler_params=_cparams(1),
        name="dsa_s_attn",
    )(page_table, q, k_new, v_new, bias, k_t, v_t)


def _ret_log_gamma(h):
    return math.log1p(-(2.0 ** (-5.0 - h)))


def _ret_chunk(q, k, v, rg, s, h, c):
    lg = _ret_log_gamma(h)
    ri = lax.broadcasted_iota(jnp.int32, (c, c), 0)
    ci = lax.broadcasted_iota(jnp.int32, (c, c), 1)
    diff = (ri - ci).astype(F32)
    decay = jnp.where(diff >= 0, jnp.exp(jnp.maximum(diff, 0.0) * lg), 0.0)
    idx = lax.broadcasted_iota(jnp.int32, (c, 1), 0).astype(F32)
    q_dec = jnp.exp((idx + 1.0) * lg)
    k_dec = jnp.exp((c - 1.0 - idx) * lg)
    c_dec = math.exp(c * lg)
    qb = q.astype(BF16)
    att = _dot_nt(qb, k.astype(BF16)) * decay
    inner = _dot(att.astype(BF16), v.astype(BF16))
    cross = _dot(qb, s.astype(BF16)) * q_dec
    s_new = s * c_dec + _dot_tn((k * k_dec).astype(BF16), v.astype(BF16))
    return _rms(inner + cross) * _silu(rg), s_new


def _ret_p_body(rq_ref, rk_ref, rv_ref, rg_ref, o_ref, so_ref, s_ref, *, c):
    j = pl.program_id(1)

    @pl.when(j == 0)
    def _():
        s_ref[...] = jnp.zeros_like(s_ref)

    for h in range(RET_HEADS):
        sl = slice(h * RET_DH, (h + 1) * RET_DH)
        o, s_new = _ret_chunk(rq_ref[:, sl], rk_ref[:, sl], rv_ref[:, sl], rg_ref[:, sl], s_ref[h], h, c)
        o_ref[:, sl] = o
        s_ref[h] = s_new
    so_ref[0] = s_ref[...]


def _ret_prompt(B, L, rq, rk, rv, rg, c=256):
    nc = L // c
    tok = pl.BlockSpec((c, 512), lambda b, j: (b * nc + j, 0))
    return pl.pallas_call(
        functools.partial(_ret_p_body, c=c),
        out_shape=[jax.ShapeDtypeStruct((B * L, 512), F32),
                   jax.ShapeDtypeStruct((B, RET_HEADS, RET_DH, RET_DH), F32)],
        grid=(B, nc),
        in_specs=[tok, tok, tok, tok],
        out_specs=[tok, pl.BlockSpec((1, RET_HEADS, RET_DH, RET_DH), lambda b, j: (b, 0, 0, 0))],
        scratch_shapes=[pltpu.VMEM((RET_HEADS, RET_DH, RET_DH), F32)],
        compiler_params=_cparams(2),
        name="ret_prompt",
    )(rq, rk, rv, rg)


def _ret_s_body(rq_ref, rk_ref, rv_ref, rg_ref, s0_ref, o_ref, so_ref, *, t):
    for s in range(SEQ_PER_STEP):
        rows = slice(s * t, (s + 1) * t)
        for h in range(RET_HEADS):
            sl = slice(h * RET_DH, (h + 1) * RET_DH)
            o, s_new = _ret_chunk(rq_ref[rows, sl], rk_ref[rows, sl], rv_ref[rows, sl], rg_ref[rows, sl],
                                  s0_ref[s, h], h, t)
            o_ref[rows, sl] = o
            so_ref[s, h] = s_new


def _ret_sample(Bs, T, e, rq, rk, rv, rg, state_ret):
    tok = pl.BlockSpec((SEQ_PER_STEP * T, 512), lambda b: (b, 0))
    return pl.pallas_call(
        functools.partial(_ret_s_body, t=T),
        out_shape=[jax.ShapeDtypeStruct((Bs * T, 512), F32),
                   jax.ShapeDtypeStruct((Bs, RET_HEADS, RET_DH, RET_DH), F32)],
        grid=(Bs // SEQ_PER_STEP,),
        in_specs=[tok, tok, tok, tok,
                  pl.BlockSpec((None, SEQ_PER_STEP, RET_HEADS, RET_DH, RET_DH), lambda b: (e, b, 0, 0, 0))],
        out_specs=[tok, pl.BlockSpec((SEQ_PER_STEP, RET_HEADS, RET_DH, RET_DH), lambda b: (b, 0, 0, 0))],
        compiler_params=_cparams(1),
        name="ret_sample",
    )(rq, rk, rv, rg, state_ret)


def _even_out_body(x_ref, oa_ref, or_ref, gt_ref, gpost_ref, wa_ref, wr_ref, o_ref, *, g, rt):
    y = _dot(oa_ref[...].astype(BF16), wa_ref[...]) + _dot(or_ref[...].astype(BF16), wr_ref[...])
    o_ref[...] = x_ref[...] + _gated(_rms(y, gpost_ref[...]), gt_ref[...], g, rt)


def _even_out(t, x, o_a, o_r, mod, mcol, l, e, p):
    half = A_HEADS * A_DH
    return pl.pallas_call(
        functools.partial(_even_out_body, g=t.g, rt=t.rt),
        out_shape=jax.ShapeDtypeStruct((t.n_tok, D), F32),
        grid=t.grid,
        in_specs=[t.tok(D), t.tok(half), t.tok(half), t.mod(mcol), t.sel((None, None, 1, D), (l, 1, 0, 0)),
                  t.sel((None, half, D), (e, 0, 0)), t.sel((None, half, D), (e, 1, 0))],
        out_specs=t.tok(D),
        compiler_params=_cparams(2),
        name="even_out",
    )(x, o_a, o_r, mod, p["g_post"], p["w_out_even"], p["w_out_even"])


def _softplus(x):
    return jnp.maximum(x, 0.0) + jnp.log1p(jnp.exp(-jnp.abs(x)))


def _scan_sublanes(a, b):
    sub = lax.broadcasted_iota(jnp.int32, (1, SUBLANES, a.shape[-1]), 1)
    d = 1
    while d < SUBLANES:
        keep = sub >= d
        a_sh = jnp.where(keep, pltpu.roll(a, d, 1), 1.0)
        b_sh = jnp.where(keep, pltpu.roll(b, d, 1), 0.0)
        b = b + a * b_sh
        a = a * a_sh
        d *= 2
    return a, b


def _odd_body(*refs, g, rt, prompt):
    (x_ref, sh_ref, sc_ref, gt_ref, gpre_ref, gpost_ref, win_ref, cw_ref, cb_ref,
     wa_ref, ba_ref, wx_ref, bx_ref, lam_ref, wout_ref) = refs[:15]
    if prompt:
        o_ref, tail_out_ref, h_out_ref, tail_ref, hcar_ref = refs[15:]
    else:
        prev_ref, h0_ref, o_ref, xb_out_ref, hs_out_ref = refs[15:]
    rows = g * rt
    n_grp = rows // SUBLANES
    x = x_ref[...]
    h = _prenorm(x, gpre_ref[...], sh_ref[...], sc_ref[...], g, rt).astype(BF16)
    proj = _dot(h, win_ref[...])
    gate_br = proj[:, :D]
    xb = proj[:, D:]
    row = lax.broadcasted_iota(jnp.int32, (rows, 1), 0)
    cw = cw_ref[...]

    if prompt:
        j = pl.program_id(1)

        @pl.when(j == 0)
        def _():
            tail_ref[...] = jnp.zeros_like(tail_ref)
            hcar_ref[...] = jnp.zeros_like(hcar_ref)

        xext = jnp.concatenate([tail_ref[...], xb], axis=0)
        conv = xb * cw[CONV_W - 1:CONV_W] + cb_ref[...]
        for jj in range(CONV_W - 1):
            d = CONV_W - 1 - jj
            conv = conv + pltpu.roll(xext, d, 0)[SUBLANES:] * cw[jj:jj + 1]
        tail_ref[...] = xb[rows - SUBLANES:]
        tail_out_ref[0] = xb[rows - SUBLANES:]
    else:
        pos = jnp.bitwise_and(row, rt - 1)
        prev = prev_ref[...]
        conv = xb * cw[CONV_W - 1:CONV_W] + cb_ref[...]
        for jj in range(CONV_W - 1):
            d = CONV_W - 1 - jj
            tap = jnp.where(pos >= d, pltpu.roll(xb, d, 0), pltpu.roll(prev, rows - (SUBLANES - d), 0))
            conv = conv + tap * cw[jj:jj + 1]
        xb_out_ref[...] = xb

    convb = conv.astype(BF16)
    ra = jnp.concatenate([_dot(convb[:, n * RG_BW:(n + 1) * RG_BW], wa_ref[n]) for n in range(RG_BLOCKS)], axis=-1)
    rx = jnp.concatenate([_dot(convb[:, n * RG_BW:(n + 1) * RG_BW], wx_ref[n]) for n in range(RG_BLOCKS)], axis=-1)
    r = _sigmoid(ra + ba_ref[...])
    ig = _sigmoid(rx + bx_ref[...])
    log_a = (-RG_C) * r * _softplus(-lam_ref[...])
    a = jnp.exp(log_a)
    mult = jnp.sqrt(1.0 - a * a)
    if prompt:
        mult = jnp.where(jnp.logical_and(j == 0, row == 0), 1.0, mult)
    b = mult * (ig * conv)
    if not prompt:
        b = b + a * h0_ref[...]
    a, b = _scan_sublanes(a.reshape(n_grp, SUBLANES, D), b.reshape(n_grp, SUBLANES, D))
    if prompt:
        carry = hcar_ref[0:1]
        parts = []
        for r in range(n_grp):
            part = b[r] + a[r] * carry
            carry = part[SUBLANES - 1:SUBLANES]
            parts.append(part)
        hs = jnp.concatenate(parts, axis=0)
        hcar_ref[0:1] = carry
        h_out_ref[0] = parts[-1]
    else:
        hs = b.reshape(rows, D)
        hs_out_ref[...] = hs
    y = (hs * _gelu_tanh(gate_br)).astype(BF16)
    out = _dot(y, wout_ref[...])
    o_ref[...] = x + _gated(_rms(out, gpost_ref[...]), gt_ref[...], g, rt)


def _odd(t, prompt, x, mod, mcol, l, e, p, prev=None, h0=None):
    vec = t.sel((None, 1, D), (e, 0, 0))
    rgw = t.sel((None, RG_BLOCKS, RG_BW, RG_BW), (e, 0, 0, 0))
    in_specs = [t.tok(D), t.mod(mcol), t.mod(mcol + 1), t.mod(mcol + 2),
                t.sel((None, None, 1, D), (l, 1, 0, 0)), t.sel((None, None, 1, D), (l, 1, 0, 0)),
                t.sel((None, D, 2 * D), (e, 0, 0)), t.sel((None, CONV_W, D), (e, 0, 0)), vec,
                rgw, vec, rgw, vec, vec, t.sel((None, D, D), (e, 0, 0))]
    args = [x, mod, mod, mod, p["g_pre"], p["g_post"], p["w_in_odd"], p["conv_w"], p["conv_b"],
            p["w_rg_a"], p["b_rg_a"], p["w_rg_x"], p["b_rg_x"], p["rg_lambda"], p["w_out_odd"]]
    if prompt:
        last = pl.BlockSpec((1, SUBLANES, D), lambda i, j: (i, 0, 0))
        out_shape = [jax.ShapeDtypeStruct((t.n_tok, D), F32),
                     jax.ShapeDtypeStruct((t.G, SUBLANES, D), F32),
                     jax.ShapeDtypeStruct((t.G, SUBLANES, D), F32)]
        out_specs = [t.tok(D), last, last]
        scratch = [pltpu.VMEM((SUBLANES, D), F32), pltpu.VMEM((SUBLANES, D), F32)]
    else:
        in_specs += [t.tok(D), t.tok(D)]
        args += [prev, h0]
        out_shape = [jax.ShapeDtypeStruct((t.n_tok, D), F32)] * 3
        out_specs = [t.tok(D)] * 3
        scratch = []
    return pl.pallas_call(
        functools.partial(_odd_body, g=t.g, rt=t.rt, prompt=prompt),
        out_shape=out_shape, grid=t.grid, in_specs=in_specs, out_specs=out_specs,
        scratch_shapes=scratch, compiler_params=_cparams(2),
        name="odd_prompt" if prompt else "odd_sample",
    )(*args)


FF_HALF = D_FF // 2


def _odd_ffn_body(x_ref, sh_ref, sc_ref, gt_ref, sh2_ref, sc2_ref, gt2_ref,
                  gpre_ref, gpost_ref, gpre2_ref, gpost2_ref, win_ref, cw_ref, cb_ref,
                  wa_ref, ba_ref, wx_ref, bx_ref, lam_ref, wout_ref, wg_ref, wu_ref, wd_ref,
                  o_ref, tail_out_ref, h_out_ref, x1_ref, tail_ref, hcar_ref, *, rows, nt, n_tiles):
    t = pl.program_id(0)
    j = lax.rem(t, nt)
    slot = lax.rem(t, 2)
    n_grp = rows // SUBLANES

    @pl.when(t == 0)
    def _():
        x1_ref[...] = jnp.zeros_like(x1_ref)

    @pl.when(j == 0)
    def _():
        tail_ref[...] = jnp.zeros_like(tail_ref)
        hcar_ref[...] = jnp.zeros_like(hcar_ref)

    x = x_ref[...]
    h = _prenorm(x, gpre_ref[...], sh_ref[...], sc_ref[...], 1, rows).astype(BF16)
    proj = _dot(h, win_ref[...])
    gate_br = proj[:, :D]
    xb = proj[:, D:]
    row = lax.broadcasted_iota(jnp.int32, (rows, 1), 0)
    cw = cw_ref[...]
    xext = jnp.concatenate([tail_ref[...], xb], axis=0)
    conv = xb * cw[CONV_W - 1:CONV_W] + cb_ref[...]
    for jj in range(CONV_W - 1):
        conv = conv + pltpu.roll(xext, CONV_W - 1 - jj, 0)[SUBLANES:] * cw[jj:jj + 1]
    tail_ref[...] = xb[rows - SUBLANES:]
    convb = conv.astype(BF16)
    ra = jnp.concatenate([_dot(convb[:, n * RG_BW:(n + 1) * RG_BW], wa_ref[n]) for n in range(RG_BLOCKS)], axis=-1)
    rx = jnp.concatenate([_dot(convb[:, n * RG_BW:(n + 1) * RG_BW], wx_ref[n]) for n in range(RG_BLOCKS)], axis=-1)

    x1p = x1_ref[1 - slot]
    hf = _prenorm(x1p, gpre2_ref[...], sh2_ref[...], sc2_ref[...], 1, rows).astype(BF16)
    au = [(_dot(hf, wg_ref[:, c * FF_HALF:(c + 1) * FF_HALF]), _dot(hf, wu_ref[:, c * FF_HALF:(c + 1) * FF_HALF]))
          for c in range(2)]

    r = _sigmoid(ra + ba_ref[...])
    ig = _sigmoid(rx + bx_ref[...])
    log_a = (-RG_C) * r * _softplus(-lam_ref[...])
    a = jnp.exp(log_a)
    mult = jnp.sqrt(1.0 - a * a)
    mult = jnp.where(jnp.logical_and(j == 0, row == 0), 1.0, mult)
    b = mult * (ig * conv)
    a, b = _scan_sublanes(a.reshape(n_grp, SUBLANES, D), b.reshape(n_grp, SUBLANES, D))
    carry = hcar_ref[0:1]
    parts = []
    for g8 in range(n_grp):
        part = b[g8] + a[g8] * carry
        carry = part[SUBLANES - 1:SUBLANES]
        parts.append(part)
    hs = jnp.concatenate(parts, axis=0)
    hcar_ref[0:1] = carry
    y = (hs * _gelu_tanh(gate_br)).astype(BF16)

    act = [(_silu(a_c) * u_c).astype(BF16) for a_c, u_c in au]
    x1 = x + _rms(_dot(y, wout_ref[...]), gpost_ref[...]) * gt_ref[...][0]
    x1_ref[slot] = x1
    f = _dot(act[0], wd_ref[0:FF_HALF, :]) + _dot(act[1], wd_ref[FF_HALF:D_FF, :])
    o_ref[...] = x1p + 0.5 * (_rms(f, gpost2_ref[...]) * gt2_ref[...][0])

    @pl.when(t < n_tiles)
    def _():
        tail_out_ref[0] = xb[rows - SUBLANES:]
        h_out_ref[0] = parts[-1]


def _odd_ffn_prompt(B, L, x, mod, l, e, p, rows=512):
    nt = L // rows
    n_tiles = B * nt
    cur = lambda t: jnp.minimum(t, n_tiles - 1)
    prv = lambda t: jnp.maximum(t - 1, 0)
    sel = lambda block, idx: pl.BlockSpec(block, lambda t: idx, pipeline_mode=pl.Buffered(1))
    modc = lambda c, which: pl.BlockSpec((1, 1, D), lambda t: (which(t) // nt, 0, c))
    gain = lambda n: sel((None, None, 1, D), (l, n, 0, 0))
    vec = sel((None, 1, D), (e, 0, 0))
    rgw = sel((None, RG_BLOCKS, RG_BW, RG_BW), (e, 0, 0, 0))
    last = pl.BlockSpec((1, SUBLANES, D), lambda t: (cur(t) // nt, 0, 0))
    return pl.pallas_call(
        functools.partial(_odd_ffn_body, rows=rows, nt=nt, n_tiles=n_tiles),
        out_shape=[jax.ShapeDtypeStruct((B * L, D), F32),
                   jax.ShapeDtypeStruct((B, SUBLANES, D), F32),
                   jax.ShapeDtypeStruct((B, SUBLANES, D), F32)],
        grid=(n_tiles + 1,),
        in_specs=[pl.BlockSpec((rows, D), lambda t: (cur(t), 0)),
                  modc(3, cur), modc(4, cur), modc(5, cur), modc(6, prv), modc(7, prv), modc(8, prv),
                  gain(1), gain(1), gain(2), gain(2),
                  sel((None, D, 2 * D), (e, 0, 0)), sel((None, CONV_W, D), (e, 0, 0)), vec,
                  rgw, vec, rgw, vec, vec, sel((None, D, D), (e, 0, 0)),
                  sel((None, None, D, D_FF), (l, 1, 0, 0)), sel((None, None, D, D_FF), (l, 1, 0, 0)),
                  sel((None, None, D_FF, D), (l, 1, 0, 0))],
        out_specs=[pl.BlockSpec((rows, D), lambda t: (prv(t), 0)), last, last],
        scratch_shapes=[pltpu.VMEM((2, rows, D), F32), pltpu.VMEM((SUBLANES, D), F32), pltpu.VMEM((SUBLANES, D), F32)],
        compiler_params=_cparams(1),
        name="odd_ffn_prompt",
    )(x, mod, mod, mod, mod, mod, mod, p["g_pre"], p["g_post"], p["g_pre"], p["g_post"],
      p["w_in_odd"], p["conv_w"], p["conv_b"], p["w_rg_a"], p["b_rg_a"], p["w_rg_x"], p["b_rg_x"], p["rg_lambda"],
      p["w_out_odd"], p["wg"], p["wu"], p["wd"])


def _rope_tables(pos):
    posf = pos.astype(F32)[:, None]
    inv_a = jnp.power(jnp.float32(ROPE_THETA), -jnp.arange(0, ROT_DIM, 2, dtype=F32) / ROT_DIM)
    ang = posf * inv_a[None, :]
    cos, sin = jnp.cos(ang), jnp.sin(ang)
    n = pos.shape[0]
    half = ROT_DIM // 2
    rest = A_DH - ROT_DIM
    one, zero = jnp.ones((n, rest), F32), jnp.zeros((n, rest), F32)
    zh = jnp.zeros((n, half), F32)
    cos_h = jnp.concatenate([cos, cos, one], axis=1)
    s1_h = jnp.concatenate([-sin, zh, zero], axis=1)
    s2_h = jnp.concatenate([zh, sin, zero], axis=1)
    both = lambda a: jnp.concatenate([a, a], axis=1)
    wi_scale = jnp.full((n, A_DH), IDX_HEADS ** -0.5, F32)
    z64 = jnp.zeros((n, A_DH), F32)
    rope_a = jnp.stack([both(cos_h), both(s1_h), both(s2_h),
                        jnp.concatenate([cos_h, wi_scale], axis=1),
                        jnp.concatenate([s1_h, z64], axis=1),
                        jnp.concatenate([s2_h, z64], axis=1)])
    inv_r = jnp.power(jnp.float32(RET_THETA), -jnp.linspace(0.0, 1.0, RET_DH // 2, dtype=F32))
    ang_r = posf * inv_r[None, :]
    cr, sr = jnp.cos(ang_r), jnp.sin(ang_r)
    rope_r = jnp.stack([jnp.concatenate([cr, cr], axis=1), jnp.concatenate([-sr, sr], axis=1)])
    return rope_a, rope_r


def _pack_w_in_even(w):
    o = np.cumsum((0, 512, 128, 128, 256, IDX_HEADS, IDX_DIM, 512, 512, 512, 512))
    pad = jnp.zeros(w.shape[:2] + (LANES - IDX_DIM - IDX_HEADS,), w.dtype)
    return jnp.concatenate([w[..., o[0]:o[4]], w[..., o[5]:o[6]], w[..., o[4]:o[5]], pad, w[..., o[6]:o[10]]],
                           axis=-1).astype(BF16)


def kernel(x_prompt, x_sample, cache_k, cache_v, cache_kidx, state_ret, state_conv, state_rglru, page_table,
           c_prompt, c_sample, w_ada, b_ada, g_pre, g_post, w_ffn_gate, w_ffn_up, w_ffn_down,
           w_in_even, w_out_even, w_in_odd, w_out_odd, conv_w, conv_b, w_rg_a, b_rg_a, w_rg_x, b_rg_x, rg_lambda):
    B, L, _ = x_prompt.shape
    Bs, T, _ = x_sample.shape
    n_pages = page_table.shape[1]
    past = n_pages * cache_k.shape[2]
    n_phys = cache_k.shape[1]
    assert cache_k.shape[2] == PAGE and T == SUBLANES

    tp = _Tiling(B, L, 1, 512)
    ts = _Tiling(Bs, T, 64, T)
    tp_odd = _Tiling(B, L, 1, 256)
    ts_odd = _Tiling(Bs, T, 32, T)

    vec3 = lambda a: a.reshape(a.shape[0], 1, a.shape[1])
    p = {
        "g_pre": g_pre.reshape(DEPTH, 3, 1, D), "g_post": g_post.reshape(DEPTH, 3, 1, D),
        "wg": w_ffn_gate.astype(BF16), "wu": w_ffn_up.astype(BF16), "wd": w_ffn_down.astype(BF16),
        "w_in_even": _pack_w_in_even(w_in_even), "w_out_even": w_out_even.astype(BF16),
        "w_in_odd": w_in_odd.astype(BF16), "w_out_odd": w_out_odd.astype(BF16),
        "conv_w": conv_w, "conv_b": vec3(conv_b),
        "w_rg_a": w_rg_a.astype(BF16), "b_rg_a": vec3(b_rg_a),
        "w_rg_x": w_rg_x.astype(BF16), "b_rg_x": vec3(b_rg_x), "rg_lambda": vec3(rg_lambda),
    }
    k_t = jnp.transpose(cache_k, (0, 1, 3, 4, 2)).reshape(cache_k.shape[0], n_phys, LANES, PAGE)
    v_t = jnp.transpose(cache_v, (0, 1, 3, 4, 2)).reshape(cache_v.shape[0], n_phys, LANES, PAGE)
    kidx_t = jnp.transpose(cache_kidx, (0, 1, 3, 2))

    mod_all = _ada(jnp.concatenate([c_prompt, c_sample], axis=0), w_ada, b_ada)
    mod_p = mod_all[:, :B].reshape(DEPTH, B, 1, N_MOD * D)
    mod_s = mod_all[:, B:].reshape(DEPTH, Bs, 1, N_MOD * D)

    rope_a_p, rope_r_p = _rope_tables(jnp.arange(L, dtype=jnp.int32))
    rope_a_s, rope_r_s = _rope_tables(jnp.tile(past + jnp.arange(T, dtype=jnp.int32), ts.g))
    rope_map_p = lambda i, j: (0, j, 0)
    rope_map_s = lambda i, j: (0, 0, 0)

    xp = x_prompt.reshape(B * L, D)
    xs = x_sample.reshape(Bs * T, D)
    ks, vs, kis, rets, convs, hs = ([[], []] for _ in range(6))

    for l in range(DEPTH):
        e = l // 2
        groups = ((0, tp, xp, mod_p[l]), (1, ts, xs, mod_s[l]))
        new_x = []
        for gi, t, x, mod in groups:
            x = _ffn(t, x, mod, 0, l, 0, 0, p)
            if l % 2 == 0:
                if gi == 0:
                    q, kt, v, qi, kw, kit, rq, rk, rv, rg = _even_in(
                        t, True, x, mod, 3, l, e, p, rope_a_p, rope_r_p, rope_map_p)
                    o_a = _dsa_prompt(B, L, q, qi, kw, kt, v)
                    o_r, s_new = _ret_prompt(B, L, rq, rk, rv, rg)
                    ks[0].append(jnp.transpose(kt.reshape(B, A_KV, A_DH, L), (0, 3, 1, 2)))
                    vs[0].append(v.reshape(B, L, A_KV, A_DH))
                    kis[0].append(jnp.transpose(kit, (0, 2, 1)))
                else:
                    q, k, v, qi, kw, rq, rk, rv, rg = _even_in(
                        t, False, x, mod, 3, l, e, p, rope_a_s, rope_r_s, rope_map_s)
                    scores = _dsa_s_scores(Bs, T, e, page_table, qi, kw, kidx_t)
                    bias = _dsa_s_select(Bs * T, T, past, scores)
                    o_a = _dsa_s_attn(Bs, T, e, page_table, q, k, v, bias, k_t, v_t)
                    o_r, s_new = _ret_sample(Bs, T, e, rq, rk, rv, rg, state_ret)
                    ks[1].append(k.reshape(Bs, T, A_KV, A_DH))
                    vs[1].append(v.reshape(Bs, T, A_KV, A_DH))
                    kis[1].append(kw[:, :IDX_DIM].reshape(Bs, T, IDX_DIM))
                rets[gi].append(s_new)
                x = _even_out(t, x, o_a, o_r, mod, 5, l, e, p)
            else:
                if gi == 0:
                    x, tail, hlast = _odd_ffn_prompt(B, L, x, mod, l, e, p)
                    convs[0].append(tail[:, SUBLANES - (CONV_W - 1):])
                    hs[0].append(hlast[:, SUBLANES - 1])
                    new_x.append(x)
                    continue
                else:
                    prev = jnp.pad(state_conv[e], ((0, 0), (SUBLANES - (CONV_W - 1), 0), (0, 0))).reshape(Bs * T, D)
                    h0 = jnp.pad(state_rglru[e][:, None, :], ((0, 0), (0, T - 1), (0, 0))).reshape(Bs * T, D)
                    x, xb, hseq = _odd(ts_odd, False, x, mod, 3, l, e, p, prev=prev, h0=h0)
                    convs[1].append(xb.reshape(Bs, T, D)[:, T - (CONV_W - 1):])
                    hs[1].append(hseq.reshape(Bs, T, D)[:, T - 1])
            x = _ffn(t, x, mod, 6, l, 1, 2, p)
            new_x.append(x)
        xp, xs = new_x

    st = lambda lists, gi: jnp.stack(lists[gi])
    return (xp.reshape(B, L, D), xs.reshape(Bs, T, D),
            st(ks, 0), st(vs, 0), st(kis, 0), st(rets, 0), st(convs, 0), st(hs, 0),
            st(ks, 1), st(vs, 1), st(kis, 1), st(rets, 1), st(convs, 1), st(hs, 1))
```

```python
import functools
import math

import jax
import jax.numpy as jnp
import numpy as np
from jax import lax
from jax.experimental import pallas as pl
from jax.experimental.pallas import tpu as pltpu

F32 = jnp.float32
BF16 = jnp.bfloat16

D = 1024
DEPTH = 4
N_MOD = 9
A_HEADS = 8
A_KV = 2
A_DH = 64
ROT_DIM = 16
ROPE_THETA = 500000.0
IDX_HEADS = 4
IDX_DIM = 64
TOPK_MAX = 256
RET_HEADS = 4
RET_DH = 128
RET_THETA = 10000.0
RG_BLOCKS = 8
RG_BW = 128
CONV_W = 4
RG_C = 8.0
D_FF = 2816
EPS = 1e-6
P_IN = 3200
PAGE = 128

LANES = 128
SUBLANES = 8
VMEM_LIMIT = 56 * 1024 * 1024

NEG_INF = float("-inf")


def _cparams(n_axes, vmem=VMEM_LIMIT):
    return pltpu.CompilerParams(dimension_semantics=("arbitrary",) * n_axes, vmem_limit_bytes=vmem)


def _dot(a, b):
    return jnp.dot(a, b, preferred_element_type=F32)


def _dot_nt(a, b):
    return lax.dot_general(a, b, (((1,), (1,)), ((), ())), preferred_element_type=F32)


def _dot_tn(a, b):
    return lax.dot_general(a, b, (((0,), (0,)), ((), ())), preferred_element_type=F32)


def _sigmoid(x):
    return 1.0 / (1.0 + jnp.exp(-x))


def _silu(x):
    return x * _sigmoid(x)


def _gelu_tanh(x):
    return 0.5 * x * (1.0 + jnp.tanh(math.sqrt(2.0 / math.pi) * (x + 0.044715 * (x * x * x))))


def _rms(x, g=None):
    y = x * lax.rsqrt(jnp.mean(x * x, axis=-1, keepdims=True) + EPS)
    return y if g is None else y * g


def _prenorm(x, g_pre, shift, scale, g, rt):
    h = _rms(x, g_pre)
    if g == 1:
        return h * (1.0 + scale[0]) + shift[0]
    h3 = h.reshape(g, rt, D)
    return (h3 * (1.0 + scale) + shift).reshape(g * rt, D)


def _gated(y, gate, g, rt):
    if g == 1:
        return y * gate[0]
    return (y.reshape(g, rt, D) * gate).reshape(g * rt, D)


class _Tiling:
    def __init__(self, G, R, g, rt):
        assert G % g == 0 and R % rt == 0 and (g == 1 or rt == R)
        self.G, self.R, self.g, self.rt = G, R, g, rt
        self.nr = R // rt
        self.rows = g * rt
        self.grid = (G // g, self.nr)
        self.n_tok = G * R

    def tok(self, width):
        nr = self.nr
        return pl.BlockSpec((self.rows, width), lambda i, j: (i * nr + j, 0))

    def mod(self, c):
        return pl.BlockSpec((self.g, 1, D), lambda i, j: (i, 0, c))

    def sel(self, block, idx):
        return pl.BlockSpec(block, lambda i, j: idx, pipeline_mode=pl.Buffered(1))


def _ada_body(c_ref, w_ref, b_ref, o_ref):
    c = _silu(c_ref[...]).astype(BF16)
    o_ref[0] = _dot(c, w_ref[0].astype(BF16)) + b_ref[0]


def _ada(c_all, w_ada, b_ada):
    n = c_all.shape[0]
    return pl.pallas_call(
        _ada_body,
        out_shape=jax.ShapeDtypeStruct((DEPTH, n, N_MOD * D), F32),
        grid=(DEPTH, N_MOD),
        in_specs=[pl.BlockSpec((n, D), lambda l, j: (0, 0)),
                  pl.BlockSpec((1, D, D), lambda l, j: (l, 0, j)),
                  pl.BlockSpec((1, 1, D), lambda l, j: (l, 0, j))],
        out_specs=pl.BlockSpec((1, n, D), lambda l, j: (l, 0, j)),
        compiler_params=_cparams(2),
        name="ada",
    )(c_all, w_ada, b_ada.reshape(DEPTH, 1, N_MOD * D))


def _ffn_body(x_ref, sh_ref, sc_ref, gt_ref, gpre_ref, gpost_ref, wg_ref, wu_ref, wd_ref, o_ref, *, g, rt):
    x = x_ref[...]
    h = _prenorm(x, gpre_ref[...], sh_ref[...], sc_ref[...], g, rt).astype(BF16)
    a = _dot(h, wg_ref[...])
    u = _dot(h, wu_ref[...])
    act = (_silu(a) * u).astype(BF16)
    f = _dot(act, wd_ref[...])
    y = _rms(f, gpost_ref[...])
    o_ref[...] = x + 0.5 * _gated(y, gt_ref[...], g, rt)


def _ffn(t, x, mod, mcol, l, s, n, p):
    return pl.pallas_call(
        functools.partial(_ffn_body, g=t.g, rt=t.rt),
        out_shape=jax.ShapeDtypeStruct((t.n_tok, D), F32),
        grid=t.grid,
        in_specs=[t.tok(D), t.mod(mcol), t.mod(mcol + 1), t.mod(mcol + 2),
                  t.sel((None, None, 1, D), (l, n, 0, 0)), t.sel((None, None, 1, D), (l, n, 0, 0)),
                  t.sel((None, None, D, D_FF), (l, s, 0, 0)), t.sel((None, None, D, D_FF), (l, s, 0, 0)),
                  t.sel((None, None, D_FF, D), (l, s, 0, 0))],
        out_specs=t.tok(D),
        compiler_params=_cparams(2),
        name="ffn",
    )(x, mod, mod, mod, p["g_pre"], p["g_post"], p["wg"], p["wu"], p["wd"])


def _even_in_body(*refs, g, rt, prompt):
    x_ref, sh_ref, sc_ref, gpre_ref, w_ref, ra_ref, rr_ref = refs[:7]
    q_ref, k_ref, v_ref, qi_ref, kw_ref = refs[7:12]
    if prompt:
        kit_ref, rq_ref, rk_ref, rv_ref, rg_ref = refs[12:]
    else:
        rq_ref, rk_ref, rv_ref, rg_ref = refs[12:]
    h = _prenorm(x_ref[...], gpre_ref[...], sh_ref[...], sc_ref[...], g, rt).astype(BF16)
    proj = _dot(h, w_ref[...])

    def sec(c):
        return proj[:, c * LANES:(c + 1) * LANES]

    def rope_a(t, v):
        return (t * ra_ref[3 * v] + pltpu.roll(t, LANES - ROT_DIM // 2, 1) * ra_ref[3 * v + 1]
                + pltpu.roll(t, ROT_DIM // 2, 1) * ra_ref[3 * v + 2])

    def rope_r(t):
        return t * rr_ref[0] + pltpu.roll(t, RET_DH // 2, 1) * rr_ref[1]

    for c in range(4):
        q_ref[:, c * LANES:(c + 1) * LANES] = rope_a(sec(c), 0) * (A_DH ** -0.5)
    k = rope_a(sec(4), 0)
    v_ref[...] = sec(5)
    for c in range(2):
        qi_ref[:, c * LANES:(c + 1) * LANES] = rope_a(sec(6 + c), 0) * (IDX_DIM ** -0.5)
    kw = rope_a(sec(8), 1)
    kw_ref[...] = kw
    if prompt:
        k_ref[...] = k.T
        kit_ref[...] = kw.T[0:IDX_DIM]
    else:
        k_ref[...] = k
    for c in range(4):
        rq_ref[:, c * LANES:(c + 1) * LANES] = rope_r(sec(9 + c))
        rk_ref[:, c * LANES:(c + 1) * LANES] = rope_r(sec(13 + c) * (RET_DH ** -0.5))
        rv_ref[:, c * LANES:(c + 1) * LANES] = sec(17 + c)
        rg_ref[:, c * LANES:(c + 1) * LANES] = sec(21 + c)


def _even_in(t, prompt, x, mod, mcol, l, e, p, rope_a, rope_r, rope_map):
    tokw = lambda w: (jax.ShapeDtypeStruct((t.n_tok, w), F32), t.tok(w))
    seq_t = lambda w: (jax.ShapeDtypeStruct((t.G, w, t.R), F32),
                       pl.BlockSpec((None, w, t.rt), lambda i, j: (i, 0, j)))
    outs = [tokw(512), seq_t(LANES) if prompt else tokw(LANES), tokw(LANES), tokw(256), tokw(LANES)]
    if prompt:
        outs.append(seq_t(IDX_DIM))
    outs += [tokw(512)] * 4
    return pl.pallas_call(
        functools.partial(_even_in_body, g=t.g, rt=t.rt, prompt=prompt),
        out_shape=[o[0] for o in outs],
        grid=t.grid,
        in_specs=[t.tok(D), t.mod(mcol), t.mod(mcol + 1), t.sel((None, None, 1, D), (l, 1, 0, 0)),
                  t.sel((None, D, P_IN), (e, 0, 0)),
                  pl.BlockSpec((6, t.rows, LANES), rope_map),
                  pl.BlockSpec((2, t.rows, LANES), rope_map)],
        out_specs=[o[1] for o in outs],
        compiler_params=_cparams(2),
        name="even_in_p" if prompt else "even_in_s",
    )(x, mod, mod, p["g_pre"], p["w_in_even"], rope_a, rope_r)


def _sortable(s):
    s = jnp.where(s == 0.0, 0.0, s)
    b = lax.bitcast_convert_type(s, jnp.int32)
    return b ^ ((b >> 31) & 0x7FFFFFFF)


def _topk_bias_t(scores_t, visible_t, k):
    s_len, n_q = scores_t.shape
    key = _sortable(scores_t)
    fold = 8 * SUBLANES
    assert s_len % fold == 0 and s_len % LANES == 0
    i16 = jnp.int16

    def count(mask):
        part = jnp.sum(jnp.where(mask, 1.0, 0.0).reshape(s_len // fold, fold, n_q), axis=0)
        return jnp.sum(part, axis=0, keepdims=True)

    def count16(mask):
        ind = jnp.where(mask, i16(1), i16(0))
        part = ind[0:LANES]
        for r in range(1, s_len // LANES):
            part = part + ind[r * LANES:(r + 1) * LANES]
        return jnp.sum(part.astype(jnp.int32).astype(F32), axis=0, keepdims=True)

    def kth16(vals, kk):
        def step(it, t):
            cand = t + jnp.left_shift(jnp.int32(1), 15 - it)
            return jnp.where(count16(vals >= cand.astype(i16)) >= kk, cand, t)
        return lax.fori_loop(0, 16, step, jnp.full((1, n_q), -(2 ** 15), jnp.int32))

    hi = (key >> 16).astype(i16)
    t_hi = kth16(hi, k)
    t_hi16 = t_hi.astype(i16)
    lo = jnp.where(hi == t_hi16, ((key & 0xFFFF) - 2 ** 15).astype(i16), i16(-(2 ** 15)))
    t_lo = kth16(lo, k - count16(hi > t_hi16))
    thr = jnp.left_shift(t_hi, 16) + (t_lo + 2 ** 15)
    ge = key >= thr
    cnt_ge = count(ge)

    def no_ties():
        return jnp.where(visible_t, jnp.where(ge, 0.0, NEG_INF), NEG_INF)

    def ties():
        need = k - count(key > thr)
        tri = jnp.where(lax.broadcasted_iota(jnp.int32, (LANES, LANES), 0)
                        >= lax.broadcasted_iota(jnp.int32, (LANES, LANES), 1), 1.0, 0.0).astype(BF16)
        off = jnp.zeros((1, n_q), F32)
        pieces = []
        for c in range(s_len // LANES):
            sl = slice(c * LANES, (c + 1) * LANES)
            kc = key[sl, :]
            eq = kc == thr
            pc = _dot(tri, jnp.where(eq, 1.0, 0.0).astype(BF16)) + off
            off = pc[LANES - 1:LANES, :]
            tie_ok = jnp.where(eq, jnp.where(pc <= need, 0.0, NEG_INF), NEG_INF)
            b = jnp.where(kc > thr, 0.0, tie_ok)
            pieces.append(jnp.where(visible_t[sl, :], b, NEG_INF))
        return jnp.concatenate(pieces, axis=0)

    return lax.cond(jnp.max(cnt_ge) > k, ties, no_ties)


def _softmax(logits):
    m = jnp.max(logits, axis=-1, keepdims=True)
    p = jnp.exp(logits - m)
    return p, jnp.sum(p, axis=-1, keepdims=True)


def _dsa_p_body(q_ref, qi_ref, kw_ref, kt_ref, v_ref, o_ref, *, tq, topk):
    s_len = q_ref.shape[0]
    rep = A_HEADS // A_KV
    kt = kt_ref[...].astype(BF16)
    ki = kw_ref[:, 0:IDX_DIM].astype(BF16)
    v = v_ref[...].astype(BF16)
    for i in range(s_len // tq):
        rows = slice(i * tq, (i + 1) * tq)
        s_vis = (i + 1) * tq
        qit = qi_ref[rows, :].T.astype(BF16)
        wit = kw_ref[rows, :].T
        sc = jnp.zeros((s_vis, tq), F32)
        for h in range(IDX_HEADS):
            raw = _dot(ki[:s_vis], qit[h * IDX_DIM:(h + 1) * IDX_DIM, :])
            sc = sc + jnp.maximum(raw, 0.0) * wit[IDX_DIM + h:IDX_DIM + h + 1, :]
        kpos = lax.broadcasted_iota(jnp.int32, (s_vis, 1), 0)
        qpos = i * tq + lax.broadcasted_iota(jnp.int32, (1, tq), 1)
        visible = kpos <= qpos
        bias = _topk_bias_t(jnp.where(visible, sc, NEG_INF), visible, topk).T
        q = q_ref[rows, :].astype(BF16)
        for hp in range(A_HEADS // 2):
            outs = []
            for h in (2 * hp, 2 * hp + 1):
                gsl = slice((h // rep) * A_DH, (h // rep + 1) * A_DH)
                p, l = _softmax(_dot(q[:, h * A_DH:(h + 1) * A_DH], kt[gsl, :s_vis]) + bias)
                outs.append(_dot(p.astype(BF16), v[:s_vis, gsl]) / l)
            o_ref[rows, hp * LANES:(hp + 1) * LANES] = jnp.concatenate(outs, axis=-1)


def _dsa_prompt(B, L, q, qi, kw, kt, v, tq=256):
    tok = lambda w: pl.BlockSpec((L, w), lambda b: (b, 0))
    seq_t = lambda w: pl.BlockSpec((None, w, L), lambda b: (b, 0, 0))
    return pl.pallas_call(
        functools.partial(_dsa_p_body, tq=tq, topk=min(TOPK_MAX, L // 4)),
        out_shape=jax.ShapeDtypeStruct((B * L, A_HEADS * A_DH), F32),
        grid=(B,),
        in_specs=[tok(512), tok(256), tok(LANES), seq_t(LANES), tok(LANES)],
        out_specs=tok(512),
        compiler_params=_cparams(1),
        name="dsa_prompt",
    )(q, qi, kw, kt, v)


SEQ_PER_STEP = 4


def _page_gather(pt_ref, pool_ref, buf_ref, sem_ref, e, n_pages):
    b = pl.program_id(0)

    def copies(step, slot):
        return [pltpu.make_async_copy(pool_ref.at[e, pt_ref[step * SEQ_PER_STEP + s, p]],
                                      buf_ref.at[slot, s, :, pl.ds(p * PAGE, PAGE)],
                                      sem_ref.at[slot])
                for s in range(SEQ_PER_STEP) for p in range(n_pages)]

    @pl.when(b == 0)
    def _():
        for c in copies(0, 0):
            c.start()

    @pl.when(b + 1 < pl.num_programs(0))
    def _():
        for c in copies(b + 1, (b + 1) % 2):
            c.start()

    slot = b % 2
    for c in copies(b, slot):
        c.wait()
    return slot


def _dsa_s_scores_body(pt_ref, qi_ref, kw_ref, pool_ref, o_ref, buf_ref, sem_ref, *, e, n_pages, t):
    slot = _page_gather(pt_ref, pool_ref, buf_ref, sem_ref, e, n_pages)
    past = n_pages * PAGE
    lane = lax.broadcasted_iota(jnp.int32, (t, PAGE), 1)
    row = lax.broadcasted_iota(jnp.int32, (t, PAGE), 0)
    for s in range(SEQ_PER_STEP):
        rows = slice(s * t, (s + 1) * t)
        qi = qi_ref[rows, :]
        qs = jnp.concatenate([qi[:, h * IDX_DIM:(h + 1) * IDX_DIM] for h in range(IDX_HEADS)], axis=0).astype(BF16)
        kw = kw_ref[rows, :]
        wi = kw[:, IDX_DIM:LANES]
        ki_new = jnp.concatenate([kw[:, 0:IDX_DIM], jnp.zeros((PAGE - t, IDX_DIM), F32)], axis=0).astype(BF16)
        kit = buf_ref[slot, s].astype(BF16)

        def weigh(raw):
            raw = jnp.maximum(raw, 0.0)
            sc = raw[0:t] * wi[:, 0:1]
            for h in range(1, IDX_HEADS):
                sc = sc + raw[h * t:(h + 1) * t] * wi[:, h:h + 1]
            return sc

        o_ref[rows, :past] = weigh(_dot(qs, kit))
        o_ref[rows, past:] = jnp.where(lane <= row, weigh(_dot_nt(qs, ki_new)), NEG_INF)


def _dsa_s_scores(Bs, T, e, page_table, qi, kw, kidx_t):
    n_pages = page_table.shape[1]
    tok = lambda w: pl.BlockSpec((SEQ_PER_STEP * T, w), lambda b, pt: (b, 0))
    s_pad = (n_pages + 1) * PAGE
    return pl.pallas_call(
        functools.partial(_dsa_s_scores_body, e=e, n_pages=n_pages, t=T),
        out_shape=jax.ShapeDtypeStruct((Bs * T, s_pad), F32),
        grid_spec=pltpu.PrefetchScalarGridSpec(
            num_scalar_prefetch=1, grid=(Bs // SEQ_PER_STEP,),
            in_specs=[tok(256), tok(LANES), pl.BlockSpec(memory_space=pl.ANY)],
            out_specs=tok(s_pad),
            scratch_shapes=[pltpu.VMEM((2, SEQ_PER_STEP, IDX_DIM, n_pages * PAGE), F32),
                            pltpu.SemaphoreType.DMA((2,))]),
        compiler_params=_cparams(1),
        name="dsa_s_scores",
    )(page_table, qi, kw, kidx_t)


def _dsa_s_select_body(sc_ref, o_ref, *, t, past, topk):
    rows, s_pad = sc_ref.shape
    tpos = jnp.bitwise_and(lax.broadcasted_iota(jnp.int32, (1, rows), 1), t - 1)
    kpos = lax.broadcasted_iota(jnp.int32, (s_pad, 1), 0)
    visible = kpos <= past + tpos
    o_ref[...] = _topk_bias_t(sc_ref[...].T, visible, topk).T


def _dsa_s_select(n_rows, T, past, scores, rows=256):
    s_pad = scores.shape[1]
    spec = pl.BlockSpec((rows, s_pad), lambda i: (i, 0))
    return pl.pallas_call(
        functools.partial(_dsa_s_select_body, t=T, past=past, topk=min(TOPK_MAX, (past + T) // 4)),
        out_shape=jax.ShapeDtypeStruct((n_rows, s_pad), F32),
        grid=(n_rows // rows,),
        in_specs=[spec], out_specs=spec,
        compiler_params=_cparams(1),
        name="dsa_s_select",
    )(scores)


def _dsa_s_attn_body(pt_ref, q_ref, kn_ref, vn_ref, b_ref, kpool_ref, vpool_ref, o_ref,
                     kbuf_ref, vbuf_ref, ksem_ref, vsem_ref, *, e, n_pages, t):
    slot = _page_gather(pt_ref, kpool_ref, kbuf_ref, ksem_ref, e, n_pages)
    _page_gather(pt_ref, vpool_ref, vbuf_ref, vsem_ref, e, n_pages)
    past = n_pages * PAGE
    rep = A_HEADS // A_KV
    pad = jnp.zeros((PAGE - t, LANES), F32)
    for s in range(SEQ_PER_STEP):
        rows = slice(s * t, (s + 1) * t)
        kt = kbuf_ref[slot, s].astype(BF16)
        vt = vbuf_ref[slot, s].astype(BF16)
        k_new = jnp.concatenate([kn_ref[rows, :], pad], axis=0).astype(BF16)
        v_new = jnp.concatenate([vn_ref[rows, :], pad], axis=0).astype(BF16)
        q = q_ref[rows, :].astype(BF16)
        bias = jnp.concatenate([b_ref[rows, :]] * rep, axis=0)
        outs = []
        for gi in range(A_KV):
            gsl = slice(gi * A_DH, (gi + 1) * A_DH)
            qg = jnp.concatenate([q[:, (gi * rep + r) * A_DH:(gi * rep + r + 1) * A_DH] for r in range(rep)], axis=0)
            logits = jnp.concatenate([_dot(qg, kt[gsl, :]), _dot_nt(qg, k_new[:, gsl])], axis=-1) + bias
            p, l = _softmax(logits)
            p = p.astype(BF16)
            og = (_dot_nt(p[:, :past], vt[gsl, :]) + _dot(p[:, past:], v_new[:, gsl])) / l
            outs += [og[r * t:(r + 1) * t] for r in range(rep)]
        for hp in range(A_HEADS // 2):
            o_ref[rows, hp * LANES:(hp + 1) * LANES] = jnp.concatenate(outs[2 * hp:2 * hp + 2], axis=-1)


def _dsa_s_attn(Bs, T, e, page_table, q, k_new, v_new, bias, k_t, v_t):
    n_pages = page_table.shape[1]
    tok = lambda w: pl.BlockSpec((SEQ_PER_STEP * T, w), lambda b, pt: (b, 0))
    pool = pl.BlockSpec(memory_space=pl.ANY)
    buf = pltpu.VMEM((2, SEQ_PER_STEP, LANES, n_pages * PAGE), F32)
    return pl.pallas_call(
        functools.partial(_dsa_s_attn_body, e=e, n_pages=n_pages, t=T),
        out_shape=jax.ShapeDtypeStruct((Bs * T, A_HEADS * A_DH), F32),
        grid_spec=pltpu.PrefetchScalarGridSpec(
            num_scalar_prefetch=1, grid=(Bs // SEQ_PER_STEP,),
            in_specs=[tok(512), tok(LANES), tok(LANES), tok(bias.shape[1]), pool, pool],
            out_specs=tok(512),
            scratch_shapes=[buf, buf, pltpu.SemaphoreType.DMA((2,)), pltpu.SemaphoreType.DMA((2,))]),
        compiler_params=_cparams(1),
        name="dsa_s_attn",
    )(page_table, q, k_new, v_new, bias, k_t, v_t)


def _ret_log_gamma(h):
    return math.log1p(-(2.0 ** (-5.0 - h)))


def _ret_chunk(q, k, v, rg, s, h, c):
    lg = _ret_log_gamma(h)
    ri = lax.broadcasted_iota(jnp.int32, (c, c), 0)
    ci = lax.broadcasted_iota(jnp.int32, (c, c), 1)
    diff = (ri - ci).astype(F32)
    decay = jnp.where(diff >= 0, jnp.exp(jnp.maximum(diff, 0.0) * lg), 0.0)
    idx = lax.broadcasted_iota(jnp.int32, (c, 1), 0).astype(F32)
    q_dec = jnp.exp((idx + 1.0) * lg)
    k_dec = jnp.exp((c - 1.0 - idx) * lg)
    c_dec = math.exp(c * lg)
    qb = q.astype(BF16)
    att = _dot_nt(qb, k.astype(BF16)) * decay
    inner = _dot(att.astype(BF16), v.astype(BF16))
    cross = _dot(qb, s.astype(BF16)) * q_dec
    s_new = s * c_dec + _dot_tn((k * k_dec).astype(BF16), v.astype(BF16))
    return _rms(inner + cross) * _silu(rg), s_new


def _ret_s_body(rq_ref, rk_ref, rv_ref, rg_ref, s0_ref, o_ref, so_ref, *, t):
    for s in range(SEQ_PER_STEP):
        rows = slice(s * t, (s + 1) * t)
        for h in range(RET_HEADS):
            sl = slice(h * RET_DH, (h + 1) * RET_DH)
            o, s_new = _ret_chunk(rq_ref[rows, sl], rk_ref[rows, sl], rv_ref[rows, sl], rg_ref[rows, sl],
                                  s0_ref[s, h], h, t)
            o_ref[rows, sl] = o
            so_ref[s, h] = s_new


def _ret_sample(Bs, T, e, rq, rk, rv, rg, state_ret):
    tok = pl.BlockSpec((SEQ_PER_STEP * T, 512), lambda b: (b, 0))
    return pl.pallas_call(
        functools.partial(_ret_s_body, t=T),
        out_shape=[jax.ShapeDtypeStruct((Bs * T, 512), F32),
                   jax.ShapeDtypeStruct((Bs, RET_HEADS, RET_DH, RET_DH), F32)],
        grid=(Bs // SEQ_PER_STEP,),
        in_specs=[tok, tok, tok, tok,
                  pl.BlockSpec((None, SEQ_PER_STEP, RET_HEADS, RET_DH, RET_DH), lambda b: (e, b, 0, 0, 0))],
        out_specs=[tok, pl.BlockSpec((SEQ_PER_STEP, RET_HEADS, RET_DH, RET_DH), lambda b: (b, 0, 0, 0))],
        compiler_params=_cparams(1),
        name="ret_sample",
    )(rq, rk, rv, rg, state_ret)


def _even_out_body(x_ref, oa_ref, or_ref, gt_ref, gpost_ref, wa_ref, wr_ref, o_ref, *, g, rt):
    y = _dot(oa_ref[...].astype(BF16), wa_ref[...]) + _dot(or_ref[...].astype(BF16), wr_ref[...])
    o_ref[...] = x_ref[...] + _gated(_rms(y, gpost_ref[...]), gt_ref[...], g, rt)


def _even_out(t, x, o_a, o_r, mod, mcol, l, e, p):
    half = A_HEADS * A_DH
    return pl.pallas_call(
        functools.partial(_even_out_body, g=t.g, rt=t.rt),
        out_shape=jax.ShapeDtypeStruct((t.n_tok, D), F32),
        grid=t.grid,
        in_specs=[t.tok(D), t.tok(half), t.tok(half), t.mod(mcol), t.sel((None, None, 1, D), (l, 1, 0, 0)),
                  t.sel((None, half, D), (e, 0, 0)), t.sel((None, half, D), (e, 1, 0))],
        out_specs=t.tok(D),
        compiler_params=_cparams(2),
        name="even_out",
    )(x, o_a, o_r, mod, p["g_post"], p["w_out_even"], p["w_out_even"])


FF_HALF = D_FF // 2
RET_CHUNK = 256


def _ffn_staged(x, mod_refs, gpre_ref, gpost_ref, wg_ref, wu_ref, wd_ref, between=()):
    sh_ref, sc_ref, gt_ref = mod_refs
    hf = _prenorm(x, gpre_ref[...], sh_ref[...], sc_ref[...], 1, x.shape[0]).astype(BF16)
    act = []
    for c in range(2):
        cols = slice(c * FF_HALF, (c + 1) * FF_HALF)
        a_c, u_c = _dot(hf, wg_ref[:, cols]), _dot(hf, wu_ref[:, cols])
        if c < len(between):
            between[c]()
        act.append((_silu(a_c) * u_c).astype(BF16))
    f = _dot(act[0], wd_ref[0:FF_HALF, :]) + _dot(act[1], wd_ref[FF_HALF:D_FF, :])
    return x + 0.5 * (_rms(f, gpost_ref[...]) * gt_ref[...][0])


def _even_tail_body(rq_ref, rk_ref, rv_ref, rg_ref, x_ref, oa_ref, gt_ref, sh2_ref, sc2_ref, gt2_ref,
                    gpost_ref, gpre2_ref, gpost2_ref, wa_ref, wr_ref, wg_ref, wu_ref, wd_ref,
                    o_ref, so_ref, or_ref, s_ref, *, rows, nt, n_tiles):
    t = pl.program_id(0)
    j = lax.rem(t, nt)
    slot = lax.rem(t, 2)

    @pl.when(t == 0)
    def _():
        or_ref[...] = jnp.zeros_like(or_ref)

    @pl.when(j == 0)
    def _():
        s_ref[...] = jnp.zeros_like(s_ref)

    def retention(c):
        def run():
            rs = slice(c * RET_CHUNK, (c + 1) * RET_CHUNK)
            for h in range(RET_HEADS):
                sl = slice(h * RET_DH, (h + 1) * RET_DH)
                o, s_new = _ret_chunk(rq_ref[rs, sl], rk_ref[rs, sl], rv_ref[rs, sl], rg_ref[rs, sl],
                                      s_ref[h], h, RET_CHUNK)
                or_ref[slot, rs, sl] = o
                s_ref[h] = s_new
        return run

    o_r = or_ref[1 - slot]
    y = _dot(oa_ref[...].astype(BF16), wa_ref[...]) + _dot(o_r.astype(BF16), wr_ref[...])
    x1 = x_ref[...] + _rms(y, gpost_ref[...]) * gt_ref[...][0]
    o_ref[...] = _ffn_staged(x1, (sh2_ref, sc2_ref, gt2_ref), gpre2_ref, gpost2_ref, wg_ref, wu_ref, wd_ref,
                             between=[retention(c) for c in range(rows // RET_CHUNK)])

    @pl.when(t < n_tiles)
    def _():
        so_ref[0] = s_ref[...]


def _even_tail_prompt(B, L, x, o_a, rq, rk, rv, rg, mod, l, e, p, rows=512):
    assert rows == 2 * RET_CHUNK
    nt = L // rows
    n_tiles = B * nt
    half = A_HEADS * A_DH
    cur = lambda t: jnp.minimum(t, n_tiles - 1)
    prv = lambda t: jnp.maximum(t - 1, 0)
    sel = lambda block, idx: pl.BlockSpec(block, lambda t: idx, pipeline_mode=pl.Buffered(1))
    modc = lambda c: pl.BlockSpec((1, 1, D), lambda t: (prv(t) // nt, 0, c))
    gain = lambda n: sel((None, None, 1, D), (l, n, 0, 0))
    ret_in = pl.BlockSpec((rows, RET_HEADS * RET_DH), lambda t: (cur(t), 0))
    return pl.pallas_call(
        functools.partial(_even_tail_body, rows=rows, nt=nt, n_tiles=n_tiles),
        out_shape=[jax.ShapeDtypeStruct((B * L, D), F32),
                   jax.ShapeDtypeStruct((B, RET_HEADS, RET_DH, RET_DH), F32)],
        grid=(n_tiles + 1,),
        in_specs=[ret_in, ret_in, ret_in, ret_in,
                  pl.BlockSpec((rows, D), lambda t: (prv(t), 0)), pl.BlockSpec((rows, half), lambda t: (prv(t), 0)),
                  modc(5), modc(6), modc(7), modc(8), gain(1), gain(2), gain(2),
                  sel((None, half, D), (e, 0, 0)), sel((None, half, D), (e, 1, 0)),
                  sel((None, None, D, D_FF), (l, 1, 0, 0)), sel((None, None, D, D_FF), (l, 1, 0, 0)),
                  sel((None, None, D_FF, D), (l, 1, 0, 0))],
        out_specs=[pl.BlockSpec((rows, D), lambda t: (prv(t), 0)),
                   pl.BlockSpec((1, RET_HEADS, RET_DH, RET_DH), lambda t: (cur(t) // nt, 0, 0, 0))],
        scratch_shapes=[pltpu.VMEM((2, rows, RET_HEADS * RET_DH), F32), pltpu.VMEM((RET_HEADS, RET_DH, RET_DH), F32)],
        compiler_params=_cparams(1),
        name="even_tail_prompt",
    )(rq, rk, rv, rg, x, o_a, mod, mod, mod, mod, p["g_post"], p["g_pre"], p["g_post"],
      p["w_out_even"], p["w_out_even"], p["wg"], p["wu"], p["wd"])


def _softplus(x):
    return jnp.maximum(x, 0.0) + jnp.log1p(jnp.exp(-jnp.abs(x)))


def _scan_sublanes(a, b):
    sub = lax.broadcasted_iota(jnp.int32, (1, SUBLANES, a.shape[-1]), 1)
    d = 1
    while d < SUBLANES:
        keep = sub >= d
        a_sh = jnp.where(keep, pltpu.roll(a, d, 1), 1.0)
        b_sh = jnp.where(keep, pltpu.roll(b, d, 1), 0.0)
        b = b + a * b_sh
        a = a * a_sh
        d *= 2
    return a, b


def _rglru_gates(conv, wa_ref, wx_ref):
    convb = conv.astype(BF16)
    ra = jnp.concatenate([_dot(convb[:, n * RG_BW:(n + 1) * RG_BW], wa_ref[n]) for n in range(RG_BLOCKS)], axis=-1)
    rx = jnp.concatenate([_dot(convb[:, n * RG_BW:(n + 1) * RG_BW], wx_ref[n]) for n in range(RG_BLOCKS)], axis=-1)
    return ra, rx


def _odd_s_body(x_ref, sh_ref, sc_ref, gt_ref, gpre_ref, gpost_ref, win_ref, cw_ref, cb_ref,
                wa_ref, ba_ref, wx_ref, bx_ref, lam_ref, wout_ref, prev_ref, h0_ref,
                o_ref, xb_out_ref, hs_out_ref, *, g, rt):
    rows = g * rt
    x = x_ref[...]
    h = _prenorm(x, gpre_ref[...], sh_ref[...], sc_ref[...], g, rt).astype(BF16)
    proj = _dot(h, win_ref[...])
    gate_br = proj[:, :D]
    xb = proj[:, D:]
    pos = jnp.bitwise_and(lax.broadcasted_iota(jnp.int32, (rows, 1), 0), rt - 1)
    cw = cw_ref[...]
    prev = prev_ref[...]
    conv = xb * cw[CONV_W - 1:CONV_W] + cb_ref[...]
    for jj in range(CONV_W - 1):
        d = CONV_W - 1 - jj
        tap = jnp.where(pos >= d, pltpu.roll(xb, d, 0), pltpu.roll(prev, rows - (SUBLANES - d), 0))
        conv = conv + tap * cw[jj:jj + 1]
    xb_out_ref[...] = xb
    ra, rx = _rglru_gates(conv, wa_ref, wx_ref)
    r = _sigmoid(ra + ba_ref[...])
    ig = _sigmoid(rx + bx_ref[...])
    a = jnp.exp((-RG_C) * r * _softplus(-lam_ref[...]))
    b = jnp.sqrt(1.0 - a * a) * (ig * conv)
    b = b + a * h0_ref[...]
    n_grp = rows // SUBLANES
    hs = _scan_sublanes(a.reshape(n_grp, SUBLANES, D), b.reshape(n_grp, SUBLANES, D))[1].reshape(rows, D)
    hs_out_ref[...] = hs
    y = (hs * _gelu_tanh(gate_br)).astype(BF16)
    out = _dot(y, wout_ref[...])
    o_ref[...] = x + _gated(_rms(out, gpost_ref[...]), gt_ref[...], g, rt)


def _odd_sample(t, x, mod, mcol, l, e, p, prev, h0):
    vec = t.sel((None, 1, D), (e, 0, 0))
    rgw = t.sel((None, RG_BLOCKS, RG_BW, RG_BW), (e, 0, 0, 0))
    return pl.pallas_call(
        functools.partial(_odd_s_body, g=t.g, rt=t.rt),
        out_shape=[jax.ShapeDtypeStruct((t.n_tok, D), F32)] * 3,
        grid=t.grid,
        in_specs=[t.tok(D), t.mod(mcol), t.mod(mcol + 1), t.mod(mcol + 2),
                  t.sel((None, None, 1, D), (l, 1, 0, 0)), t.sel((None, None, 1, D), (l, 1, 0, 0)),
                  t.sel((None, D, 2 * D), (e, 0, 0)), t.sel((None, CONV_W, D), (e, 0, 0)), vec,
                  rgw, vec, rgw, vec, vec, t.sel((None, D, D), (e, 0, 0)), t.tok(D), t.tok(D)],
        out_specs=[t.tok(D)] * 3,
        compiler_params=_cparams(2),
        name="odd_sample",
    )(x, mod, mod, mod, p["g_pre"], p["g_post"], p["w_in_odd"], p["conv_w"], p["conv_b"],
      p["w_rg_a"], p["b_rg_a"], p["w_rg_x"], p["b_rg_x"], p["rg_lambda"], p["w_out_odd"], prev, h0)


def _odd_ffn_body(x_ref, sh_ref, sc_ref, gt_ref, sh2_ref, sc2_ref, gt2_ref,
                  gpre_ref, gpost_ref, gpre2_ref, gpost2_ref, win_ref, cw_ref, cb_ref,
                  wa_ref, ba_ref, wx_ref, bx_ref, lam_ref, wout_ref, wg_ref, wu_ref, wd_ref,
                  o_ref, tail_out_ref, h_out_ref, x1_ref, tail_ref, hcar_ref, *, rows, nt, n_tiles):
    t = pl.program_id(0)
    j = lax.rem(t, nt)
    slot = lax.rem(t, 2)
    n_grp = rows // SUBLANES

    @pl.when(t == 0)
    def _():
        x1_ref[...] = jnp.zeros_like(x1_ref)

    @pl.when(j == 0)
    def _():
        tail_ref[...] = jnp.zeros_like(tail_ref)
        hcar_ref[...] = jnp.zeros_like(hcar_ref)

    x = x_ref[...]
    h = _prenorm(x, gpre_ref[...], sh_ref[...], sc_ref[...], 1, rows).astype(BF16)
    proj = _dot(h, win_ref[...])
    gate_br = proj[:, :D]
    xb = proj[:, D:]
    row = lax.broadcasted_iota(jnp.int32, (rows, 1), 0)
    cw = cw_ref[...]
    xext = jnp.concatenate([tail_ref[...], xb], axis=0)
    conv = xb * cw[CONV_W - 1:CONV_W] + cb_ref[...]
    for jj in range(CONV_W - 1):
        conv = conv + pltpu.roll(xext, CONV_W - 1 - jj, 0)[SUBLANES:] * cw[jj:jj + 1]
    tail_ref[...] = xb[rows - SUBLANES:]
    ra, rx = _rglru_gates(conv, wa_ref, wx_ref)

    x1p = x1_ref[1 - slot]
    hf = _prenorm(x1p, gpre2_ref[...], sh2_ref[...], sc2_ref[...], 1, rows).astype(BF16)
    au = [(_dot(hf, wg_ref[:, c * FF_HALF:(c + 1) * FF_HALF]), _dot(hf, wu_ref[:, c * FF_HALF:(c + 1) * FF_HALF]))
          for c in range(2)]

    r = _sigmoid(ra + ba_ref[...])
    ig = _sigmoid(rx + bx_ref[...])
    log_a = (-RG_C) * r * _softplus(-lam_ref[...])
    a = jnp.exp(log_a)
    mult = jnp.sqrt(1.0 - a * a)
    mult = jnp.where(jnp.logical_and(j == 0, row == 0), 1.0, mult)
    b = mult * (ig * conv)
    a, b = _scan_sublanes(a.reshape(n_grp, SUBLANES, D), b.reshape(n_grp, SUBLANES, D))
    carry = hcar_ref[0:1]
    parts = []
    for g8 in range(n_grp):
        part = b[g8] + a[g8] * carry
        carry = part[SUBLANES - 1:SUBLANES]
        parts.append(part)
    hs = jnp.concatenate(parts, axis=0)
    hcar_ref[0:1] = carry
    y = (hs * _gelu_tanh(gate_br)).astype(BF16)

    act = [(_silu(a_c) * u_c).astype(BF16) for a_c, u_c in au]
    x1 = x + _rms(_dot(y, wout_ref[...]), gpost_ref[...]) * gt_ref[...][0]
    x1_ref[slot] = x1
    f = _dot(act[0], wd_ref[0:FF_HALF, :]) + _dot(act[1], wd_ref[FF_HALF:D_FF, :])
    o_ref[...] = x1p + 0.5 * (_rms(f, gpost2_ref[...]) * gt2_ref[...][0])

    @pl.when(t < n_tiles)
    def _():
        tail_out_ref[0] = xb[rows - SUBLANES:]
        h_out_ref[0] = parts[-1]


def _odd_ffn_prompt(B, L, x, mod, l, e, p, rows=512):
    nt = L // rows
    n_tiles = B * nt
    cur = lambda t: jnp.minimum(t, n_tiles - 1)
    prv = lambda t: jnp.maximum(t - 1, 0)
    sel = lambda block, idx: pl.BlockSpec(block, lambda t: idx, pipeline_mode=pl.Buffered(1))
    modc = lambda c, which: pl.BlockSpec((1, 1, D), lambda t: (which(t) // nt, 0, c))
    gain = lambda n: sel((None, None, 1, D), (l, n, 0, 0))
    vec = sel((None, 1, D), (e, 0, 0))
    rgw = sel((None, RG_BLOCKS, RG_BW, RG_BW), (e, 0, 0, 0))
    last = pl.BlockSpec((1, SUBLANES, D), lambda t: (cur(t) // nt, 0, 0))
    return pl.pallas_call(
        functools.partial(_odd_ffn_body, rows=rows, nt=nt, n_tiles=n_tiles),
        out_shape=[jax.ShapeDtypeStruct((B * L, D), F32),
                   jax.ShapeDtypeStruct((B, SUBLANES, D), F32),
                   jax.ShapeDtypeStruct((B, SUBLANES, D), F32)],
        grid=(n_tiles + 1,),
        in_specs=[pl.BlockSpec((rows, D), lambda t: (cur(t), 0)),
                  modc(3, cur), modc(4, cur), modc(5, cur), modc(6, prv), modc(7, prv), modc(8, prv),
                  gain(1), gain(1), gain(2), gain(2),
                  sel((None, D, 2 * D), (e, 0, 0)), sel((None, CONV_W, D), (e, 0, 0)), vec,
                  rgw, vec, rgw, vec, vec, sel((None, D, D), (e, 0, 0)),
                  sel((None, None, D, D_FF), (l, 1, 0, 0)), sel((None, None, D, D_FF), (l, 1, 0, 0)),
                  sel((None, None, D_FF, D), (l, 1, 0, 0))],
        out_specs=[pl.BlockSpec((rows, D), lambda t: (prv(t), 0)), last, last],
        scratch_shapes=[pltpu.VMEM((2, rows, D), F32), pltpu.VMEM((SUBLANES, D), F32), pltpu.VMEM((SUBLANES, D), F32)],
        compiler_params=_cparams(1),
        name="odd_ffn_prompt",
    )(x, mod, mod, mod, mod, mod, mod, p["g_pre"], p["g_post"], p["g_pre"], p["g_post"],
      p["w_in_odd"], p["conv_w"], p["conv_b"], p["w_rg_a"], p["b_rg_a"], p["w_rg_x"], p["b_rg_x"], p["rg_lambda"],
      p["w_out_odd"], p["wg"], p["wu"], p["wd"])


def _rope_tables(pos):
    posf = pos.astype(F32)[:, None]
    inv_a = jnp.power(jnp.float32(ROPE_THETA), -jnp.arange(0, ROT_DIM, 2, dtype=F32) / ROT_DIM)
    ang = posf * inv_a[None, :]
    cos, sin = jnp.cos(ang), jnp.sin(ang)
    n = pos.shape[0]
    half = ROT_DIM // 2
    rest = A_DH - ROT_DIM
    one, zero = jnp.ones((n, rest), F32), jnp.zeros((n, rest), F32)
    zh = jnp.zeros((n, half), F32)
    cos_h = jnp.concatenate([cos, cos, one], axis=1)
    s1_h = jnp.concatenate([-sin, zh, zero], axis=1)
    s2_h = jnp.concatenate([zh, sin, zero], axis=1)
    both = lambda a: jnp.concatenate([a, a], axis=1)
    wi_scale = jnp.full((n, A_DH), IDX_HEADS ** -0.5, F32)
    z64 = jnp.zeros((n, A_DH), F32)
    rope_a = jnp.stack([both(cos_h), both(s1_h), both(s2_h),
                        jnp.concatenate([cos_h, wi_scale], axis=1),
                        jnp.concatenate([s1_h, z64], axis=1),
                        jnp.concatenate([s2_h, z64], axis=1)])
    inv_r = jnp.power(jnp.float32(RET_THETA), -jnp.linspace(0.0, 1.0, RET_DH // 2, dtype=F32))
    ang_r = posf * inv_r[None, :]
    cr, sr = jnp.cos(ang_r), jnp.sin(ang_r)
    rope_r = jnp.stack([jnp.concatenate([cr, cr], axis=1), jnp.concatenate([-sr, sr], axis=1)])
    return rope_a, rope_r


def _pack_w_in_even(w):
    o = np.cumsum((0, 512, 128, 128, 256, IDX_HEADS, IDX_DIM, 512, 512, 512, 512))
    pad = jnp.zeros(w.shape[:2] + (LANES - IDX_DIM - IDX_HEADS,), w.dtype)
    return jnp.concatenate([w[..., o[0]:o[4]], w[..., o[5]:o[6]], w[..., o[4]:o[5]], pad, w[..., o[6]:o[10]]],
                           axis=-1).astype(BF16)


def kernel(x_prompt, x_sample, cache_k, cache_v, cache_kidx, state_ret, state_conv, state_rglru, page_table,
           c_prompt, c_sample, w_ada, b_ada, g_pre, g_post, w_ffn_gate, w_ffn_up, w_ffn_down,
           w_in_even, w_out_even, w_in_odd, w_out_odd, conv_w, conv_b, w_rg_a, b_rg_a, w_rg_x, b_rg_x, rg_lambda):
    B, L, _ = x_prompt.shape
    Bs, T, _ = x_sample.shape
    n_pages = page_table.shape[1]
    past = n_pages * cache_k.shape[2]
    n_phys = cache_k.shape[1]
    assert cache_k.shape[2] == PAGE and T == SUBLANES

    tp = _Tiling(B, L, 1, 512)
    ts = _Tiling(Bs, T, 64, T)
    ts_odd = _Tiling(Bs, T, 32, T)

    vec3 = lambda a: a.reshape(a.shape[0], 1, a.shape[1])
    p = {
        "g_pre": g_pre.reshape(DEPTH, 3, 1, D), "g_post": g_post.reshape(DEPTH, 3, 1, D),
        "wg": w_ffn_gate.astype(BF16), "wu": w_ffn_up.astype(BF16), "wd": w_ffn_down.astype(BF16),
        "w_in_even": _pack_w_in_even(w_in_even), "w_out_even": w_out_even.astype(BF16),
        "w_in_odd": w_in_odd.astype(BF16), "w_out_odd": w_out_odd.astype(BF16),
        "conv_w": conv_w, "conv_b": vec3(conv_b),
        "w_rg_a": w_rg_a.astype(BF16), "b_rg_a": vec3(b_rg_a),
        "w_rg_x": w_rg_x.astype(BF16), "b_rg_x": vec3(b_rg_x), "rg_lambda": vec3(rg_lambda),
    }
    k_t = jnp.transpose(cache_k, (0, 1, 3, 4, 2)).reshape(cache_k.shape[0], n_phys, LANES, PAGE)
    v_t = jnp.transpose(cache_v, (0, 1, 3, 4, 2)).reshape(cache_v.shape[0], n_phys, LANES, PAGE)
    kidx_t = jnp.transpose(cache_kidx, (0, 1, 3, 2))

    mod_all = _ada(jnp.concatenate([c_prompt, c_sample], axis=0), w_ada, b_ada)
    mod_p = mod_all[:, :B].reshape(DEPTH, B, 1, N_MOD * D)
    mod_s = mod_all[:, B:].reshape(DEPTH, Bs, 1, N_MOD * D)

    rope_a_p, rope_r_p = _rope_tables(jnp.arange(L, dtype=jnp.int32))
    rope_a_s, rope_r_s = _rope_tables(jnp.tile(past + jnp.arange(T, dtype=jnp.int32), ts.g))
    rope_map_p = lambda i, j: (0, j, 0)
    rope_map_s = lambda i, j: (0, 0, 0)

    xp = x_prompt.reshape(B * L, D)
    xs = x_sample.reshape(Bs * T, D)
    ks, vs, kis, rets, convs, hs = ([[], []] for _ in range(6))

    for l in range(DEPTH):
        e = l // 2
        groups = ((0, tp, xp, mod_p[l]), (1, ts, xs, mod_s[l]))
        new_x = []
        for gi, t, x, mod in groups:
            x = _ffn(t, x, mod, 0, l, 0, 0, p)
            if l % 2 == 0:
                if gi == 0:
                    q, kt, v, qi, kw, kit, rq, rk, rv, rg = _even_in(
                        t, True, x, mod, 3, l, e, p, rope_a_p, rope_r_p, rope_map_p)
                    o_a = _dsa_prompt(B, L, q, qi, kw, kt, v)
                    ks[0].append(jnp.transpose(kt.reshape(B, A_KV, A_DH, L), (0, 3, 1, 2)))
                    vs[0].append(v.reshape(B, L, A_KV, A_DH))
                    kis[0].append(jnp.transpose(kit, (0, 2, 1)))
                    x, s_new = _even_tail_prompt(B, L, x, o_a, rq, rk, rv, rg, mod, l, e, p)
                    rets[0].append(s_new)
                    new_x.append(x)
                    continue
                else:
                    q, k, v, qi, kw, rq, rk, rv, rg = _even_in(
                        t, False, x, mod, 3, l, e, p, rope_a_s, rope_r_s, rope_map_s)
                    scores = _dsa_s_scores(Bs, T, e, page_table, qi, kw, kidx_t)
                    bias = _dsa_s_select(Bs * T, T, past, scores)
                    o_a = _dsa_s_attn(Bs, T, e, page_table, q, k, v, bias, k_t, v_t)
                    o_r, s_new = _ret_sample(Bs, T, e, rq, rk, rv, rg, state_ret)
                    ks[1].append(k.reshape(Bs, T, A_KV, A_DH))
                    vs[1].append(v.reshape(Bs, T, A_KV, A_DH))
                    kis[1].append(kw[:, :IDX_DIM].reshape(Bs, T, IDX_DIM))
                rets[gi].append(s_new)
                x = _even_out(t, x, o_a, o_r, mod, 5, l, e, p)
            else:
                if gi == 0:
                    x, tail, hlast = _odd_ffn_prompt(B, L, x, mod, l, e, p)
                    convs[0].append(tail[:, SUBLANES - (CONV_W - 1):])
                    hs[0].append(hlast[:, SUBLANES - 1])
                    new_x.append(x)
                    continue
                else:
                    prev = jnp.pad(state_conv[e], ((0, 0), (SUBLANES - (CONV_W - 1), 0), (0, 0))).reshape(Bs * T, D)
                    h0 = jnp.pad(state_rglru[e][:, None, :], ((0, 0), (0, T - 1), (0, 0))).reshape(Bs * T, D)
                    x, xb, hseq = _odd_sample(ts_odd, x, mod, 3, l, e, p, prev, h0)
                    convs[1].append(xb.reshape(Bs, T, D)[:, T - (CONV_W - 1):])
                    hs[1].append(hseq.reshape(Bs, T, D)[:, T - 1])
            x = _ffn(t, x, mod, 6, l, 1, 2, p)
            new_x.append(x)
        xp, xs = new_x

    st = lambda lists, gi: jnp.stack(lists[gi])
    return (xp.reshape(B, L, D), xs.reshape(Bs, T, D),
            st(ks, 0), st(vs, 0), st(kis, 0), st(rets, 0), st(convs, 0), st(hs, 0),
            st(ks, 1), st(vs, 1), st(kis, 1), st(rets, 1), st(convs, 1), st(hs, 1))
```

```python
import functools
import math

import jax
import jax.numpy as jnp
import numpy as np
from jax import lax
from jax.experimental import pallas as pl
from jax.experimental.pallas import tpu as pltpu

F32 = jnp.float32
BF16 = jnp.bfloat16

D = 1024
DEPTH = 4
N_MOD = 9
A_HEADS = 8
A_KV = 2
A_DH = 64
ROT_DIM = 16
ROPE_THETA = 500000.0
IDX_HEADS = 4
IDX_DIM = 64
TOPK_MAX = 256
RET_HEADS = 4
RET_DH = 128
RET_THETA = 10000.0
RG_BLOCKS = 8
RG_BW = 128
CONV_W = 4
RG_C = 8.0
D_FF = 2816
EPS = 1e-6
P_IN = 3200
PAGE = 128

LANES = 128
SUBLANES = 8
VMEM_LIMIT = 56 * 1024 * 1024

NEG_INF = float("-inf")


def _cparams(n_axes, vmem=VMEM_LIMIT):
    return pltpu.CompilerParams(dimension_semantics=("arbitrary",) * n_axes, vmem_limit_bytes=vmem)


def _dot(a, b):
    return jnp.dot(a, b, preferred_element_type=F32)


def _dot_nt(a, b):
    return lax.dot_general(a, b, (((1,), (1,)), ((), ())), preferred_element_type=F32)


def _dot_tn(a, b):
    return lax.dot_general(a, b, (((0,), (0,)), ((), ())), preferred_element_type=F32)


def _sigmoid(x):
    return 1.0 / (1.0 + jnp.exp(-x))


def _silu(x):
    return x * _sigmoid(x)


def _gelu_tanh(x):
    return 0.5 * x * (1.0 + jnp.tanh(math.sqrt(2.0 / math.pi) * (x + 0.044715 * (x * x * x))))


def _rms(x, g=None):
    y = x * lax.rsqrt(jnp.mean(x * x, axis=-1, keepdims=True) + EPS)
    return y if g is None else y * g


def _prenorm(x, g_pre, shift, scale, g, rt):
    h = _rms(x, g_pre)
    if g == 1:
        return h * (1.0 + scale[0]) + shift[0]
    h3 = h.reshape(g, rt, D)
    return (h3 * (1.0 + scale) + shift).reshape(g * rt, D)


def _gated(y, gate, g, rt):
    if g == 1:
        return y * gate[0]
    return (y.reshape(g, rt, D) * gate).reshape(g * rt, D)


class _Tiling:
    def __init__(self, G, R, g, rt):
        assert G % g == 0 and R % rt == 0 and (g == 1 or rt == R)
        self.G, self.R, self.g, self.rt = G, R, g, rt
        self.nr = R // rt
        self.rows = g * rt
        self.grid = (G // g, self.nr)
        self.n_tok = G * R

    def tok(self, width):
        nr = self.nr
        return pl.BlockSpec((self.rows, width), lambda i, j: (i * nr + j, 0))

    def mod(self, c):
        return pl.BlockSpec((self.g, 1, D), lambda i, j: (i, 0, c))

    def sel(self, block, idx):
        return pl.BlockSpec(block, lambda i, j: idx, pipeline_mode=pl.Buffered(1))


def _ada_body(c_ref, w_ref, b_ref, o_ref):
    c = _silu(c_ref[...]).astype(BF16)
    o_ref[0] = _dot(c, w_ref[0].astype(BF16)) + b_ref[0]


def _ada(c_all, w_ada, b_ada):
    n = c_all.shape[0]
    return pl.pallas_call(
        _ada_body,
        out_shape=jax.ShapeDtypeStruct((DEPTH, n, N_MOD * D), F32),
        grid=(DEPTH, N_MOD),
        in_specs=[pl.BlockSpec((n, D), lambda l, j: (0, 0)),
                  pl.BlockSpec((1, D, D), lambda l, j: (l, 0, j)),
                  pl.BlockSpec((1, 1, D), lambda l, j: (l, 0, j))],
        out_specs=pl.BlockSpec((1, n, D), lambda l, j: (l, 0, j)),
        compiler_params=_cparams(2),
        name="ada",
    )(c_all, w_ada, b_ada.reshape(DEPTH, 1, N_MOD * D))


def _ffn_body(x_ref, sh_ref, sc_ref, gt_ref, gpre_ref, gpost_ref, wg_ref, wu_ref, wd_ref, o_ref, *, g, rt):
    x = x_ref[...]
    h = _prenorm(x, gpre_ref[...], sh_ref[...], sc_ref[...], g, rt).astype(BF16)
    a = _dot(h, wg_ref[...])
    u = _dot(h, wu_ref[...])
    act = (_silu(a) * u).astype(BF16)
    f = _dot(act, wd_ref[...])
    y = _rms(f, gpost_ref[...])
    o_ref[...] = x + 0.5 * _gated(y, gt_ref[...], g, rt)


def _ffn(t, x, mod, mcol, l, s, n, p):
    return pl.pallas_call(
        functools.partial(_ffn_body, g=t.g, rt=t.rt),
        out_shape=jax.ShapeDtypeStruct((t.n_tok, D), F32),
        grid=t.grid,
        in_specs=[t.tok(D), t.mod(mcol), t.mod(mcol + 1), t.mod(mcol + 2),
                  t.sel((None, None, 1, D), (l, n, 0, 0)), t.sel((None, None, 1, D), (l, n, 0, 0)),
                  t.sel((None, None, D, D_FF), (l, s, 0, 0)), t.sel((None, None, D, D_FF), (l, s, 0, 0)),
                  t.sel((None, None, D_FF, D), (l, s, 0, 0))],
        out_specs=t.tok(D),
        compiler_params=_cparams(2),
        name="ffn",
    )(x, mod, mod, mod, p["g_pre"], p["g_post"], p["wg"], p["wu"], p["wd"])


def _ffn_stream_body(x_ref, sh_ref, sc_ref, gt_ref, gpre_ref, gpost_ref, wg_ref, wu_ref, wd_ref, o_ref,
                     h_ref, acc_ref, *, g, rt):
    c = pl.program_id(0)

    @pl.when(c == 0)
    def _():
        h_ref[...] = _prenorm(x_ref[...], gpre_ref[...], sh_ref[...], sc_ref[...], g, rt).astype(BF16)
        acc_ref[...] = jnp.zeros_like(acc_ref)

    h = h_ref[...]
    act = (_silu(_dot(h, wg_ref[...])) * _dot(h, wu_ref[...])).astype(BF16)
    acc_ref[...] += _dot(act, wd_ref[...])

    @pl.when(c == pl.num_programs(0) - 1)
    def _():
        o_ref[...] = x_ref[...] + 0.5 * _gated(_rms(acc_ref[...], gpost_ref[...]), gt_ref[...], g, rt)


def _ffn_stream(G, R, x, mod, mcol, l, s, n, p, n_slabs=2):
    n_tok = G * R
    slab = D_FF // n_slabs
    assert slab % LANES == 0
    once = lambda block, idx: pl.BlockSpec(block, lambda c: idx, pipeline_mode=pl.Buffered(1))
    modc = lambda col: once((G, 1, D), (0, 0, col))
    gain = once((None, None, 1, D), (l, n, 0, 0))
    return pl.pallas_call(
        functools.partial(_ffn_stream_body, g=G, rt=R),
        out_shape=jax.ShapeDtypeStruct((n_tok, D), F32),
        grid=(n_slabs,),
        in_specs=[once((n_tok, D), (0, 0)), modc(mcol), modc(mcol + 1), modc(mcol + 2), gain, gain,
                  pl.BlockSpec((None, None, D, slab), lambda c: (l, s, 0, c)),
                  pl.BlockSpec((None, None, D, slab), lambda c: (l, s, 0, c)),
                  pl.BlockSpec((None, None, slab, D), lambda c: (l, s, c, 0))],
        out_specs=pl.BlockSpec((n_tok, D), lambda c: (0, 0)),
        scratch_shapes=[pltpu.VMEM((n_tok, D), BF16), pltpu.VMEM((n_tok, D), F32)],
        compiler_params=_cparams(1),
        name="ffn_stream",
    )(x, mod, mod, mod, p["g_pre"], p["g_post"], p["wg"], p["wu"], p["wd"])


def _even_in_body(*refs, g, rt, prompt):
    x_ref, sh_ref, sc_ref, gpre_ref, w_ref, ra_ref, rr_ref = refs[:7]
    q_ref, k_ref, v_ref, qi_ref, kw_ref = refs[7:12]
    if prompt:
        kit_ref, rq_ref, rk_ref, rv_ref, rg_ref = refs[12:]
    else:
        rq_ref, rk_ref, rv_ref, rg_ref = refs[12:]
    h = _prenorm(x_ref[...], gpre_ref[...], sh_ref[...], sc_ref[...], g, rt).astype(BF16)
    proj = _dot(h, w_ref[...])

    def sec(c):
        return proj[:, c * LANES:(c + 1) * LANES]

    def rope_a(t, v):
        return (t * ra_ref[3 * v] + pltpu.roll(t, LANES - ROT_DIM // 2, 1) * ra_ref[3 * v + 1]
                + pltpu.roll(t, ROT_DIM // 2, 1) * ra_ref[3 * v + 2])

    def rope_r(t):
        return t * rr_ref[0] + pltpu.roll(t, RET_DH // 2, 1) * rr_ref[1]

    for c in range(4):
        q_ref[:, c * LANES:(c + 1) * LANES] = rope_a(sec(c), 0) * (A_DH ** -0.5)
    k = rope_a(sec(4), 0)
    v_ref[...] = sec(5)
    for c in range(2):
        qi_ref[:, c * LANES:(c + 1) * LANES] = rope_a(sec(6 + c), 0) * (IDX_DIM ** -0.5)
    kw = rope_a(sec(8), 1)
    kw_ref[...] = kw
    if prompt:
        k_ref[...] = k.T
        kit_ref[...] = kw.T[0:IDX_DIM]
    else:
        k_ref[...] = k
    for c in range(4):
        rq_ref[:, c * LANES:(c + 1) * LANES] = rope_r(sec(9 + c))
        rk_ref[:, c * LANES:(c + 1) * LANES] = rope_r(sec(13 + c) * (RET_DH ** -0.5))
        rv_ref[:, c * LANES:(c + 1) * LANES] = sec(17 + c)
        rg_ref[:, c * LANES:(c + 1) * LANES] = sec(21 + c)


def _even_in(t, prompt, x, mod, mcol, l, e, p, rope_a, rope_r, rope_map):
    tokw = lambda w: (jax.ShapeDtypeStruct((t.n_tok, w), F32), t.tok(w))
    seq_t = lambda w: (jax.ShapeDtypeStruct((t.G, w, t.R), F32),
                       pl.BlockSpec((None, w, t.rt), lambda i, j: (i, 0, j)))
    outs = [tokw(512), seq_t(LANES) if prompt else tokw(LANES), tokw(LANES), tokw(256), tokw(LANES)]
    if prompt:
        outs.append(seq_t(IDX_DIM))
    outs += [tokw(512)] * 4
    return pl.pallas_call(
        functools.partial(_even_in_body, g=t.g, rt=t.rt, prompt=prompt),
        out_shape=[o[0] for o in outs],
        grid=t.grid,
        in_specs=[t.tok(D), t.mod(mcol), t.mod(mcol + 1), t.sel((None, None, 1, D), (l, 1, 0, 0)),
                  t.sel((None, D, P_IN), (e, 0, 0)),
                  pl.BlockSpec((6, t.rows, LANES), rope_map),
                  pl.BlockSpec((2, t.rows, LANES), rope_map)],
        out_specs=[o[1] for o in outs],
        compiler_params=_cparams(2),
        name="even_in_p" if prompt else "even_in_s",
    )(x, mod, mod, p["g_pre"], p["w_in_even"], rope_a, rope_r)


def _sortable(s):
    s = jnp.where(s == 0.0, 0.0, s)
    b = lax.bitcast_convert_type(s, jnp.int32)
    return b ^ ((b >> 31) & 0x7FFFFFFF)


def _topk_bias_t(scores_t, visible_t, k):
    s_len, n_q = scores_t.shape
    key = _sortable(scores_t)
    fold = 8 * SUBLANES
    assert s_len % fold == 0 and s_len % LANES == 0
    i16 = jnp.int16

    def count(mask):
        part = jnp.sum(jnp.where(mask, 1.0, 0.0).reshape(s_len // fold, fold, n_q), axis=0)
        return jnp.sum(part, axis=0, keepdims=True)

    def count16(mask):
        ind = jnp.where(mask, i16(1), i16(0))
        part = ind[0:LANES]
        for r in range(1, s_len // LANES):
            part = part + ind[r * LANES:(r + 1) * LANES]
        return jnp.sum(part.astype(jnp.int32).astype(F32), axis=0, keepdims=True)

    def kth16(vals, kk):
        def step(it, t):
            cand = t + jnp.left_shift(jnp.int32(1), 15 - it)
            return jnp.where(count16(vals >= cand.astype(i16)) >= kk, cand, t)
        return lax.fori_loop(0, 16, step, jnp.full((1, n_q), -(2 ** 15), jnp.int32))

    hi = (key >> 16).astype(i16)
    t_hi = kth16(hi, k)
    t_hi16 = t_hi.astype(i16)
    lo = jnp.where(hi == t_hi16, ((key & 0xFFFF) - 2 ** 15).astype(i16), i16(-(2 ** 15)))
    t_lo = kth16(lo, k - count16(hi > t_hi16))
    thr = jnp.left_shift(t_hi, 16) + (t_lo + 2 ** 15)
    ge = key >= thr
    cnt_ge = count(ge)

    def no_ties():
        return jnp.where(visible_t, jnp.where(ge, 0.0, NEG_INF), NEG_INF)

    def ties():
        need = k - count(key > thr)
        tri = jnp.where(lax.broadcasted_iota(jnp.int32, (LANES, LANES), 0)
                        >= lax.broadcasted_iota(jnp.int32, (LANES, LANES), 1), 1.0, 0.0).astype(BF16)
        off = jnp.zeros((1, n_q), F32)
        pieces = []
        for c in range(s_len // LANES):
            sl = slice(c * LANES, (c + 1) * LANES)
            kc = key[sl, :]
            eq = kc == thr
            pc = _dot(tri, jnp.where(eq, 1.0, 0.0).astype(BF16)) + off
            off = pc[LANES - 1:LANES, :]
            tie_ok = jnp.where(eq, jnp.where(pc <= need, 0.0, NEG_INF), NEG_INF)
            b = jnp.where(kc > thr, 0.0, tie_ok)
            pieces.append(jnp.where(visible_t[sl, :], b, NEG_INF))
        return jnp.concatenate(pieces, axis=0)

    return lax.cond(jnp.max(cnt_ge) > k, ties, no_ties)


def _softmax(logits):
    m = jnp.max(logits, axis=-1, keepdims=True)
    p = jnp.exp(logits - m)
    return p, jnp.sum(p, axis=-1, keepdims=True)


def _dsa_p_body(q_ref, qi_ref, kw_ref, kt_ref, v_ref, o_ref, *, tq, topk):
    s_len = q_ref.shape[0]
    rep = A_HEADS // A_KV
    kt = kt_ref[...].astype(BF16)
    ki = kw_ref[:, 0:IDX_DIM].astype(BF16)
    v = v_ref[...].astype(BF16)
    for i in range(s_len // tq):
        rows = slice(i * tq, (i + 1) * tq)
        s_vis = (i + 1) * tq
        qit = qi_ref[rows, :].T.astype(BF16)
        wit = kw_ref[rows, :].T
        sc = jnp.zeros((s_vis, tq), F32)
        for h in range(IDX_HEADS):
            raw = _dot(ki[:s_vis], qit[h * IDX_DIM:(h + 1) * IDX_DIM, :])
            sc = sc + jnp.maximum(raw, 0.0) * wit[IDX_DIM + h:IDX_DIM + h + 1, :]
        kpos = lax.broadcasted_iota(jnp.int32, (s_vis, 1), 0)
        qpos = i * tq + lax.broadcasted_iota(jnp.int32, (1, tq), 1)
        visible = kpos <= qpos
        bias = _topk_bias_t(jnp.where(visible, sc, NEG_INF), visible, topk).T
        q = q_ref[rows, :].astype(BF16)
        for hp in range(A_HEADS // 2):
            outs = []
            for h in (2 * hp, 2 * hp + 1):
                gsl = slice((h // rep) * A_DH, (h // rep + 1) * A_DH)
                p, l = _softmax(_dot(q[:, h * A_DH:(h + 1) * A_DH], kt[gsl, :s_vis]) + bias)
                outs.append(_dot(p.astype(BF16), v[:s_vis, gsl]) / l)
            o_ref[rows, hp * LANES:(hp + 1) * LANES] = jnp.concatenate(outs, axis=-1)


def _dsa_prompt(B, L, q, qi, kw, kt, v, tq=256):
    tok = lambda w: pl.BlockSpec((L, w), lambda b: (b, 0))
    seq_t = lambda w: pl.BlockSpec((None, w, L), lambda b: (b, 0, 0))
    return pl.pallas_call(
        functools.partial(_dsa_p_body, tq=tq, topk=min(TOPK_MAX, L // 4)),
        out_shape=jax.ShapeDtypeStruct((B * L, A_HEADS * A_DH), F32),
        grid=(B,),
        in_specs=[tok(512), tok(256), tok(LANES), seq_t(LANES), tok(LANES)],
        out_specs=tok(512),
        compiler_params=_cparams(1),
        name="dsa_prompt",
    )(q, qi, kw, kt, v)


SEQ_PER_STEP = 4


def _page_gather(pt_ref, pool_ref, buf_ref, sem_ref, e, n_pages):
    b = pl.program_id(0)

    def copies(step, slot):
        return [pltpu.make_async_copy(pool_ref.at[e, pt_ref[step * SEQ_PER_STEP + s, p]],
                                      buf_ref.at[slot, s, :, pl.ds(p * PAGE, PAGE)],
                                      sem_ref.at[slot])
                for s in range(SEQ_PER_STEP) for p in range(n_pages)]

    @pl.when(b == 0)
    def _():
        for c in copies(0, 0):
            c.start()

    @pl.when(b + 1 < pl.num_programs(0))
    def _():
        for c in copies(b + 1, (b + 1) % 2):
            c.start()

    slot = b % 2
    for c in copies(b, slot):
        c.wait()
    return slot


def _dsa_s_scores_body(pt_ref, qi_ref, kw_ref, pool_ref, o_ref, buf_ref, sem_ref, *, e, n_pages, t):
    slot = _page_gather(pt_ref, pool_ref, buf_ref, sem_ref, e, n_pages)
    past = n_pages * PAGE
    lane = lax.broadcasted_iota(jnp.int32, (t, PAGE), 1)
    row = lax.broadcasted_iota(jnp.int32, (t, PAGE), 0)
    for s in range(SEQ_PER_STEP):
        rows = slice(s * t, (s + 1) * t)
        qi = qi_ref[rows, :]
        qs = jnp.concatenate([qi[:, h * IDX_DIM:(h + 1) * IDX_DIM] for h in range(IDX_HEADS)], axis=0).astype(BF16)
        kw = kw_ref[rows, :]
        wi = kw[:, IDX_DIM:LANES]
        ki_new = jnp.concatenate([kw[:, 0:IDX_DIM], jnp.zeros((PAGE - t, IDX_DIM), F32)], axis=0).astype(BF16)
        kit = buf_ref[slot, s].astype(BF16)

        def weigh(raw):
            raw = jnp.maximum(raw, 0.0)
            sc = raw[0:t] * wi[:, 0:1]
            for h in range(1, IDX_HEADS):
                sc = sc + raw[h * t:(h + 1) * t] * wi[:, h:h + 1]
            return sc

        o_ref[rows, :past] = weigh(_dot(qs, kit))
        o_ref[rows, past:] = jnp.where(lane <= row, weigh(_dot_nt(qs, ki_new)), NEG_INF)


def _dsa_s_scores(Bs, T, e, page_table, qi, kw, kidx_t):
    n_pages = page_table.shape[1]
    tok = lambda w: pl.BlockSpec((SEQ_PER_STEP * T, w), lambda b, pt: (b, 0))
    s_pad = (n_pages + 1) * PAGE
    return pl.pallas_call(
        functools.partial(_dsa_s_scores_body, e=e, n_pages=n_pages, t=T),
        out_shape=jax.ShapeDtypeStruct((Bs * T, s_pad), F32),
        grid_spec=pltpu.PrefetchScalarGridSpec(
            num_scalar_prefetch=1, grid=(Bs // SEQ_PER_STEP,),
            in_specs=[tok(256), tok(LANES), pl.BlockSpec(memory_space=pl.ANY)],
            out_specs=tok(s_pad),
            scratch_shapes=[pltpu.VMEM((2, SEQ_PER_STEP, IDX_DIM, n_pages * PAGE), F32),
                            pltpu.SemaphoreType.DMA((2,))]),
        compiler_params=_cparams(1),
        name="dsa_s_scores",
    )(page_table, qi, kw, kidx_t)


def _dsa_s_select_body(sc_ref, o_ref, *, t, past, topk):
    rows, s_pad = sc_ref.shape
    tpos = jnp.bitwise_and(lax.broadcasted_iota(jnp.int32, (1, rows), 1), t - 1)
    kpos = lax.broadcasted_iota(jnp.int32, (s_pad, 1), 0)
    visible = kpos <= past + tpos
    o_ref[...] = _topk_bias_t(sc_ref[...].T, visible, topk).T


def _dsa_s_select(n_rows, T, past, scores, rows=256):
    s_pad = scores.shape[1]
    spec = pl.BlockSpec((rows, s_pad), lambda i: (i, 0))
    return pl.pallas_call(
        functools.partial(_dsa_s_select_body, t=T, past=past, topk=min(TOPK_MAX, (past + T) // 4)),
        out_shape=jax.ShapeDtypeStruct((n_rows, s_pad), F32),
        grid=(n_rows // rows,),
        in_specs=[spec], out_specs=spec,
        compiler_params=_cparams(1),
        name="dsa_s_select",
    )(scores)


def _dsa_s_attn_body(pt_ref, q_ref, kn_ref, vn_ref, b_ref, kpool_ref, vpool_ref, o_ref,
                     kbuf_ref, vbuf_ref, ksem_ref, vsem_ref, *, e, n_pages, t):
    slot = _page_gather(pt_ref, kpool_ref, kbuf_ref, ksem_ref, e, n_pages)
    _page_gather(pt_ref, vpool_ref, vbuf_ref, vsem_ref, e, n_pages)
    past = n_pages * PAGE
    rep = A_HEADS // A_KV
    pad = jnp.zeros((PAGE - t, LANES), F32)
    for s in range(SEQ_PER_STEP):
        rows = slice(s * t, (s + 1) * t)
        kt = kbuf_ref[slot, s].astype(BF16)
        vt = vbuf_ref[slot, s].astype(BF16)
        k_new = jnp.concatenate([kn_ref[rows, :], pad], axis=0).astype(BF16)
        v_new = jnp.concatenate([vn_ref[rows, :], pad], axis=0).astype(BF16)
        q = q_ref[rows, :].astype(BF16)
        bias = jnp.concatenate([b_ref[rows, :]] * rep, axis=0)
        outs = []
        for gi in range(A_KV):
            gsl = slice(gi * A_DH, (gi + 1) * A_DH)
            qg = jnp.concatenate([q[:, (gi * rep + r) * A_DH:(gi * rep + r + 1) * A_DH] for r in range(rep)], axis=0)
            logits = jnp.concatenate([_dot(qg, kt[gsl, :]), _dot_nt(qg, k_new[:, gsl])], axis=-1) + bias
            p, l = _softmax(logits)
            p = p.astype(BF16)
            og = (_dot_nt(p[:, :past], vt[gsl, :]) + _dot(p[:, past:], v_new[:, gsl])) / l
            outs += [og[r * t:(r + 1) * t] for r in range(rep)]
        for hp in range(A_HEADS // 2):
            o_ref[rows, hp * LANES:(hp + 1) * LANES] = jnp.concatenate(outs[2 * hp:2 * hp + 2], axis=-1)


def _dsa_s_attn(Bs, T, e, page_table, q, k_new, v_new, bias, k_t, v_t):
    n_pages = page_table.shape[1]
    tok = lambda w: pl.BlockSpec((SEQ_PER_STEP * T, w), lambda b, pt: (b, 0))
    pool = pl.BlockSpec(memory_space=pl.ANY)
    buf = pltpu.VMEM((2, SEQ_PER_STEP, LANES, n_pages * PAGE), F32)
    return pl.pallas_call(
        functools.partial(_dsa_s_attn_body, e=e, n_pages=n_pages, t=T),
        out_shape=jax.ShapeDtypeStruct((Bs * T, A_HEADS * A_DH), F32),
        grid_spec=pltpu.PrefetchScalarGridSpec(
            num_scalar_prefetch=1, grid=(Bs // SEQ_PER_STEP,),
            in_specs=[tok(512), tok(LANES), tok(LANES), tok(bias.shape[1]), pool, pool],
            out_specs=tok(512),
            scratch_shapes=[buf, buf, pltpu.SemaphoreType.DMA((2,)), pltpu.SemaphoreType.DMA((2,))]),
        compiler_params=_cparams(1),
        name="dsa_s_attn",
    )(page_table, q, k_new, v_new, bias, k_t, v_t)


def _ret_log_gamma(h):
    return math.log1p(-(2.0 ** (-5.0 - h)))


def _ret_chunk(q, k, v, rg, s, h, c):
    lg = _ret_log_gamma(h)
    ri = lax.broadcasted_iota(jnp.int32, (c, c), 0)
    ci = lax.broadcasted_iota(jnp.int32, (c, c), 1)
    diff = (ri - ci).astype(F32)
    decay = jnp.where(diff >= 0, jnp.exp(jnp.maximum(diff, 0.0) * lg), 0.0)
    idx = lax.broadcasted_iota(jnp.int32, (c, 1), 0).astype(F32)
    q_dec = jnp.exp((idx + 1.0) * lg)
    k_dec = jnp.exp((c - 1.0 - idx) * lg)
    c_dec = math.exp(c * lg)
    qb = q.astype(BF16)
    att = _dot_nt(qb, k.astype(BF16)) * decay
    inner = _dot(att.astype(BF16), v.astype(BF16))
    cross = _dot(qb, s.astype(BF16)) * q_dec
    s_new = s * c_dec + _dot_tn((k * k_dec).astype(BF16), v.astype(BF16))
    return _rms(inner + cross) * _silu(rg), s_new


def _ret_s_body(rq_ref, rk_ref, rv_ref, rg_ref, s0_ref, o_ref, so_ref, *, t):
    for s in range(SEQ_PER_STEP):
        rows = slice(s * t, (s + 1) * t)
        for h in range(RET_HEADS):
            sl = slice(h * RET_DH, (h + 1) * RET_DH)
            o, s_new = _ret_chunk(rq_ref[rows, sl], rk_ref[rows, sl], rv_ref[rows, sl], rg_ref[rows, sl],
                                  s0_ref[s, h], h, t)
            o_ref[rows, sl] = o
            so_ref[s, h] = s_new


def _ret_sample(Bs, T, e, rq, rk, rv, rg, state_ret):
    tok = pl.BlockSpec((SEQ_PER_STEP * T, 512), lambda b: (b, 0))
    return pl.pallas_call(
        functools.partial(_ret_s_body, t=T),
        out_shape=[jax.ShapeDtypeStruct((Bs * T, 512), F32),
                   jax.ShapeDtypeStruct((Bs, RET_HEADS, RET_DH, RET_DH), F32)],
        grid=(Bs // SEQ_PER_STEP,),
        in_specs=[tok, tok, tok, tok,
                  pl.BlockSpec((None, SEQ_PER_STEP, RET_HEADS, RET_DH, RET_DH), lambda b: (e, b, 0, 0, 0))],
        out_specs=[tok, pl.BlockSpec((SEQ_PER_STEP, RET_HEADS, RET_DH, RET_DH), lambda b: (b, 0, 0, 0))],
        compiler_params=_cparams(1),
        name="ret_sample",
    )(rq, rk, rv, rg, state_ret)


def _even_out_body(x_ref, oa_ref, or_ref, gt_ref, gpost_ref, wa_ref, wr_ref, o_ref, *, g, rt):
    y = _dot(oa_ref[...].astype(BF16), wa_ref[...]) + _dot(or_ref[...].astype(BF16), wr_ref[...])
    o_ref[...] = x_ref[...] + _gated(_rms(y, gpost_ref[...]), gt_ref[...], g, rt)


def _even_out(t, x, o_a, o_r, mod, mcol, l, e, p):
    half = A_HEADS * A_DH
    return pl.pallas_call(
        functools.partial(_even_out_body, g=t.g, rt=t.rt),
        out_shape=jax.ShapeDtypeStruct((t.n_tok, D), F32),
        grid=t.grid,
        in_specs=[t.tok(D), t.tok(half), t.tok(half), t.mod(mcol), t.sel((None, None, 1, D), (l, 1, 0, 0)),
                  t.sel((None, half, D), (e, 0, 0)), t.sel((None, half, D), (e, 1, 0))],
        out_specs=t.tok(D),
        compiler_params=_cparams(2),
        name="even_out",
    )(x, o_a, o_r, mod, p["g_post"], p["w_out_even"], p["w_out_even"])


FF_HALF = D_FF // 2
RET_CHUNK = 256


def _ffn_staged(x, mod_refs, gpre_ref, gpost_ref, wg_ref, wu_ref, wd_ref, between=()):
    sh_ref, sc_ref, gt_ref = mod_refs
    hf = _prenorm(x, gpre_ref[...], sh_ref[...], sc_ref[...], 1, x.shape[0]).astype(BF16)
    act = []
    for c in range(2):
        cols = slice(c * FF_HALF, (c + 1) * FF_HALF)
        a_c, u_c = _dot(hf, wg_ref[:, cols]), _dot(hf, wu_ref[:, cols])
        if c < len(between):
            between[c]()
        act.append((_silu(a_c) * u_c).astype(BF16))
    f = _dot(act[0], wd_ref[0:FF_HALF, :]) + _dot(act[1], wd_ref[FF_HALF:D_FF, :])
    return x + 0.5 * (_rms(f, gpost_ref[...]) * gt_ref[...][0])


def _even_tail_body(rq_ref, rk_ref, rv_ref, rg_ref, x_ref, oa_ref, gt_ref, sh2_ref, sc2_ref, gt2_ref,
                    gpost_ref, gpre2_ref, gpost2_ref, wa_ref, wr_ref, wg_ref, wu_ref, wd_ref,
                    o_ref, so_ref, or_ref, s_ref, *, rows, nt, n_tiles):
    t = pl.program_id(0)
    j = lax.rem(t, nt)
    slot = lax.rem(t, 2)

    @pl.when(t == 0)
    def _():
        or_ref[...] = jnp.zeros_like(or_ref)

    @pl.when(j == 0)
    def _():
        s_ref[...] = jnp.zeros_like(s_ref)

    def retention(c):
        def run():
            rs = slice(c * RET_CHUNK, (c + 1) * RET_CHUNK)
            for h in range(RET_HEADS):
                sl = slice(h * RET_DH, (h + 1) * RET_DH)
                o, s_new = _ret_chunk(rq_ref[rs, sl], rk_ref[rs, sl], rv_ref[rs, sl], rg_ref[rs, sl],
                                      s_ref[h], h, RET_CHUNK)
                or_ref[slot, rs, sl] = o
                s_ref[h] = s_new
        return run

    o_r = or_ref[1 - slot]
    y = _dot(oa_ref[...].astype(BF16), wa_ref[...]) + _dot(o_r.astype(BF16), wr_ref[...])
    x1 = x_ref[...] + _rms(y, gpost_ref[...]) * gt_ref[...][0]
    o_ref[...] = _ffn_staged(x1, (sh2_ref, sc2_ref, gt2_ref), gpre2_ref, gpost2_ref, wg_ref, wu_ref, wd_ref,
                             between=[retention(c) for c in range(rows // RET_CHUNK)])

    @pl.when(t < n_tiles)
    def _():
        so_ref[0] = s_ref[...]


def _even_tail_prompt(B, L, x, o_a, rq, rk, rv, rg, mod, l, e, p, rows=512):
    assert rows == 2 * RET_CHUNK
    nt = L // rows
    n_tiles = B * nt
    half = A_HEADS * A_DH
    cur = lambda t: jnp.minimum(t, n_tiles - 1)
    prv = lambda t: jnp.maximum(t - 1, 0)
    sel = lambda block, idx: pl.BlockSpec(block, lambda t: idx, pipeline_mode=pl.Buffered(1))
    modc = lambda c: pl.BlockSpec((1, 1, D), lambda t: (prv(t) // nt, 0, c))
    gain = lambda n: sel((None, None, 1, D), (l, n, 0, 0))
    ret_in = pl.BlockSpec((rows, RET_HEADS * RET_DH), lambda t: (cur(t), 0))
    return pl.pallas_call(
        functools.partial(_even_tail_body, rows=rows, nt=nt, n_tiles=n_tiles),
        out_shape=[jax.ShapeDtypeStruct((B * L, D), F32),
                   jax.ShapeDtypeStruct((B, RET_HEADS, RET_DH, RET_DH), F32)],
        grid=(n_tiles + 1,),
        in_specs=[ret_in, ret_in, ret_in, ret_in,
                  pl.BlockSpec((rows, D), lambda t: (prv(t), 0)), pl.BlockSpec((rows, half), lambda t: (prv(t), 0)),
                  modc(5), modc(6), modc(7), modc(8), gain(1), gain(2), gain(2),
                  sel((None, half, D), (e, 0, 0)), sel((None, half, D), (e, 1, 0)),
                  sel((None, None, D, D_FF), (l, 1, 0, 0)), sel((None, None, D, D_FF), (l, 1, 0, 0)),
                  sel((None, None, D_FF, D), (l, 1, 0, 0))],
        out_specs=[pl.BlockSpec((rows, D), lambda t: (prv(t), 0)),
                   pl.BlockSpec((1, RET_HEADS, RET_DH, RET_DH), lambda t: (cur(t) // nt, 0, 0, 0))],
        scratch_shapes=[pltpu.VMEM((2, rows, RET_HEADS * RET_DH), F32), pltpu.VMEM((RET_HEADS, RET_DH, RET_DH), F32)],
        compiler_params=_cparams(1),
        name="even_tail_prompt",
    )(rq, rk, rv, rg, x, o_a, mod, mod, mod, mod, p["g_post"], p["g_pre"], p["g_post"],
      p["w_out_even"], p["w_out_even"], p["wg"], p["wu"], p["wd"])


def _softplus(x):
    return jnp.maximum(x, 0.0) + jnp.log1p(jnp.exp(-jnp.abs(x)))


def _scan_sublanes(a, b):
    sub = lax.broadcasted_iota(jnp.int32, (1, SUBLANES, a.shape[-1]), 1)
    d = 1
    while d < SUBLANES:
        keep = sub >= d
        a_sh = jnp.where(keep, pltpu.roll(a, d, 1), 1.0)
        b_sh = jnp.where(keep, pltpu.roll(b, d, 1), 0.0)
        b = b + a * b_sh
        a = a * a_sh
        d *= 2
    return a, b


def _rglru_gates(conv, wa_ref, wx_ref):
    convb = conv.astype(BF16)
    ra = jnp.concatenate([_dot(convb[:, n * RG_BW:(n + 1) * RG_BW], wa_ref[n]) for n in range(RG_BLOCKS)], axis=-1)
    rx = jnp.concatenate([_dot(convb[:, n * RG_BW:(n + 1) * RG_BW], wx_ref[n]) for n in range(RG_BLOCKS)], axis=-1)
    return ra, rx


def _odd_s_body(x_ref, sh_ref, sc_ref, gt_ref, gpre_ref, gpost_ref, win_ref, cw_ref, cb_ref,
                wa_ref, ba_ref, wx_ref, bx_ref, lam_ref, wout_ref, prev_ref, h0_ref,
                o_ref, xb_out_ref, hs_out_ref, *, g, rt):
    rows = g * rt
    x = x_ref[...]
    h = _prenorm(x, gpre_ref[...], sh_ref[...], sc_ref[...], g, rt).astype(BF16)
    proj = _dot(h, win_ref[...])
    gate_br = proj[:, :D]
    xb = proj[:, D:]
    pos = jnp.bitwise_and(lax.broadcasted_iota(jnp.int32, (rows, 1), 0), rt - 1)
    cw = cw_ref[...]
    prev = prev_ref[...]
    conv = xb * cw[CONV_W - 1:CONV_W] + cb_ref[...]
    for jj in range(CONV_W - 1):
        d = CONV_W - 1 - jj
        tap = jnp.where(pos >= d, pltpu.roll(xb, d, 0), pltpu.roll(prev, rows - (SUBLANES - d), 0))
        conv = conv + tap * cw[jj:jj + 1]
    xb_out_ref[...] = xb
    ra, rx = _rglru_gates(conv, wa_ref, wx_ref)
    r = _sigmoid(ra + ba_ref[...])
    ig = _sigmoid(rx + bx_ref[...])
    a = jnp.exp((-RG_C) * r * _softplus(-lam_ref[...]))
    b = jnp.sqrt(1.0 - a * a) * (ig * conv)
    b = b + a * h0_ref[...]
    n_grp = rows // SUBLANES
    hs = _scan_sublanes(a.reshape(n_grp, SUBLANES, D), b.reshape(n_grp, SUBLANES, D))[1].reshape(rows, D)
    hs_out_ref[...] = hs
    y = (hs * _gelu_tanh(gate_br)).astype(BF16)
    out = _dot(y, wout_ref[...])
    o_ref[...] = x + _gated(_rms(out, gpost_ref[...]), gt_ref[...], g, rt)


def _odd_sample(t, x, mod, mcol, l, e, p, prev, h0):
    vec = t.sel((None, 1, D), (e, 0, 0))
    rgw = t.sel((None, RG_BLOCKS, RG_BW, RG_BW), (e, 0, 0, 0))
    return pl.pallas_call(
        functools.partial(_odd_s_body, g=t.g, rt=t.rt),
        out_shape=[jax.ShapeDtypeStruct((t.n_tok, D), F32)] * 3,
        grid=t.grid,
        in_specs=[t.tok(D), t.mod(mcol), t.mod(mcol + 1), t.mod(mcol + 2),
                  t.sel((None, None, 1, D), (l, 1, 0, 0)), t.sel((None, None, 1, D), (l, 1, 0, 0)),
                  t.sel((None, D, 2 * D), (e, 0, 0)), t.sel((None, CONV_W, D), (e, 0, 0)), vec,
                  rgw, vec, rgw, vec, vec, t.sel((None, D, D), (e, 0, 0)), t.tok(D), t.tok(D)],
        out_specs=[t.tok(D)] * 3,
        compiler_params=_cparams(2),
        name="odd_sample",
    )(x, mod, mod, mod, p["g_pre"], p["g_post"], p["w_in_odd"], p["conv_w"], p["conv_b"],
      p["w_rg_a"], p["b_rg_a"], p["w_rg_x"], p["b_rg_x"], p["rg_lambda"], p["w_out_odd"], prev, h0)


def _odd_ffn_body(x_ref, sh_ref, sc_ref, gt_ref, sh2_ref, sc2_ref, gt2_ref,
                  gpre_ref, gpost_ref, gpre2_ref, gpost2_ref, win_ref, cw_ref, cb_ref,
                  wa_ref, ba_ref, wx_ref, bx_ref, lam_ref, wout_ref, wg_ref, wu_ref, wd_ref,
                  o_ref, tail_out_ref, h_out_ref, x1_ref, tail_ref, hcar_ref, *, rows, nt, n_tiles):
    t = pl.program_id(0)
    j = lax.rem(t, nt)
    slot = lax.rem(t, 2)
    n_grp = rows // SUBLANES

    @pl.when(t == 0)
    def _():
        x1_ref[...] = jnp.zeros_like(x1_ref)

    @pl.when(j == 0)
    def _():
        tail_ref[...] = jnp.zeros_like(tail_ref)
        hcar_ref[...] = jnp.zeros_like(hcar_ref)

    x = x_ref[...]
    h = _prenorm(x, gpre_ref[...], sh_ref[...], sc_ref[...], 1, rows).astype(BF16)
    proj = _dot(h, win_ref[...])
    gate_br = proj[:, :D]
    xb = proj[:, D:]
    row = lax.broadcasted_iota(jnp.int32, (rows, 1), 0)
    cw = cw_ref[...]
    xext = jnp.concatenate([tail_ref[...], xb], axis=0)
    conv = xb * cw[CONV_W - 1:CONV_W] + cb_ref[...]
    for jj in range(CONV_W - 1):
        conv = conv + pltpu.roll(xext, CONV_W - 1 - jj, 0)[SUBLANES:] * cw[jj:jj + 1]
    tail_ref[...] = xb[rows - SUBLANES:]
    ra, rx = _rglru_gates(conv, wa_ref, wx_ref)

    x1p = x1_ref[1 - slot]
    hf = _prenorm(x1p, gpre2_ref[...], sh2_ref[...], sc2_ref[...], 1, rows).astype(BF16)
    au = [(_dot(hf, wg_ref[:, c * FF_HALF:(c + 1) * FF_HALF]), _dot(hf, wu_ref[:, c * FF_HALF:(c + 1) * FF_HALF]))
          for c in range(2)]

    r = _sigmoid(ra + ba_ref[...])
    ig = _sigmoid(rx + bx_ref[...])
    log_a = (-RG_C) * r * _softplus(-lam_ref[...])
    a = jnp.exp(log_a)
    mult = jnp.sqrt(1.0 - a * a)
    mult = jnp.where(jnp.logical_and(j == 0, row == 0), 1.0, mult)
    b = mult * (ig * conv)
    a, b = _scan_sublanes(a.reshape(n_grp, SUBLANES, D), b.reshape(n_grp, SUBLANES, D))
    carry = hcar_ref[0:1]
    parts = []
    for g8 in range(n_grp):
        part = b[g8] + a[g8] * carry
        carry = part[SUBLANES - 1:SUBLANES]
        parts.append(part)
    hs = jnp.concatenate(parts, axis=0)
    hcar_ref[0:1] = carry
    y = (hs * _gelu_tanh(gate_br)).astype(BF16)

    act = [(_silu(a_c) * u_c).astype(BF16) for a_c, u_c in au]
    x1 = x + _rms(_dot(y, wout_ref[...]), gpost_ref[...]) * gt_ref[...][0]
    x1_ref[slot] = x1
    f = _dot(act[0], wd_ref[0:FF_HALF, :]) + _dot(act[1], wd_ref[FF_HALF:D_FF, :])
    o_ref[...] = x1p + 0.5 * (_rms(f, gpost2_ref[...]) * gt2_ref[...][0])

    @pl.when(t < n_tiles)
    def _():
        tail_out_ref[0] = xb[rows - SUBLANES:]
        h_out_ref[0] = parts[-1]


def _odd_ffn_prompt(B, L, x, mod, l, e, p, rows=512):
    nt = L // rows
    n_tiles = B * nt
    cur = lambda t: jnp.minimum(t, n_tiles - 1)
    prv = lambda t: jnp.maximum(t - 1, 0)
    sel = lambda block, idx: pl.BlockSpec(block, lambda t: idx, pipeline_mode=pl.Buffered(1))
    modc = lambda c, which: pl.BlockSpec((1, 1, D), lambda t: (which(t) // nt, 0, c))
    gain = lambda n: sel((None, None, 1, D), (l, n, 0, 0))
    vec = sel((None, 1, D), (e, 0, 0))
    rgw = sel((None, RG_BLOCKS, RG_BW, RG_BW), (e, 0, 0, 0))
    last = pl.BlockSpec((1, SUBLANES, D), lambda t: (cur(t) // nt, 0, 0))
    return pl.pallas_call(
        functools.partial(_odd_ffn_body, rows=rows, nt=nt, n_tiles=n_tiles),
        out_shape=[jax.ShapeDtypeStruct((B * L, D), F32),
                   jax.ShapeDtypeStruct((B, SUBLANES, D), F32),
                   jax.ShapeDtypeStruct((B, SUBLANES, D), F32)],
        grid=(n_tiles + 1,),
        in_specs=[pl.BlockSpec((rows, D), lambda t: (cur(t), 0)),
                  modc(3, cur), modc(4, cur), modc(5, cur), modc(6, prv), modc(7, prv), modc(8, prv),
                  gain(1), gain(1), gain(2), gain(2),
                  sel((None, D, 2 * D), (e, 0, 0)), sel((None, CONV_W, D), (e, 0, 0)), vec,
                  rgw, vec, rgw, vec, vec, sel((None, D, D), (e, 0, 0)),
                  sel((None, None, D, D_FF), (l, 1, 0, 0)), sel((None, None, D, D_FF), (l, 1, 0, 0)),
                  sel((None, None, D_FF, D), (l, 1, 0, 0))],
        out_specs=[pl.BlockSpec((rows, D), lambda t: (prv(t), 0)), last, last],
        scratch_shapes=[pltpu.VMEM((2, rows, D), F32), pltpu.VMEM((SUBLANES, D), F32), pltpu.VMEM((SUBLANES, D), F32)],
        compiler_params=_cparams(1),
        name="odd_ffn_prompt",
    )(x, mod, mod, mod, mod, mod, mod, p["g_pre"], p["g_post"], p["g_pre"], p["g_post"],
      p["w_in_odd"], p["conv_w"], p["conv_b"], p["w_rg_a"], p["b_rg_a"], p["w_rg_x"], p["b_rg_x"], p["rg_lambda"],
      p["w_out_odd"], p["wg"], p["wu"], p["wd"])


def _rope_tables(pos):
    posf = pos.astype(F32)[:, None]
    inv_a = jnp.power(jnp.float32(ROPE_THETA), -jnp.arange(0, ROT_DIM, 2, dtype=F32) / ROT_DIM)
    ang = posf * inv_a[None, :]
    cos, sin = jnp.cos(ang), jnp.sin(ang)
    n = pos.shape[0]
    half = ROT_DIM // 2
    rest = A_DH - ROT_DIM
    one, zero = jnp.ones((n, rest), F32), jnp.zeros((n, rest), F32)
    zh = jnp.zeros((n, half), F32)
    cos_h = jnp.concatenate([cos, cos, one], axis=1)
    s1_h = jnp.concatenate([-sin, zh, zero], axis=1)
    s2_h = jnp.concatenate([zh, sin, zero], axis=1)
    both = lambda a: jnp.concatenate([a, a], axis=1)
    wi_scale = jnp.full((n, A_DH), IDX_HEADS ** -0.5, F32)
    z64 = jnp.zeros((n, A_DH), F32)
    rope_a = jnp.stack([both(cos_h), both(s1_h), both(s2_h),
                        jnp.concatenate([cos_h, wi_scale], axis=1),
                        jnp.concatenate([s1_h, z64], axis=1),
                        jnp.concatenate([s2_h, z64], axis=1)])
    inv_r = jnp.power(jnp.float32(RET_THETA), -jnp.linspace(0.0, 1.0, RET_DH // 2, dtype=F32))
    ang_r = posf * inv_r[None, :]
    cr, sr = jnp.cos(ang_r), jnp.sin(ang_r)
    rope_r = jnp.stack([jnp.concatenate([cr, cr], axis=1), jnp.concatenate([-sr, sr], axis=1)])
    return rope_a, rope_r


def _pack_w_in_even(w):
    o = np.cumsum((0, 512, 128, 128, 256, IDX_HEADS, IDX_DIM, 512, 512, 512, 512))
    pad = jnp.zeros(w.shape[:2] + (LANES - IDX_DIM - IDX_HEADS,), w.dtype)
    return jnp.concatenate([w[..., o[0]:o[4]], w[..., o[5]:o[6]], w[..., o[4]:o[5]], pad, w[..., o[6]:o[10]]],
                           axis=-1).astype(BF16)


def kernel(x_prompt, x_sample, cache_k, cache_v, cache_kidx, state_ret, state_conv, state_rglru, page_table,
           c_prompt, c_sample, w_ada, b_ada, g_pre, g_post, w_ffn_gate, w_ffn_up, w_ffn_down,
           w_in_even, w_out_even, w_in_odd, w_out_odd, conv_w, conv_b, w_rg_a, b_rg_a, w_rg_x, b_rg_x, rg_lambda):
    B, L, _ = x_prompt.shape
    Bs, T, _ = x_sample.shape
    n_pages = page_table.shape[1]
    past = n_pages * cache_k.shape[2]
    n_phys = cache_k.shape[1]
    assert cache_k.shape[2] == PAGE and T == SUBLANES

    tp = _Tiling(B, L, 1, 512)
    ts = _Tiling(Bs, T, 64, T)
    ts_odd = _Tiling(Bs, T, 32, T)

    vec3 = lambda a: a.reshape(a.shape[0], 1, a.shape[1])
    p = {
        "g_pre": g_pre.reshape(DEPTH, 3, 1, D), "g_post": g_post.reshape(DEPTH, 3, 1, D),
        "wg": w_ffn_gate.astype(BF16), "wu": w_ffn_up.astype(BF16), "wd": w_ffn_down.astype(BF16),
        "w_in_even": _pack_w_in_even(w_in_even), "w_out_even": w_out_even.astype(BF16),
        "w_in_odd": w_in_odd.astype(BF16), "w_out_odd": w_out_odd.astype(BF16),
        "conv_w": conv_w, "conv_b": vec3(conv_b),
        "w_rg_a": w_rg_a.astype(BF16), "b_rg_a": vec3(b_rg_a),
        "w_rg_x": w_rg_x.astype(BF16), "b_rg_x": vec3(b_rg_x), "rg_lambda": vec3(rg_lambda),
    }
    k_t = jnp.transpose(cache_k, (0, 1, 3, 4, 2)).reshape(cache_k.shape[0], n_phys, LANES, PAGE)
    v_t = jnp.transpose(cache_v, (0, 1, 3, 4, 2)).reshape(cache_v.shape[0], n_phys, LANES, PAGE)
    kidx_t = jnp.transpose(cache_kidx, (0, 1, 3, 2))

    mod_all = _ada(jnp.concatenate([c_prompt, c_sample], axis=0), w_ada, b_ada)
    mod_p = mod_all[:, :B].reshape(DEPTH, B, 1, N_MOD * D)
    mod_s = mod_all[:, B:].reshape(DEPTH, Bs, 1, N_MOD * D)

    rope_a_p, rope_r_p = _rope_tables(jnp.arange(L, dtype=jnp.int32))
    rope_a_s, rope_r_s = _rope_tables(jnp.tile(past + jnp.arange(T, dtype=jnp.int32), ts.g))
    rope_map_p = lambda i, j: (0, j, 0)
    rope_map_s = lambda i, j: (0, 0, 0)

    xp = x_prompt.reshape(B * L, D)
    xs = x_sample.reshape(Bs * T, D)
    ks, vs, kis, rets, convs, hs = ([[], []] for _ in range(6))

    for l in range(DEPTH):
        e = l // 2
        groups = ((0, tp, xp, mod_p[l]), (1, ts, xs, mod_s[l]))
        new_x = []
        for gi, t, x, mod in groups:
            x = _ffn(t, x, mod, 0, l, 0, 0, p) if gi == 0 else _ffn_stream(Bs, T, x, mod, 0, l, 0, 0, p)
            if l % 2 == 0:
                if gi == 0:
                    q, kt, v, qi, kw, kit, rq, rk, rv, rg = _even_in(
                        t, True, x, mod, 3, l, e, p, rope_a_p, rope_r_p, rope_map_p)
                    o_a = _dsa_prompt(B, L, q, qi, kw, kt, v)
                    ks[0].append(jnp.transpose(kt.reshape(B, A_KV, A_DH, L), (0, 3, 1, 2)))
                    vs[0].append(v.reshape(B, L, A_KV, A_DH))
                    kis[0].append(jnp.transpose(kit, (0, 2, 1)))
                    x, s_new = _even_tail_prompt(B, L, x, o_a, rq, rk, rv, rg, mod, l, e, p)
                    rets[0].append(s_new)
                    new_x.append(x)
                    continue
                else:
                    q, k, v, qi, kw, rq, rk, rv, rg = _even_in(
                        t, False, x, mod, 3, l, e, p, rope_a_s, rope_r_s, rope_map_s)
                    scores = _dsa_s_scores(Bs, T, e, page_table, qi, kw, kidx_t)
                    bias = _dsa_s_select(Bs * T, T, past, scores)
                    o_a = _dsa_s_attn(Bs, T, e, page_table, q, k, v, bias, k_t, v_t)
                    o_r, s_new = _ret_sample(Bs, T, e, rq, rk, rv, rg, state_ret)
                    ks[1].append(k.reshape(Bs, T, A_KV, A_DH))
                    vs[1].append(v.reshape(Bs, T, A_KV, A_DH))
                    kis[1].append(kw[:, :IDX_DIM].reshape(Bs, T, IDX_DIM))
                rets[gi].append(s_new)
                x = _even_out(t, x, o_a, o_r, mod, 5, l, e, p)
            else:
                if gi == 0:
                    x, tail, hlast = _odd_ffn_prompt(B, L, x, mod, l, e, p)
                    convs[0].append(tail[:, SUBLANES - (CONV_W - 1):])
                    hs[0].append(hlast[:, SUBLANES - 1])
                    new_x.append(x)
                    continue
                else:
                    prev = jnp.pad(state_conv[e], ((0, 0), (SUBLANES - (CONV_W - 1), 0), (0, 0))).reshape(Bs * T, D)
                    h0 = jnp.pad(state_rglru[e][:, None, :], ((0, 0), (0, T - 1), (0, 0))).reshape(Bs * T, D)
                    x, xb, hseq = _odd_sample(ts_odd, x, mod, 3, l, e, p, prev, h0)
                    convs[1].append(xb.reshape(Bs, T, D)[:, T - (CONV_W - 1):])
                    hs[1].append(hseq.reshape(Bs, T, D)[:, T - 1])
            x = _ffn_stream(Bs, T, x, mod, 6, l, 1, 2, p)
            new_x.append(x)
        xp, xs = new_x

    st = lambda lists, gi: jnp.stack(lists[gi])
    return (xp.reshape(B, L, D), xs.reshape(Bs, T, D),
            st(ks, 0), st(vs, 0), st(kis, 0), st(rets, 0), st(convs, 0), st(hs, 0),
            st(ks, 1), st(vs, 1), st(kis, 1), st(rets, 1), st(convs, 1), st(hs, 1))
```

```python
import functools
import math

import jax
import jax.numpy as jnp
import numpy as np
from jax import lax
from jax.experimental import pallas as pl
from jax.experimental.pallas import tpu as pltpu

F32 = jnp.float32
BF16 = jnp.bfloat16

D = 1024
DEPTH = 4
N_MOD = 9
A_HEADS = 8
A_KV = 2
A_DH = 64
ROT_DIM = 16
ROPE_THETA = 500000.0
IDX_HEADS = 4
IDX_DIM = 64
TOPK_MAX = 256
RET_HEADS = 4
RET_DH = 128
RET_THETA = 10000.0
RG_BLOCKS = 8
RG_BW = 128
CONV_W = 4
RG_C = 8.0
D_FF = 2816
EPS = 1e-6
P_IN = 3200
PAGE = 128

LANES = 128
SUBLANES = 8
VMEM_LIMIT = 56 * 1024 * 1024

NEG_INF = float("-inf")


def _cparams(n_axes, vmem=VMEM_LIMIT):
    return pltpu.CompilerParams(dimension_semantics=("arbitrary",) * n_axes, vmem_limit_bytes=vmem)


def _dot(a, b):
    return jnp.dot(a, b, preferred_element_type=F32)


def _dot_nt(a, b):
    return lax.dot_general(a, b, (((1,), (1,)), ((), ())), preferred_element_type=F32)


def _dot_tn(a, b):
    return lax.dot_general(a, b, (((0,), (0,)), ((), ())), preferred_element_type=F32)


def _sigmoid(x):
    return 1.0 / (1.0 + jnp.exp(-x))


def _silu(x):
    return x * _sigmoid(x)


def _gelu_tanh(x):
    return 0.5 * x * (1.0 + jnp.tanh(math.sqrt(2.0 / math.pi) * (x + 0.044715 * (x * x * x))))


def _rms(x, g=None):
    y = x * lax.rsqrt(jnp.mean(x * x, axis=-1, keepdims=True) + EPS)
    return y if g is None else y * g


def _prenorm(x, g_pre, shift, scale, g, rt):
    h = _rms(x, g_pre)
    if g == 1:
        return h * (1.0 + scale[0]) + shift[0]
    h3 = h.reshape(g, rt, D)
    return (h3 * (1.0 + scale) + shift).reshape(g * rt, D)


def _gated(y, gate, g, rt):
    if g == 1:
        return y * gate[0]
    return (y.reshape(g, rt, D) * gate).reshape(g * rt, D)


class _Tiling:
    def __init__(self, G, R, g, rt):
        assert G % g == 0 and R % rt == 0 and (g == 1 or rt == R)
        self.G, self.R, self.g, self.rt = G, R, g, rt
        self.nr = R // rt
        self.rows = g * rt
        self.grid = (G // g, self.nr)
        self.n_tok = G * R

    def tok(self, width):
        nr = self.nr
        return pl.BlockSpec((self.rows, width), lambda i, j: (i * nr + j, 0))

    def mod(self, c):
        return pl.BlockSpec((self.g, 1, D), lambda i, j: (i, 0, c))

    def sel(self, block, idx):
        return pl.BlockSpec(block, lambda i, j: idx, pipeline_mode=pl.Buffered(1))


def _ada_body(c_ref, w_ref, b_ref, o_ref):
    c = _silu(c_ref[...]).astype(BF16)
    o_ref[0] = _dot(c, w_ref[0].astype(BF16)) + b_ref[0]


def _ada(c_all, w_ada, b_ada):
    n = c_all.shape[0]
    return pl.pallas_call(
        _ada_body,
        out_shape=jax.ShapeDtypeStruct((DEPTH, n, N_MOD * D), F32),
        grid=(DEPTH, N_MOD),
        in_specs=[pl.BlockSpec((n, D), lambda l, j: (0, 0)),
                  pl.BlockSpec((1, D, D), lambda l, j: (l, 0, j)),
                  pl.BlockSpec((1, 1, D), lambda l, j: (l, 0, j))],
        out_specs=pl.BlockSpec((1, n, D), lambda l, j: (l, 0, j)),
        compiler_params=_cparams(2),
        name="ada",
    )(c_all, w_ada, b_ada.reshape(DEPTH, 1, N_MOD * D))


def _ffn_body(x_ref, sh_ref, sc_ref, gt_ref, gpre_ref, gpost_ref, wg_ref, wu_ref, wd_ref, o_ref, *, g, rt):
    x = x_ref[...]
    h = _prenorm(x, gpre_ref[...], sh_ref[...], sc_ref[...], g, rt).astype(BF16)
    a = _dot(h, wg_ref[...])
    u = _dot(h, wu_ref[...])
    act = (_silu(a) * u).astype(BF16)
    f = _dot(act, wd_ref[...])
    y = _rms(f, gpost_ref[...])
    o_ref[...] = x + 0.5 * _gated(y, gt_ref[...], g, rt)


def _ffn(t, x, mod, mcol, l, s, n, p):
    return pl.pallas_call(
        functools.partial(_ffn_body, g=t.g, rt=t.rt),
        out_shape=jax.ShapeDtypeStruct((t.n_tok, D), F32),
        grid=t.grid,
        in_specs=[t.tok(D), t.mod(mcol), t.mod(mcol + 1), t.mod(mcol + 2),
                  t.sel((None, None, 1, D), (l, n, 0, 0)), t.sel((None, None, 1, D), (l, n, 0, 0)),
                  t.sel((None, None, D, D_FF), (l, s, 0, 0)), t.sel((None, None, D, D_FF), (l, s, 0, 0)),
                  t.sel((None, None, D_FF, D), (l, s, 0, 0))],
        out_specs=t.tok(D),
        compiler_params=_cparams(2),
        name="ffn",
    )(x, mod, mod, mod, p["g_pre"], p["g_post"], p["wg"], p["wu"], p["wd"])


def _even_in_body(*refs, g, rt, prompt):
    x_ref, sh_ref, sc_ref, gpre_ref, w_ref, ra_ref, rr_ref = refs[:7]
    q_ref, k_ref, v_ref, qi_ref, kw_ref = refs[7:12]
    if prompt:
        kit_ref, rq_ref, rk_ref, rv_ref, rg_ref = refs[12:]
    else:
        rq_ref, rk_ref, rv_ref, rg_ref = refs[12:]
    h = _prenorm(x_ref[...], gpre_ref[...], sh_ref[...], sc_ref[...], g, rt).astype(BF16)
    proj = _dot(h, w_ref[...])

    def sec(c):
        return proj[:, c * LANES:(c + 1) * LANES]

    def rope_a(t, v):
        return (t * ra_ref[3 * v] + pltpu.roll(t, LANES - ROT_DIM // 2, 1) * ra_ref[3 * v + 1]
                + pltpu.roll(t, ROT_DIM // 2, 1) * ra_ref[3 * v + 2])

    def rope_r(t):
        return t * rr_ref[0] + pltpu.roll(t, RET_DH // 2, 1) * rr_ref[1]

    for c in range(4):
        q_ref[:, c * LANES:(c + 1) * LANES] = rope_a(sec(c), 0) * (A_DH ** -0.5)
    k = rope_a(sec(4), 0)
    v_ref[...] = sec(5)
    for c in range(2):
        qi_ref[:, c * LANES:(c + 1) * LANES] = rope_a(sec(6 + c), 0) * (IDX_DIM ** -0.5)
    kw = rope_a(sec(8), 1)
    kw_ref[...] = kw
    if prompt:
        k_ref[...] = k.T
        kit_ref[...] = kw.T[0:IDX_DIM]
    else:
        k_ref[...] = k
    for c in range(4):
        rq_ref[:, c * LANES:(c + 1) * LANES] = rope_r(sec(9 + c))
        rk_ref[:, c * LANES:(c + 1) * LANES] = rope_r(sec(13 + c) * (RET_DH ** -0.5))
        rv_ref[:, c * LANES:(c + 1) * LANES] = sec(17 + c)
        rg_ref[:, c * LANES:(c + 1) * LANES] = sec(21 + c)


def _even_in(t, prompt, x, mod, mcol, l, e, p, rope_a, rope_r, rope_map):
    tokw = lambda w: (jax.ShapeDtypeStruct((t.n_tok, w), F32), t.tok(w))
    seq_t = lambda w: (jax.ShapeDtypeStruct((t.G, w, t.R), F32),
                       pl.BlockSpec((None, w, t.rt), lambda i, j: (i, 0, j)))
    outs = [tokw(512), seq_t(LANES) if prompt else tokw(LANES), tokw(LANES), tokw(256), tokw(LANES)]
    if prompt:
        outs.append(seq_t(IDX_DIM))
    outs += [tokw(512)] * 4
    return pl.pallas_call(
        functools.partial(_even_in_body, g=t.g, rt=t.rt, prompt=prompt),
        out_shape=[o[0] for o in outs],
        grid=t.grid,
        in_specs=[t.tok(D), t.mod(mcol), t.mod(mcol + 1), t.sel((None, None, 1, D), (l, 1, 0, 0)),
                  t.sel((None, D, P_IN), (e, 0, 0)),
                  pl.BlockSpec((6, t.rows, LANES), rope_map),
                  pl.BlockSpec((2, t.rows, LANES), rope_map)],
        out_specs=[o[1] for o in outs],
        compiler_params=_cparams(2),
        name="even_in_p" if prompt else "even_in_s",
    )(x, mod, mod, p["g_pre"], p["w_in_even"], rope_a, rope_r)


def _sortable(s):
    s = jnp.where(s == 0.0, 0.0, s)
    b = lax.bitcast_convert_type(s, jnp.int32)
    return b ^ ((b >> 31) & 0x7FFFFFFF)


def _topk_bias_t(scores_t, visible_t, k):
    s_len, n_q = scores_t.shape
    key = _sortable(scores_t)
    fold = 8 * SUBLANES
    assert s_len % fold == 0 and s_len % LANES == 0
    i16 = jnp.int16

    def count(mask):
        part = jnp.sum(jnp.where(mask, 1.0, 0.0).reshape(s_len // fold, fold, n_q), axis=0)
        return jnp.sum(part, axis=0, keepdims=True)

    def count16(mask):
        ind = jnp.where(mask, i16(1), i16(0))
        part = ind[0:LANES]
        for r in range(1, s_len // LANES):
            part = part + ind[r * LANES:(r + 1) * LANES]
        return jnp.sum(part.astype(jnp.int32).astype(F32), axis=0, keepdims=True)

    def kth16(vals, kk):
        def step(it, t):
            cand = t + jnp.left_shift(jnp.int32(1), 15 - it)
            return jnp.where(count16(vals >= cand.astype(i16)) >= kk, cand, t)
        return lax.fori_loop(0, 16, step, jnp.full((1, n_q), -(2 ** 15), jnp.int32))

    hi = (key >> 16).astype(i16)
    t_hi = kth16(hi, k)
    t_hi16 = t_hi.astype(i16)
    lo = jnp.where(hi == t_hi16, ((key & 0xFFFF) - 2 ** 15).astype(i16), i16(-(2 ** 15)))
    t_lo = kth16(lo, k - count16(hi > t_hi16))
    thr = jnp.left_shift(t_hi, 16) + (t_lo + 2 ** 15)
    ge = key >= thr
    cnt_ge = count(ge)

    def no_ties():
        return jnp.where(visible_t, jnp.where(ge, 0.0, NEG_INF), NEG_INF)

    def ties():
        need = k - count(key > thr)
        tri = jnp.where(lax.broadcasted_iota(jnp.int32, (LANES, LANES), 0)
                        >= lax.broadcasted_iota(jnp.int32, (LANES, LANES), 1), 1.0, 0.0).astype(BF16)
        off = jnp.zeros((1, n_q), F32)
        pieces = []
        for c in range(s_len // LANES):
            sl = slice(c * LANES, (c + 1) * LANES)
            kc = key[sl, :]
            eq = kc == thr
            pc = _dot(tri, jnp.where(eq, 1.0, 0.0).astype(BF16)) + off
            off = pc[LANES - 1:LANES, :]
            tie_ok = jnp.where(eq, jnp.where(pc <= need, 0.0, NEG_INF), NEG_INF)
            b = jnp.where(kc > thr, 0.0, tie_ok)
            pieces.append(jnp.where(visible_t[sl, :], b, NEG_INF))
        return jnp.concatenate(pieces, axis=0)

    return lax.cond(jnp.max(cnt_ge) > k, ties, no_ties)


def _softmax(logits):
    m = jnp.max(logits, axis=-1, keepdims=True)
    p = jnp.exp(logits - m)
    return p, jnp.sum(p, axis=-1, keepdims=True)


def _dsa_p_body(q_ref, qi_ref, kw_ref, kt_ref, v_ref, o_ref, *, tq, topk):
    s_len = q_ref.shape[0]
    rep = A_HEADS // A_KV
    kt = kt_ref[...].astype(BF16)
    ki = kw_ref[:, 0:IDX_DIM].astype(BF16)
    v = v_ref[...].astype(BF16)
    for i in range(s_len // tq):
        rows = slice(i * tq, (i + 1) * tq)
        s_vis = (i + 1) * tq
        qit = qi_ref[rows, :].T.astype(BF16)
        wit = kw_ref[rows, :].T
        sc = jnp.zeros((s_vis, tq), F32)
        for h in range(IDX_HEADS):
            raw = _dot(ki[:s_vis], qit[h * IDX_DIM:(h + 1) * IDX_DIM, :])
            sc = sc + jnp.maximum(raw, 0.0) * wit[IDX_DIM + h:IDX_DIM + h + 1, :]
        kpos = lax.broadcasted_iota(jnp.int32, (s_vis, 1), 0)
        qpos = i * tq + lax.broadcasted_iota(jnp.int32, (1, tq), 1)
        visible = kpos <= qpos
        bias = _topk_bias_t(jnp.where(visible, sc, NEG_INF), visible, topk).T
        q = q_ref[rows, :].astype(BF16)
        for hp in range(A_HEADS // 2):
            outs = []
            for h in (2 * hp, 2 * hp + 1):
                gsl = slice((h // rep) * A_DH, (h // rep + 1) * A_DH)
                p, l = _softmax(_dot(q[:, h * A_DH:(h + 1) * A_DH], kt[gsl, :s_vis]) + bias)
                outs.append(_dot(p.astype(BF16), v[:s_vis, gsl]) / l)
            o_ref[rows, hp * LANES:(hp + 1) * LANES] = jnp.concatenate(outs, axis=-1)


def _dsa_prompt(B, L, q, qi, kw, kt, v, tq=256):
    tok = lambda w: pl.BlockSpec((L, w), lambda b: (b, 0))
    seq_t = lambda w: pl.BlockSpec((None, w, L), lambda b: (b, 0, 0))
    return pl.pallas_call(
        functools.partial(_dsa_p_body, tq=tq, topk=min(TOPK_MAX, L // 4)),
        out_shape=jax.ShapeDtypeStruct((B * L, A_HEADS * A_DH), F32),
        grid=(B,),
        in_specs=[tok(512), tok(256), tok(LANES), seq_t(LANES), tok(LANES)],
        out_specs=tok(512),
        compiler_params=_cparams(1),
        name="dsa_prompt",
    )(q, qi, kw, kt, v)


SEQ_PER_STEP = 4


def _page_gather(pt_ref, pool_ref, buf_ref, sem_ref, e, n_pages):
    b = pl.program_id(0)

    def copies(step, slot):
        return [pltpu.make_async_copy(pool_ref.at[e, pt_ref[step * SEQ_PER_STEP + s, p]],
                                      buf_ref.at[slot, s, :, pl.ds(p * PAGE, PAGE)],
                                      sem_ref.at[slot])
                for s in range(SEQ_PER_STEP) for p in range(n_pages)]

    @pl.when(b == 0)
    def _():
        for c in copies(0, 0):
            c.start()

    @pl.when(b + 1 < pl.num_programs(0))
    def _():
        for c in copies(b + 1, (b + 1) % 2):
            c.start()

    slot = b % 2
    for c in copies(b, slot):
        c.wait()
    return slot


def _dsa_s_scores_body(pt_ref, qi_ref, kw_ref, pool_ref, o_ref, buf_ref, sem_ref, *, e, n_pages, t):
    slot = _page_gather(pt_ref, pool_ref, buf_ref, sem_ref, e, n_pages)
    past = n_pages * PAGE
    lane = lax.broadcasted_iota(jnp.int32, (t, PAGE), 1)
    row = lax.broadcasted_iota(jnp.int32, (t, PAGE), 0)
    for s in range(SEQ_PER_STEP):
        rows = slice(s * t, (s + 1) * t)
        qi = qi_ref[rows, :]
        qs = jnp.concatenate([qi[:, h * IDX_DIM:(h + 1) * IDX_DIM] for h in range(IDX_HEADS)], axis=0).astype(BF16)
        kw = kw_ref[rows, :]
        wi = kw[:, IDX_DIM:LANES]
        ki_new = jnp.concatenate([kw[:, 0:IDX_DIM], jnp.zeros((PAGE - t, IDX_DIM), F32)], axis=0).astype(BF16)
        kit = buf_ref[slot, s].astype(BF16)

        def weigh(raw):
            raw = jnp.maximum(raw, 0.0)
            sc = raw[0:t] * wi[:, 0:1]
            for h in range(1, IDX_HEADS):
                sc = sc + raw[h * t:(h + 1) * t] * wi[:, h:h + 1]
            return sc

        o_ref[rows, :past] = weigh(_dot(qs, kit))
        o_ref[rows, past:] = jnp.where(lane <= row, weigh(_dot_nt(qs, ki_new)), NEG_INF)


def _dsa_s_scores(Bs, T, e, page_table, qi, kw, kidx_t):
    n_pages = page_table.shape[1]
    tok = lambda w: pl.BlockSpec((SEQ_PER_STEP * T, w), lambda b, pt: (b, 0))
    s_pad = (n_pages + 1) * PAGE
    return pl.pallas_call(
        functools.partial(_dsa_s_scores_body, e=e, n_pages=n_pages, t=T),
        out_shape=jax.ShapeDtypeStruct((Bs * T, s_pad), F32),
        grid_spec=pltpu.PrefetchScalarGridSpec(
            num_scalar_prefetch=1, grid=(Bs // SEQ_PER_STEP,),
            in_specs=[tok(256), tok(LANES), pl.BlockSpec(memory_space=pl.ANY)],
            out_specs=tok(s_pad),
            scratch_shapes=[pltpu.VMEM((2, SEQ_PER_STEP, IDX_DIM, n_pages * PAGE), F32),
                            pltpu.SemaphoreType.DMA((2,))]),
        compiler_params=_cparams(1),
        name="dsa_s_scores",
    )(page_table, qi, kw, kidx_t)


def _dsa_s_select_body(sc_ref, o_ref, *, t, past, topk):
    rows, s_pad = sc_ref.shape
    tpos = jnp.bitwise_and(lax.broadcasted_iota(jnp.int32, (1, rows), 1), t - 1)
    kpos = lax.broadcasted_iota(jnp.int32, (s_pad, 1), 0)
    visible = kpos <= past + tpos
    o_ref[...] = _topk_bias_t(sc_ref[...].T, visible, topk).T


def _dsa_s_select(n_rows, T, past, scores, rows=256):
    s_pad = scores.shape[1]
    spec = pl.BlockSpec((rows, s_pad), lambda i: (i, 0))
    return pl.pallas_call(
        functools.partial(_dsa_s_select_body, t=T, past=past, topk=min(TOPK_MAX, (past + T) // 4)),
        out_shape=jax.ShapeDtypeStruct((n_rows, s_pad), F32),
        grid=(n_rows // rows,),
        in_specs=[spec], out_specs=spec,
        compiler_params=_cparams(1),
        name="dsa_s_select",
    )(scores)


def _dsa_s_attn_body(pt_ref, q_ref, kn_ref, vn_ref, b_ref, kpool_ref, vpool_ref, o_ref,
                     kbuf_ref, vbuf_ref, ksem_ref, vsem_ref, *, e, n_pages, t):
    slot = _page_gather(pt_ref, kpool_ref, kbuf_ref, ksem_ref, e, n_pages)
    _page_gather(pt_ref, vpool_ref, vbuf_ref, vsem_ref, e, n_pages)
    past = n_pages * PAGE
    rep = A_HEADS // A_KV
    pad = jnp.zeros((PAGE - t, LANES), F32)
    for s in range(SEQ_PER_STEP):
        rows = slice(s * t, (s + 1) * t)
        kt = kbuf_ref[slot, s].astype(BF16)
        vt = vbuf_ref[slot, s].astype(BF16)
        k_new = jnp.concatenate([kn_ref[rows, :], pad], axis=0).astype(BF16)
        v_new = jnp.concatenate([vn_ref[rows, :], pad], axis=0).astype(BF16)
        q = q_ref[rows, :].astype(BF16)
        bias = jnp.concatenate([b_ref[rows, :]] * rep, axis=0)
        outs = []
        for gi in range(A_KV):
            gsl = slice(gi * A_DH, (gi + 1) * A_DH)
            qg = jnp.concatenate([q[:, (gi * rep + r) * A_DH:(gi * rep + r + 1) * A_DH] for r in range(rep)], axis=0)
            logits = jnp.concatenate([_dot(qg, kt[gsl, :]), _dot_nt(qg, k_new[:, gsl])], axis=-1) + bias
            p, l = _softmax(logits)
            p = p.astype(BF16)
            og = (_dot_nt(p[:, :past], vt[gsl, :]) + _dot(p[:, past:], v_new[:, gsl])) / l
            outs += [og[r * t:(r + 1) * t] for r in range(rep)]
        for hp in range(A_HEADS // 2):
            o_ref[rows, hp * LANES:(hp + 1) * LANES] = jnp.concatenate(outs[2 * hp:2 * hp + 2], axis=-1)


def _dsa_s_attn(Bs, T, e, page_table, q, k_new, v_new, bias, k_t, v_t):
    n_pages = page_table.shape[1]
    tok = lambda w: pl.BlockSpec((SEQ_PER_STEP * T, w), lambda b, pt: (b, 0))
    pool = pl.BlockSpec(memory_space=pl.ANY)
    buf = pltpu.VMEM((2, SEQ_PER_STEP, LANES, n_pages * PAGE), F32)
    return pl.pallas_call(
        functools.partial(_dsa_s_attn_body, e=e, n_pages=n_pages, t=T),
        out_shape=jax.ShapeDtypeStruct((Bs * T, A_HEADS * A_DH), F32),
        grid_spec=pltpu.PrefetchScalarGridSpec(
            num_scalar_prefetch=1, grid=(Bs // SEQ_PER_STEP,),
            in_specs=[tok(512), tok(LANES), tok(LANES), tok(bias.shape[1]), pool, pool],
            out_specs=tok(512),
            scratch_shapes=[buf, buf, pltpu.SemaphoreType.DMA((2,)), pltpu.SemaphoreType.DMA((2,))]),
        compiler_params=_cparams(1),
        name="dsa_s_attn",
    )(page_table, q, k_new, v_new, bias, k_t, v_t)


def _ret_log_gamma(h):
    return math.log1p(-(2.0 ** (-5.0 - h)))


def _ret_chunk(q, k, v, rg, s, h, c):
    lg = _ret_log_gamma(h)
    ri = lax.broadcasted_iota(jnp.int32, (c, c), 0)
    ci = lax.broadcasted_iota(jnp.int32, (c, c), 1)
    diff = (ri - ci).astype(F32)
    decay = jnp.where(diff >= 0, jnp.exp(jnp.maximum(diff, 0.0) * lg), 0.0)
    idx = lax.broadcasted_iota(jnp.int32, (c, 1), 0).astype(F32)
    q_dec = jnp.exp((idx + 1.0) * lg)
    k_dec = jnp.exp((c - 1.0 - idx) * lg)
    c_dec = math.exp(c * lg)
    qb = q.astype(BF16)
    att = _dot_nt(qb, k.astype(BF16)) * decay
    inner = _dot(att.astype(BF16), v.astype(BF16))
    cross = _dot(qb, s.astype(BF16)) * q_dec
    s_new = s * c_dec + _dot_tn((k * k_dec).astype(BF16), v.astype(BF16))
    return _rms(inner + cross) * _silu(rg), s_new


def _ret_s_body(rq_ref, rk_ref, rv_ref, rg_ref, s0_ref, o_ref, so_ref, *, t):
    for s in range(SEQ_PER_STEP):
        rows = slice(s * t, (s + 1) * t)
        for h in range(RET_HEADS):
            sl = slice(h * RET_DH, (h + 1) * RET_DH)
            o, s_new = _ret_chunk(rq_ref[rows, sl], rk_ref[rows, sl], rv_ref[rows, sl], rg_ref[rows, sl],
                                  s0_ref[s, h], h, t)
            o_ref[rows, sl] = o
            so_ref[s, h] = s_new


def _ret_sample(Bs, T, e, rq, rk, rv, rg, state_ret):
    tok = pl.BlockSpec((SEQ_PER_STEP * T, 512), lambda b: (b, 0))
    return pl.pallas_call(
        functools.partial(_ret_s_body, t=T),
        out_shape=[jax.ShapeDtypeStruct((Bs * T, 512), F32),
                   jax.ShapeDtypeStruct((Bs, RET_HEADS, RET_DH, RET_DH), F32)],
        grid=(Bs // SEQ_PER_STEP,),
        in_specs=[tok, tok, tok, tok,
                  pl.BlockSpec((None, SEQ_PER_STEP, RET_HEADS, RET_DH, RET_DH), lambda b: (e, b, 0, 0, 0))],
        out_specs=[tok, pl.BlockSpec((SEQ_PER_STEP, RET_HEADS, RET_DH, RET_DH), lambda b: (b, 0, 0, 0))],
        compiler_params=_cparams(1),
        name="ret_sample",
    )(rq, rk, rv, rg, state_ret)


def _even_out_body(x_ref, oa_ref, or_ref, gt_ref, gpost_ref, wa_ref, wr_ref, o_ref, *, g, rt):
    y = _dot(oa_ref[...].astype(BF16), wa_ref[...]) + _dot(or_ref[...].astype(BF16), wr_ref[...])
    o_ref[...] = x_ref[...] + _gated(_rms(y, gpost_ref[...]), gt_ref[...], g, rt)


def _even_out(t, x, o_a, o_r, mod, mcol, l, e, p):
    half = A_HEADS * A_DH
    return pl.pallas_call(
        functools.partial(_even_out_body, g=t.g, rt=t.rt),
        out_shape=jax.ShapeDtypeStruct((t.n_tok, D), F32),
        grid=t.grid,
        in_specs=[t.tok(D), t.tok(half), t.tok(half), t.mod(mcol), t.sel((None, None, 1, D), (l, 1, 0, 0)),
                  t.sel((None, half, D), (e, 0, 0)), t.sel((None, half, D), (e, 1, 0))],
        out_specs=t.tok(D),
        compiler_params=_cparams(2),
        name="even_out",
    )(x, o_a, o_r, mod, p["g_post"], p["w_out_even"], p["w_out_even"])


RET_CHUNK = 256


def _ffn_staged(x, mod_refs, gpre_ref, gpost_ref, wg_ref, wu_ref, wd_ref, between=()):
    sh_ref, sc_ref, gt_ref = mod_refs
    hf = _prenorm(x, gpre_ref[...], sh_ref[...], sc_ref[...], 1, x.shape[0]).astype(BF16)
    a, u = _dot(hf, wg_ref[...]), _dot(hf, wu_ref[...])
    for thunk in between:
        thunk()
    f = _dot((_silu(a) * u).astype(BF16), wd_ref[...])
    return x + 0.5 * (_rms(f, gpost_ref[...]) * gt_ref[...][0])


def _even_tail_body(rq_ref, rk_ref, rv_ref, rg_ref, x_ref, oa_ref, gt_ref, sh2_ref, sc2_ref, gt2_ref,
                    gpost_ref, gpre2_ref, gpost2_ref, wa_ref, wr_ref, wg_ref, wu_ref, wd_ref,
                    o_ref, so_ref, or_ref, s_ref, *, rows, nt, n_tiles):
    t = pl.program_id(0)
    j = lax.rem(t, nt)
    slot = lax.rem(t, 2)

    @pl.when(j == 0)
    def _():
        s_ref[...] = jnp.zeros_like(s_ref)

    def retention():
        for c in range(rows // RET_CHUNK):
            rs = slice(c * RET_CHUNK, (c + 1) * RET_CHUNK)
            for h in range(RET_HEADS):
                sl = slice(h * RET_DH, (h + 1) * RET_DH)
                o, s_new = _ret_chunk(rq_ref[rs, sl], rk_ref[rs, sl], rv_ref[rs, sl], rg_ref[rs, sl],
                                      s_ref[h], h, RET_CHUNK)
                or_ref[slot, rs, sl] = o
                s_ref[h] = s_new
        so_ref[0] = s_ref[...]

    def tail(between):
        o_r = or_ref[1 - slot]
        y = _dot(oa_ref[...].astype(BF16), wa_ref[...]) + _dot(o_r.astype(BF16), wr_ref[...])
        x1 = x_ref[...] + _rms(y, gpost_ref[...]) * gt_ref[...][0]
        o_ref[...] = _ffn_staged(x1, (sh2_ref, sc2_ref, gt2_ref), gpre2_ref, gpost2_ref, wg_ref, wu_ref, wd_ref,
                                 between=between)

    pl.when(t == 0)(retention)
    pl.when(jnp.logical_and(t > 0, t < n_tiles))(functools.partial(tail, (retention,)))
    pl.when(t == n_tiles)(functools.partial(tail, ()))


def _even_tail_prompt(B, L, x, o_a, rq, rk, rv, rg, mod, l, e, p, rows=512):
    assert rows == 2 * RET_CHUNK
    nt = L // rows
    n_tiles = B * nt
    half = A_HEADS * A_DH
    cur = lambda t: jnp.minimum(t, n_tiles - 1)
    prv = lambda t: jnp.maximum(t - 1, 0)
    sel = lambda block, idx: pl.BlockSpec(block, lambda t: idx, pipeline_mode=pl.Buffered(1))
    modc = lambda c: pl.BlockSpec((1, 1, D), lambda t: (prv(t) // nt, 0, c))
    gain = lambda n: sel((None, None, 1, D), (l, n, 0, 0))
    ret_in = pl.BlockSpec((rows, RET_HEADS * RET_DH), lambda t: (cur(t), 0))
    return pl.pallas_call(
        functools.partial(_even_tail_body, rows=rows, nt=nt, n_tiles=n_tiles),
        out_shape=[jax.ShapeDtypeStruct((B * L, D), F32),
                   jax.ShapeDtypeStruct((B, RET_HEADS, RET_DH, RET_DH), F32)],
        grid=(n_tiles + 1,),
        in_specs=[ret_in, ret_in, ret_in, ret_in,
                  pl.BlockSpec((rows, D), lambda t: (prv(t), 0)), pl.BlockSpec((rows, half), lambda t: (prv(t), 0)),
                  modc(5), modc(6), modc(7), modc(8), gain(1), gain(2), gain(2),
                  sel((None, half, D), (e, 0, 0)), sel((None, half, D), (e, 1, 0)),
                  sel((None, None, D, D_FF), (l, 1, 0, 0)), sel((None, None, D, D_FF), (l, 1, 0, 0)),
                  sel((None, None, D_FF, D), (l, 1, 0, 0))],
        out_specs=[pl.BlockSpec((rows, D), lambda t: (prv(t), 0)),
                   pl.BlockSpec((1, RET_HEADS, RET_DH, RET_DH), lambda t: (cur(t) // nt, 0, 0, 0))],
        scratch_shapes=[pltpu.VMEM((2, rows, RET_HEADS * RET_DH), F32), pltpu.VMEM((RET_HEADS, RET_DH, RET_DH), F32)],
        compiler_params=_cparams(1),
        name="even_tail_prompt",
    )(rq, rk, rv, rg, x, o_a, mod, mod, mod, mod, p["g_post"], p["g_pre"], p["g_post"],
      p["w_out_even"], p["w_out_even"], p["wg"], p["wu"], p["wd"])


def _softplus(x):
    return jnp.maximum(x, 0.0) + jnp.log1p(jnp.exp(-jnp.abs(x)))


def _scan_sublanes(a, b):
    sub = lax.broadcasted_iota(jnp.int32, (1, SUBLANES, a.shape[-1]), 1)
    d = 1
    while d < SUBLANES:
        keep = sub >= d
        a_sh = jnp.where(keep, pltpu.roll(a, d, 1), 1.0)
        b_sh = jnp.where(keep, pltpu.roll(b, d, 1), 0.0)
        b = b + a * b_sh
        a = a * a_sh
        d *= 2
    return a, b


def _rglru_gates(conv, wa_ref, wx_ref):
    convb = conv.astype(BF16)
    ra = jnp.concatenate([_dot(convb[:, n * RG_BW:(n + 1) * RG_BW], wa_ref[n]) for n in range(RG_BLOCKS)], axis=-1)
    rx = jnp.concatenate([_dot(convb[:, n * RG_BW:(n + 1) * RG_BW], wx_ref[n]) for n in range(RG_BLOCKS)], axis=-1)
    return ra, rx


def _odd_s_body(x_ref, sh_ref, sc_ref, gt_ref, gpre_ref, gpost_ref, win_ref, cw_ref, cb_ref,
                wa_ref, ba_ref, wx_ref, bx_ref, lam_ref, wout_ref, prev_ref, h0_ref,
                o_ref, xb_out_ref, hs_out_ref, *, g, rt):
    rows = g * rt
    x = x_ref[...]
    h = _prenorm(x, gpre_ref[...], sh_ref[...], sc_ref[...], g, rt).astype(BF16)
    proj = _dot(h, win_ref[...])
    gate_br = proj[:, :D]
    xb = proj[:, D:]
    pos = jnp.bitwise_and(lax.broadcasted_iota(jnp.int32, (rows, 1), 0), rt - 1)
    cw = cw_ref[...]
    prev = prev_ref[...]
    conv = xb * cw[CONV_W - 1:CONV_W] + cb_ref[...]
    for jj in range(CONV_W - 1):
        d = CONV_W - 1 - jj
        tap = jnp.where(pos >= d, pltpu.roll(xb, d, 0), pltpu.roll(prev, rows - (SUBLANES - d), 0))
        conv = conv + tap * cw[jj:jj + 1]
    xb_out_ref[...] = xb
    ra, rx = _rglru_gates(conv, wa_ref, wx_ref)
    r = _sigmoid(ra + ba_ref[...])
    ig = _sigmoid(rx + bx_ref[...])
    a = jnp.exp((-RG_C) * r * _softplus(-lam_ref[...]))
    b = jnp.sqrt(1.0 - a * a) * (ig * conv)
    b = b + a * h0_ref[...]
    n_grp = rows // SUBLANES
    hs = _scan_sublanes(a.reshape(n_grp, SUBLANES, D), b.reshape(n_grp, SUBLANES, D))[1].reshape(rows, D)
    hs_out_ref[...] = hs
    y = (hs * _gelu_tanh(gate_br)).astype(BF16)
    out = _dot(y, wout_ref[...])
    o_ref[...] = x + _gated(_rms(out, gpost_ref[...]), gt_ref[...], g, rt)


def _odd_sample(t, x, mod, mcol, l, e, p, prev, h0):
    vec = t.sel((None, 1, D), (e, 0, 0))
    rgw = t.sel((None, RG_BLOCKS, RG_BW, RG_BW), (e, 0, 0, 0))
    return pl.pallas_call(
        functools.partial(_odd_s_body, g=t.g, rt=t.rt),
        out_shape=[jax.ShapeDtypeStruct((t.n_tok, D), F32)] * 3,
        grid=t.grid,
        in_specs=[t.tok(D), t.mod(mcol), t.mod(mcol + 1), t.mod(mcol + 2),
                  t.sel((None, None, 1, D), (l, 1, 0, 0)), t.sel((None, None, 1, D), (l, 1, 0, 0)),
                  t.sel((None, D, 2 * D), (e, 0, 0)), t.sel((None, CONV_W, D), (e, 0, 0)), vec,
                  rgw, vec, rgw, vec, vec, t.sel((None, D, D), (e, 0, 0)), t.tok(D), t.tok(D)],
        out_specs=[t.tok(D)] * 3,
        compiler_params=_cparams(2),
        name="odd_sample",
    )(x, mod, mod, mod, p["g_pre"], p["g_post"], p["w_in_odd"], p["conv_w"], p["conv_b"],
      p["w_rg_a"], p["b_rg_a"], p["w_rg_x"], p["b_rg_x"], p["rg_lambda"], p["w_out_odd"], prev, h0)


def _odd_ffn_body(x_ref, sh_ref, sc_ref, gt_ref, sh2_ref, sc2_ref, gt2_ref,
                  gpre_ref, gpost_ref, gpre2_ref, gpost2_ref, win_ref, cw_ref, cb_ref,
                  wa_ref, ba_ref, wx_ref, bx_ref, lam_ref, wout_ref, wg_ref, wu_ref, wd_ref,
                  o_ref, tail_out_ref, h_out_ref, x1_ref, tail_ref, hcar_ref, *, rows, nt, n_tiles):
    t = pl.program_id(0)
    j = lax.rem(t, nt)
    slot = lax.rem(t, 2)
    n_grp = rows // SUBLANES

    @pl.when(j == 0)
    def _():
        tail_ref[...] = jnp.zeros_like(tail_ref)
        hcar_ref[...] = jnp.zeros_like(hcar_ref)

    def step(mixer, ffn):
        if mixer:
            x = x_ref[...]
            h = _prenorm(x, gpre_ref[...], sh_ref[...], sc_ref[...], 1, rows).astype(BF16)
            proj = _dot(h, win_ref[...])
            gate_br = proj[:, :D]
            xb = proj[:, D:]
            row = lax.broadcasted_iota(jnp.int32, (rows, 1), 0)
            cw = cw_ref[...]
            xext = jnp.concatenate([tail_ref[...], xb], axis=0)
            conv = xb * cw[CONV_W - 1:CONV_W] + cb_ref[...]
            for jj in range(CONV_W - 1):
                conv = conv + pltpu.roll(xext, CONV_W - 1 - jj, 0)[SUBLANES:] * cw[jj:jj + 1]
            tail_ref[...] = xb[rows - SUBLANES:]
            tail_out_ref[0] = xb[rows - SUBLANES:]
            ra, rx = _rglru_gates(conv, wa_ref, wx_ref)
        if ffn:
            x1p = x1_ref[1 - slot]
            hf = _prenorm(x1p, gpre2_ref[...], sh2_ref[...], sc2_ref[...], 1, rows).astype(BF16)
            ff_a, ff_u = _dot(hf, wg_ref[...]), _dot(hf, wu_ref[...])
        if mixer:
            r = _sigmoid(ra + ba_ref[...])
            ig = _sigmoid(rx + bx_ref[...])
            a = jnp.exp((-RG_C) * r * _softplus(-lam_ref[...]))
            mult = jnp.sqrt(1.0 - a * a)
            mult = jnp.where(jnp.logical_and(j == 0, row == 0), 1.0, mult)
            b = mult * (ig * conv)
            a, b = _scan_sublanes(a.reshape(n_grp, SUBLANES, D), b.reshape(n_grp, SUBLANES, D))
            carry = hcar_ref[0:1]
            parts = []
            for g8 in range(n_grp):
                part = b[g8] + a[g8] * carry
                carry = part[SUBLANES - 1:SUBLANES]
                parts.append(part)
            hcar_ref[0:1] = carry
            h_out_ref[0] = parts[-1]
            y = (jnp.concatenate(parts, axis=0) * _gelu_tanh(gate_br)).astype(BF16)
        if ffn:
            act = (_silu(ff_a) * ff_u).astype(BF16)
        if mixer:
            x1_ref[slot] = x + _rms(_dot(y, wout_ref[...]), gpost_ref[...]) * gt_ref[...][0]
        if ffn:
            f = _dot(act, wd_ref[...])
            o_ref[...] = x1p + 0.5 * (_rms(f, gpost2_ref[...]) * gt2_ref[...][0])

    pl.when(t == 0)(functools.partial(step, True, False))
    pl.when(jnp.logical_and(t > 0, t < n_tiles))(functools.partial(step, True, True))
    pl.when(t == n_tiles)(functools.partial(step, False, True))


def _odd_ffn_prompt(B, L, x, mod, l, e, p, rows=512):
    nt = L // rows
    n_tiles = B * nt
    cur = lambda t: jnp.minimum(t, n_tiles - 1)
    prv = lambda t: jnp.maximum(t - 1, 0)
    sel = lambda block, idx: pl.BlockSpec(block, lambda t: idx, pipeline_mode=pl.Buffered(1))
    modc = lambda c, which: pl.BlockSpec((1, 1, D), lambda t: (which(t) // nt, 0, c))
    gain = lambda n: sel((None, None, 1, D), (l, n, 0, 0))
    vec = sel((None, 1, D), (e, 0, 0))
    rgw = sel((None, RG_BLOCKS, RG_BW, RG_BW), (e, 0, 0, 0))
    last = pl.BlockSpec((1, SUBLANES, D), lambda t: (cur(t) // nt, 0, 0))
    return pl.pallas_call(
        functools.partial(_odd_ffn_body, rows=rows, nt=nt, n_tiles=n_tiles),
        out_shape=[jax.ShapeDtypeStruct((B * L, D), F32),
                   jax.ShapeDtypeStruct((B, SUBLANES, D), F32),
                   jax.ShapeDtypeStruct((B, SUBLANES, D), F32)],
        grid=(n_tiles + 1,),
        in_specs=[pl.BlockSpec((rows, D), lambda t: (cur(t), 0)),
                  modc(3, cur), modc(4, cur), modc(5, cur), modc(6, prv), modc(7, prv), modc(8, prv),
                  gain(1), gain(1), gain(2), gain(2),
                  sel((None, D, 2 * D), (e, 0, 0)), sel((None, CONV_W, D), (e, 0, 0)), vec,
                  rgw, vec, rgw, vec, vec, sel((None, D, D), (e, 0, 0)),
                  sel((None, None, D, D_FF), (l, 1, 0, 0)), sel((None, None, D, D_FF), (l, 1, 0, 0)),
                  sel((None, None, D_FF, D), (l, 1, 0, 0))],
        out_specs=[pl.BlockSpec((rows, D), lambda t: (prv(t), 0)), last, last],
        scratch_shapes=[pltpu.VMEM((2, rows, D), F32), pltpu.VMEM((SUBLANES, D), F32), pltpu.VMEM((SUBLANES, D), F32)],
        compiler_params=_cparams(1),
        name="odd_ffn_prompt",
    )(x, mod, mod, mod, mod, mod, mod, p["g_pre"], p["g_post"], p["g_pre"], p["g_post"],
      p["w_in_odd"], p["conv_w"], p["conv_b"], p["w_rg_a"], p["b_rg_a"], p["w_rg_x"], p["b_rg_x"], p["rg_lambda"],
      p["w_out_odd"], p["wg"], p["wu"], p["wd"])


def _rope_tables(pos):
    posf = pos.astype(F32)[:, None]
    inv_a = jnp.power(jnp.float32(ROPE_THETA), -jnp.arange(0, ROT_DIM, 2, dtype=F32) / ROT_DIM)
    ang = posf * inv_a[None, :]
    cos, sin = jnp.cos(ang), jnp.sin(ang)
    n = pos.shape[0]
    half = ROT_DIM // 2
    rest = A_DH - ROT_DIM
    one, zero = jnp.ones((n, rest), F32), jnp.zeros((n, rest), F32)
    zh = jnp.zeros((n, half), F32)
    cos_h = jnp.concatenate([cos, cos, one], axis=1)
    s1_h = jnp.concatenate([-sin, zh, zero], axis=1)
    s2_h = jnp.concatenate([zh, sin, zero], axis=1)
    both = lambda a: jnp.concatenate([a, a], axis=1)
    wi_scale = jnp.full((n, A_DH), IDX_HEADS ** -0.5, F32)
    z64 = jnp.zeros((n, A_DH), F32)
    rope_a = jnp.stack([both(cos_h), both(s1_h), both(s2_h),
                        jnp.concatenate([cos_h, wi_scale], axis=1),
                        jnp.concatenate([s1_h, z64], axis=1),
                        jnp.concatenate([s2_h, z64], axis=1)])
    inv_r = jnp.power(jnp.float32(RET_THETA), -jnp.linspace(0.0, 1.0, RET_DH // 2, dtype=F32))
    ang_r = posf * inv_r[None, :]
    cr, sr = jnp.cos(ang_r), jnp.sin(ang_r)
    rope_r = jnp.stack([jnp.concatenate([cr, cr], axis=1), jnp.concatenate([-sr, sr], axis=1)])
    return rope_a, rope_r


def _pack_w_in_even(w):
    o = np.cumsum((0, 512, 128, 128, 256, IDX_HEADS, IDX_DIM, 512, 512, 512, 512))
    pad = jnp.zeros(w.shape[:2] + (LANES - IDX_DIM - IDX_HEADS,), w.dtype)
    return jnp.concatenate([w[..., o[0]:o[4]], w[..., o[5]:o[6]], w[..., o[4]:o[5]], pad, w[..., o[6]:o[10]]],
                           axis=-1).astype(BF16)


def kernel(x_prompt, x_sample, cache_k, cache_v, cache_kidx, state_ret, state_conv, state_rglru, page_table,
           c_prompt, c_sample, w_ada, b_ada, g_pre, g_post, w_ffn_gate, w_ffn_up, w_ffn_down,
           w_in_even, w_out_even, w_in_odd, w_out_odd, conv_w, conv_b, w_rg_a, b_rg_a, w_rg_x, b_rg_x, rg_lambda):
    B, L, _ = x_prompt.shape
    Bs, T, _ = x_sample.shape
    n_pages = page_table.shape[1]
    past = n_pages * cache_k.shape[2]
    n_phys = cache_k.shape[1]
    assert cache_k.shape[2] == PAGE and T == SUBLANES

    tp = _Tiling(B, L, 1, 512)
    ts = _Tiling(Bs, T, 64, T)
    ts_odd = _Tiling(Bs, T, 32, T)

    vec3 = lambda a: a.reshape(a.shape[0], 1, a.shape[1])
    p = {
        "g_pre": g_pre.reshape(DEPTH, 3, 1, D), "g_post": g_post.reshape(DEPTH, 3, 1, D),
        "wg": w_ffn_gate.astype(BF16), "wu": w_ffn_up.astype(BF16), "wd": w_ffn_down.astype(BF16),
        "w_in_even": _pack_w_in_even(w_in_even), "w_out_even": w_out_even.astype(BF16),
        "w_in_odd": w_in_odd.astype(BF16), "w_out_odd": w_out_odd.astype(BF16),
        "conv_w": conv_w, "conv_b": vec3(conv_b),
        "w_rg_a": w_rg_a.astype(BF16), "b_rg_a": vec3(b_rg_a),
        "w_rg_x": w_rg_x.astype(BF16), "b_rg_x": vec3(b_rg_x), "rg_lambda": vec3(rg_lambda),
    }
    k_t = jnp.transpose(cache_k, (0, 1, 3, 4, 2)).reshape(cache_k.shape[0], n_phys, LANES, PAGE)
    v_t = jnp.transpose(cache_v, (0, 1, 3, 4, 2)).reshape(cache_v.shape[0], n_phys, LANES, PAGE)
    kidx_t = jnp.transpose(cache_kidx, (0, 1, 3, 2))

    mod_all = _ada(jnp.concatenate([c_prompt, c_sample], axis=0), w_ada, b_ada)
    mod_p = mod_all[:, :B].reshape(DEPTH, B, 1, N_MOD * D)
    mod_s = mod_all[:, B:].reshape(DEPTH, Bs, 1, N_MOD * D)

    rope_a_p, rope_r_p = _rope_tables(jnp.arange(L, dtype=jnp.int32))
    rope_a_s, rope_r_s = _rope_tables(jnp.tile(past + jnp.arange(T, dtype=jnp.int32), ts.g))
    rope_map_p = lambda i, j: (0, j, 0)
    rope_map_s = lambda i, j: (0, 0, 0)

    xp = x_prompt.reshape(B * L, D)
    xs = x_sample.reshape(Bs * T, D)
    ks, vs, kis, rets, convs, hs = ([[], []] for _ in range(6))

    for l in range(DEPTH):
        e = l // 2
        groups = ((0, tp, xp, mod_p[l]), (1, ts, xs, mod_s[l]))
        new_x = []
        for gi, t, x, mod in groups:
            x = _ffn(t, x, mod, 0, l, 0, 0, p)
            if l % 2 == 0:
                if gi == 0:
                    q, kt, v, qi, kw, kit, rq, rk, rv, rg = _even_in(
                        t, True, x, mod, 3, l, e, p, rope_a_p, rope_r_p, rope_map_p)
                    o_a = _dsa_prompt(B, L, q, qi, kw, kt, v)
                    ks[0].append(jnp.transpose(kt.reshape(B, A_KV, A_DH, L), (0, 3, 1, 2)))
                    vs[0].append(v.reshape(B, L, A_KV, A_DH))
                    kis[0].append(jnp.transpose(kit, (0, 2, 1)))
                    x, s_new = _even_tail_prompt(B, L, x, o_a, rq, rk, rv, rg, mod, l, e, p)
                    rets[0].append(s_new)
                    new_x.append(x)
                    continue
                else:
                    q, k, v, qi, kw, rq, rk, rv, rg = _even_in(
                        t, False, x, mod, 3, l, e, p, rope_a_s, rope_r_s, rope_map_s)
                    scores = _dsa_s_scores(Bs, T, e, page_table, qi, kw, kidx_t)
                    bias = _dsa_s_select(Bs * T, T, past, scores)
                    o_a = _dsa_s_attn(Bs, T, e, page_table, q, k, v, bias, k_t, v_t)
                    o_r, s_new = _ret_sample(Bs, T, e, rq, rk, rv, rg, state_ret)
                    ks[1].append(k.reshape(Bs, T, A_KV, A_DH))
                    vs[1].append(v.reshape(Bs, T, A_KV, A_DH))
                    kis[1].append(kw[:, :IDX_DIM].reshape(Bs, T, IDX_DIM))
                rets[gi].append(s_new)
                x = _even_out(t, x, o_a, o_r, mod, 5, l, e, p)
            else:
                if gi == 0:
                    x, tail, hlast = _odd_ffn_prompt(B, L, x, mod, l, e, p)
                    convs[0].append(tail[:, SUBLANES - (CONV_W - 1):])
                    hs[0].append(hlast[:, SUBLANES - 1])
                    new_x.append(x)
                    continue
                else:
                    prev = jnp.pad(state_conv[e], ((0, 0), (SUBLANES - (CONV_W - 1), 0), (0, 0))).reshape(Bs * T, D)
                    h0 = jnp.pad(state_rglru[e][:, None, :], ((0, 0), (0, T - 1), (0, 0))).reshape(Bs * T, D)
                    x, xb, hseq = _odd_sample(ts_odd, x, mod, 3, l, e, p, prev, h0)
                    convs[1].append(xb.reshape(Bs, T, D)[:, T - (CONV_W - 1):])
                    hs[1].append(hseq.reshape(Bs, T, D)[:, T - 1])
            x = _ffn(t, x, mod, 6, l, 1, 2, p)
            new_x.append(x)
        xp, xs = new_x

    st = lambda lists, gi: jnp.stack(lists[gi])
    return (xp.reshape(B, L, D), xs.reshape(Bs, T, D),
            st(ks, 0), st(vs, 0), st(kis, 0), st(rets, 0), st(convs, 0), st(hs, 0),
            st(ks, 1), st(vs, 1), st(kis, 1), st(rets, 1), st(convs, 1), st(hs, 1))
```

```python
import functools
import math

import jax
import jax.numpy as jnp
import numpy as np
from jax import lax
from jax.experimental import pallas as pl
from jax.experimental.pallas import tpu as pltpu

F32 = jnp.float32
BF16 = jnp.bfloat16

D = 1024
DEPTH = 4
N_MOD = 9
A_HEADS = 8
A_KV = 2
A_DH = 64
ROT_DIM = 16
ROPE_THETA = 500000.0
IDX_HEADS = 4
IDX_DIM = 64
TOPK_MAX = 256
RET_HEADS = 4
RET_DH = 128
RET_THETA = 10000.0
RG_BLOCKS = 8
RG_BW = 128
CONV_W = 4
RG_C = 8.0
D_FF = 2816
EPS = 1e-6
P_IN = 3200
PAGE = 128

LANES = 128
SUBLANES = 8
VMEM_LIMIT = 56 * 1024 * 1024

NEG_INF = float("-inf")


def _cparams(n_axes, vmem=VMEM_LIMIT):
    return pltpu.CompilerParams(dimension_semantics=("arbitrary",) * n_axes, vmem_limit_bytes=vmem)


def _dot(a, b):
    return jnp.dot(a, b, preferred_element_type=F32)


def _dot_nt(a, b):
    return lax.dot_general(a, b, (((1,), (1,)), ((), ())), preferred_element_type=F32)


def _dot_tn(a, b):
    return lax.dot_general(a, b, (((0,), (0,)), ((), ())), preferred_element_type=F32)


def _sigmoid(x):
    return 1.0 / (1.0 + jnp.exp(-x))


def _silu(x):
    return x * _sigmoid(x)


def _gelu_tanh(x):
    return 0.5 * x * (1.0 + jnp.tanh(math.sqrt(2.0 / math.pi) * (x + 0.044715 * (x * x * x))))


def _rms(x, g=None):
    y = x * lax.rsqrt(jnp.mean(x * x, axis=-1, keepdims=True) + EPS)
    return y if g is None else y * g


def _prenorm(x, g_pre, shift, scale, g, rt):
    h = _rms(x, g_pre)
    if g == 1:
        return h * (1.0 + scale[0]) + shift[0]
    h3 = h.reshape(g, rt, D)
    return (h3 * (1.0 + scale) + shift).reshape(g * rt, D)


def _gated(y, gate, g, rt):
    if g == 1:
        return y * gate[0]
    return (y.reshape(g, rt, D) * gate).reshape(g * rt, D)


class _Tiling:
    def __init__(self, G, R, g, rt):
        assert G % g == 0 and R % rt == 0 and (g == 1 or rt == R)
        self.G, self.R, self.g, self.rt = G, R, g, rt
        self.nr = R // rt
        self.rows = g * rt
        self.grid = (G // g, self.nr)
        self.n_tok = G * R

    def tok(self, width):
        nr = self.nr
        return pl.BlockSpec((self.rows, width), lambda i, j: (i * nr + j, 0))

    def mod(self, c):
        return pl.BlockSpec((self.g, 1, D), lambda i, j: (i, 0, c))

    def sel(self, block, idx):
        return pl.BlockSpec(block, lambda i, j: idx, pipeline_mode=pl.Buffered(1))


def _ada_body(c_ref, w_ref, b_ref, o_ref):
    c = _silu(c_ref[...]).astype(BF16)
    o_ref[0] = _dot(c, w_ref[0].astype(BF16)) + b_ref[0]


def _ada(c_all, w_ada, b_ada):
    n = c_all.shape[0]
    return pl.pallas_call(
        _ada_body,
        out_shape=jax.ShapeDtypeStruct((DEPTH, n, N_MOD * D), F32),
        grid=(DEPTH, N_MOD),
        in_specs=[pl.BlockSpec((n, D), lambda l, j: (0, 0)),
                  pl.BlockSpec((1, D, D), lambda l, j: (l, 0, j)),
                  pl.BlockSpec((1, 1, D), lambda l, j: (l, 0, j))],
        out_specs=pl.BlockSpec((1, n, D), lambda l, j: (l, 0, j)),
        compiler_params=_cparams(2),
        name="ada",
    )(c_all, w_ada, b_ada.reshape(DEPTH, 1, N_MOD * D))


def _ffn_body(x_ref, sh_ref, sc_ref, gt_ref, gpre_ref, gpost_ref, wg_ref, wu_ref, wd_ref, o_ref, *, g, rt):
    x = x_ref[...]
    h = _prenorm(x, gpre_ref[...], sh_ref[...], sc_ref[...], g, rt).astype(BF16)
    a = _dot(h, wg_ref[...])
    u = _dot(h, wu_ref[...])
    act = (_silu(a) * u).astype(BF16)
    f = _dot(act, wd_ref[...])
    y = _rms(f, gpost_ref[...])
    o_ref[...] = x + 0.5 * _gated(y, gt_ref[...], g, rt)


def _ffn(t, x, mod, mcol, l, s, n, p):
    return pl.pallas_call(
        functools.partial(_ffn_body, g=t.g, rt=t.rt),
        out_shape=jax.ShapeDtypeStruct((t.n_tok, D), F32),
        grid=t.grid,
        in_specs=[t.tok(D), t.mod(mcol), t.mod(mcol + 1), t.mod(mcol + 2),
                  t.sel((None, None, 1, D), (l, n, 0, 0)), t.sel((None, None, 1, D), (l, n, 0, 0)),
                  t.sel((None, None, D, D_FF), (l, s, 0, 0)), t.sel((None, None, D, D_FF), (l, s, 0, 0)),
                  t.sel((None, None, D_FF, D), (l, s, 0, 0))],
        out_specs=t.tok(D),
        compiler_params=_cparams(2),
        name="ffn",
    )(x, mod, mod, mod, p["g_pre"], p["g_post"], p["wg"], p["wu"], p["wd"])


def _even_in_body(*refs, g, rt, prompt):
    x_ref, sh_ref, sc_ref, gpre_ref, w_ref, ra_ref, rr_ref = refs[:7]
    q_ref, k_ref, v_ref, qi_ref, kw_ref = refs[7:12]
    if prompt:
        kit_ref, rq_ref, rk_ref, rv_ref, rg_ref = refs[12:]
    else:
        rq_ref, rk_ref, rv_ref, rg_ref = refs[12:]
    h = _prenorm(x_ref[...], gpre_ref[...], sh_ref[...], sc_ref[...], g, rt).astype(BF16)
    proj = _dot(h, w_ref[...])

    def sec(c):
        return proj[:, c * LANES:(c + 1) * LANES]

    def rope_a(t, v):
        return (t * ra_ref[3 * v] + pltpu.roll(t, LANES - ROT_DIM // 2, 1) * ra_ref[3 * v + 1]
                + pltpu.roll(t, ROT_DIM // 2, 1) * ra_ref[3 * v + 2])

    def rope_r(t):
        return t * rr_ref[0] + pltpu.roll(t, RET_DH // 2, 1) * rr_ref[1]

    for c in range(4):
        q_ref[:, c * LANES:(c + 1) * LANES] = rope_a(sec(c), 0) * (A_DH ** -0.5)
    k = rope_a(sec(4), 0)
    v_ref[...] = sec(5)
    for c in range(2):
        qi_ref[:, c * LANES:(c + 1) * LANES] = rope_a(sec(6 + c), 0) * (IDX_DIM ** -0.5)
    kw = rope_a(sec(8), 1)
    kw_ref[...] = kw
    if prompt:
        k_ref[...] = k.T
        kit_ref[...] = kw.T[0:IDX_DIM]
    else:
        k_ref[...] = k
    for c in range(4):
        rq_ref[:, c * LANES:(c + 1) * LANES] = rope_r(sec(9 + c))
        rk_ref[:, c * LANES:(c + 1) * LANES] = rope_r(sec(13 + c) * (RET_DH ** -0.5))
        rv_ref[:, c * LANES:(c + 1) * LANES] = sec(17 + c)
        rg_ref[:, c * LANES:(c + 1) * LANES] = sec(21 + c)


def _even_in(t, prompt, x, mod, mcol, l, e, p, rope_a, rope_r, rope_map):
    tokw = lambda w: (jax.ShapeDtypeStruct((t.n_tok, w), F32), t.tok(w))
    seq_t = lambda w: (jax.ShapeDtypeStruct((t.G, w, t.R), F32),
                       pl.BlockSpec((None, w, t.rt), lambda i, j: (i, 0, j)))
    outs = [tokw(512), seq_t(LANES) if prompt else tokw(LANES), tokw(LANES), tokw(256), tokw(LANES)]
    if prompt:
        outs.append(seq_t(IDX_DIM))
    outs += [tokw(512)] * 4
    return pl.pallas_call(
        functools.partial(_even_in_body, g=t.g, rt=t.rt, prompt=prompt),
        out_shape=[o[0] for o in outs],
        grid=t.grid,
        in_specs=[t.tok(D), t.mod(mcol), t.mod(mcol + 1), t.sel((None, None, 1, D), (l, 1, 0, 0)),
                  t.sel((None, D, P_IN), (e, 0, 0)),
                  pl.BlockSpec((6, t.rows, LANES), rope_map),
                  pl.BlockSpec((2, t.rows, LANES), rope_map)],
        out_specs=[o[1] for o in outs],
        compiler_params=_cparams(2),
        name="even_in_p" if prompt else "even_in_s",
    )(x, mod, mod, p["g_pre"], p["w_in_even"], rope_a, rope_r)


def _sortable(s):
    s = jnp.where(s == 0.0, 0.0, s)
    b = lax.bitcast_convert_type(s, jnp.int32)
    return b ^ ((b >> 31) & 0x7FFFFFFF)


def _topk_bias_t(scores_t, visible_t, k):
    s_len, n_q = scores_t.shape
    key = _sortable(scores_t)
    fold = 8 * SUBLANES
    assert s_len % fold == 0 and s_len % LANES == 0
    i16 = jnp.int16

    def count(mask):
        part = jnp.sum(jnp.where(mask, 1.0, 0.0).reshape(s_len // fold, fold, n_q), axis=0)
        return jnp.sum(part, axis=0, keepdims=True)

    def count16(mask):
        ind = jnp.where(mask, i16(1), i16(0))
        part = ind[0:LANES]
        for r in range(1, s_len // LANES):
            part = part + ind[r * LANES:(r + 1) * LANES]
        return jnp.sum(part.astype(jnp.int32).astype(F32), axis=0, keepdims=True)

    def kth16(vals, kk):
        def step(it, t):
            cand = t + jnp.left_shift(jnp.int32(1), 15 - it)
            return jnp.where(count16(vals >= cand.astype(i16)) >= kk, cand, t)
        return lax.fori_loop(0, 16, step, jnp.full((1, n_q), -(2 ** 15), jnp.int32))

    hi = (key >> 16).astype(i16)
    t_hi = kth16(hi, k)
    t_hi16 = t_hi.astype(i16)
    lo = jnp.where(hi == t_hi16, ((key & 0xFFFF) - 2 ** 15).astype(i16), i16(-(2 ** 15)))
    t_lo = kth16(lo, k - count16(hi > t_hi16))
    thr = jnp.left_shift(t_hi, 16) + (t_lo + 2 ** 15)
    ge = key >= thr
    cnt_ge = count(ge)

    def no_ties():
        return jnp.where(visible_t, jnp.where(ge, 0.0, NEG_INF), NEG_INF)

    def ties():
        need = k - count(key > thr)
        tri = jnp.where(lax.broadcasted_iota(jnp.int32, (LANES, LANES), 0)
                        >= lax.broadcasted_iota(jnp.int32, (LANES, LANES), 1), 1.0, 0.0).astype(BF16)
        off = jnp.zeros((1, n_q), F32)
        pieces = []
        for c in range(s_len // LANES):
            sl = slice(c * LANES, (c + 1) * LANES)
            kc = key[sl, :]
            eq = kc == thr
            pc = _dot(tri, jnp.where(eq, 1.0, 0.0).astype(BF16)) + off
            off = pc[LANES - 1:LANES, :]
            tie_ok = jnp.where(eq, jnp.where(pc <= need, 0.0, NEG_INF), NEG_INF)
            b = jnp.where(kc > thr, 0.0, tie_ok)
            pieces.append(jnp.where(visible_t[sl, :], b, NEG_INF))
        return jnp.concatenate(pieces, axis=0)

    return lax.cond(jnp.max(cnt_ge) > k, ties, no_ties)


def _softmax(logits):
    m = jnp.max(logits, axis=-1, keepdims=True)
    p = jnp.exp(logits - m)
    return p, jnp.sum(p, axis=-1, keepdims=True)


def _dsa_p_body(q_ref, qi_ref, kw_ref, kt_ref, v_ref, o_ref, *, tq, topk):
    s_len = q_ref.shape[0]
    rep = A_HEADS // A_KV
    kt = kt_ref[...].astype(BF16)
    ki = kw_ref[:, 0:IDX_DIM].astype(BF16)
    v = v_ref[...].astype(BF16)
    for i in range(s_len // tq):
        rows = slice(i * tq, (i + 1) * tq)
        s_vis = (i + 1) * tq
        qit = qi_ref[rows, :].T.astype(BF16)
        wit = kw_ref[rows, :].T
        sc = jnp.zeros((s_vis, tq), F32)
        for h in range(IDX_HEADS):
            raw = _dot(ki[:s_vis], qit[h * IDX_DIM:(h + 1) * IDX_DIM, :])
            sc = sc + jnp.maximum(raw, 0.0) * wit[IDX_DIM + h:IDX_DIM + h + 1, :]
        kpos = lax.broadcasted_iota(jnp.int32, (s_vis, 1), 0)
        qpos = i * tq + lax.broadcasted_iota(jnp.int32, (1, tq), 1)
        visible = kpos <= qpos
        bias = _topk_bias_t(jnp.where(visible, sc, NEG_INF), visible, topk).T
        q = q_ref[rows, :].astype(BF16)
        for hp in range(A_HEADS // 2):
            outs = []
            for h in (2 * hp, 2 * hp + 1):
                gsl = slice((h // rep) * A_DH, (h // rep + 1) * A_DH)
                p, l = _softmax(_dot(q[:, h * A_DH:(h + 1) * A_DH], kt[gsl, :s_vis]) + bias)
                outs.append(_dot(p.astype(BF16), v[:s_vis, gsl]) / l)
            o_ref[rows, hp * LANES:(hp + 1) * LANES] = jnp.concatenate(outs, axis=-1)


def _dsa_prompt(B, L, q, qi, kw, kt, v, tq=256):
    tok = lambda w: pl.BlockSpec((L, w), lambda b: (b, 0))
    seq_t = lambda w: pl.BlockSpec((None, w, L), lambda b: (b, 0, 0))
    return pl.pallas_call(
        functools.partial(_dsa_p_body, tq=tq, topk=min(TOPK_MAX, L // 4)),
        out_shape=jax.ShapeDtypeStruct((B * L, A_HEADS * A_DH), F32),
        grid=(B,),
        in_specs=[tok(512), tok(256), tok(LANES), seq_t(LANES), tok(LANES)],
        out_specs=tok(512),
        compiler_params=_cparams(1),
        name="dsa_prompt",
    )(q, qi, kw, kt, v)


SEQ_PER_STEP = 4


def _page_gather(pt_ref, pool_ref, buf_ref, sem_ref, e, n_pages):
    b = pl.program_id(0)

    def copies(step, slot):
        return [pltpu.make_async_copy(pool_ref.at[e, pt_ref[step * SEQ_PER_STEP + s, p]],
                                      buf_ref.at[slot, s, :, pl.ds(p * PAGE, PAGE)],
                                      sem_ref.at[slot])
                for s in range(SEQ_PER_STEP) for p in range(n_pages)]

    @pl.when(b == 0)
    def _():
        for c in copies(0, 0):
            c.start()

    @pl.when(b + 1 < pl.num_programs(0))
    def _():
        for c in copies(b + 1, (b + 1) % 2):
            c.start()

    slot = b % 2
    for c in copies(b, slot):
        c.wait()
    return slot


def _dsa_s_scores_body(pt_ref, qi_ref, kw_ref, pool_ref, o_ref, buf_ref, sem_ref, *, e, n_pages, t):
    slot = _page_gather(pt_ref, pool_ref, buf_ref, sem_ref, e, n_pages)
    past = n_pages * PAGE
    lane = lax.broadcasted_iota(jnp.int32, (t, PAGE), 1)
    row = lax.broadcasted_iota(jnp.int32, (t, PAGE), 0)
    for s in range(SEQ_PER_STEP):
        rows = slice(s * t, (s + 1) * t)
        qi = qi_ref[rows, :]
        qs = jnp.concatenate([qi[:, h * IDX_DIM:(h + 1) * IDX_DIM] for h in range(IDX_HEADS)], axis=0).astype(BF16)
        kw = kw_ref[rows, :]
        wi = kw[:, IDX_DIM:LANES]
        ki_new = jnp.concatenate([kw[:, 0:IDX_DIM], jnp.zeros((PAGE - t, IDX_DIM), F32)], axis=0).astype(BF16)
        kit = buf_ref[slot, s].astype(BF16)

        def weigh(raw):
            raw = jnp.maximum(raw, 0.0)
            sc = raw[0:t] * wi[:, 0:1]
            for h in range(1, IDX_HEADS):
                sc = sc + raw[h * t:(h + 1) * t] * wi[:, h:h + 1]
            return sc

        o_ref[rows, :past] = weigh(_dot(qs, kit))
        o_ref[rows, past:] = jnp.where(lane <= row, weigh(_dot_nt(qs, ki_new)), NEG_INF)


def _dsa_s_scores(Bs, T, e, page_table, qi, kw, kidx_t):
    n_pages = page_table.shape[1]
    tok = lambda w: pl.BlockSpec((SEQ_PER_STEP * T, w), lambda b, pt: (b, 0))
    s_pad = (n_pages + 1) * PAGE
    return pl.pallas_call(
        functools.partial(_dsa_s_scores_body, e=e, n_pages=n_pages, t=T),
        out_shape=jax.ShapeDtypeStruct((Bs * T, s_pad), F32),
        grid_spec=pltpu.PrefetchScalarGridSpec(
            num_scalar_prefetch=1, grid=(Bs // SEQ_PER_STEP,),
            in_specs=[tok(256), tok(LANES), pl.BlockSpec(memory_space=pl.ANY)],
            out_specs=tok(s_pad),
            scratch_shapes=[pltpu.VMEM((2, SEQ_PER_STEP, IDX_DIM, n_pages * PAGE), F32),
                            pltpu.SemaphoreType.DMA((2,))]),
        compiler_params=_cparams(1),
        name="dsa_s_scores",
    )(page_table, qi, kw, kidx_t)


def _dsa_s_select_body(sc_ref, o_ref, *, t, past, topk):
    rows, s_pad = sc_ref.shape
    tpos = jnp.bitwise_and(lax.broadcasted_iota(jnp.int32, (1, rows), 1), t - 1)
    kpos = lax.broadcasted_iota(jnp.int32, (s_pad, 1), 0)
    visible = kpos <= past + tpos
    o_ref[...] = _topk_bias_t(sc_ref[...].T, visible, topk).T


def _dsa_s_select(n_rows, T, past, scores, rows=256):
    s_pad = scores.shape[1]
    spec = pl.BlockSpec((rows, s_pad), lambda i: (i, 0))
    return pl.pallas_call(
        functools.partial(_dsa_s_select_body, t=T, past=past, topk=min(TOPK_MAX, (past + T) // 4)),
        out_shape=jax.ShapeDtypeStruct((n_rows, s_pad), F32),
        grid=(n_rows // rows,),
        in_specs=[spec], out_specs=spec,
        compiler_params=_cparams(1),
        name="dsa_s_select",
    )(scores)


def _dsa_s_attn_body(pt_ref, q_ref, kn_ref, vn_ref, b_ref, kpool_ref, vpool_ref, o_ref,
                     kbuf_ref, vbuf_ref, ksem_ref, vsem_ref, *, e, n_pages, t):
    slot = _page_gather(pt_ref, kpool_ref, kbuf_ref, ksem_ref, e, n_pages)
    _page_gather(pt_ref, vpool_ref, vbuf_ref, vsem_ref, e, n_pages)
    past = n_pages * PAGE
    rep = A_HEADS // A_KV
    pad = jnp.zeros((PAGE - t, LANES), F32)
    for s in range(SEQ_PER_STEP):
        rows = slice(s * t, (s + 1) * t)
        kt = kbuf_ref[slot, s].astype(BF16)
        vt = vbuf_ref[slot, s].astype(BF16)
        k_new = jnp.concatenate([kn_ref[rows, :], pad], axis=0).astype(BF16)
        v_new = jnp.concatenate([vn_ref[rows, :], pad], axis=0).astype(BF16)
        q = q_ref[rows, :].astype(BF16)
        bias = jnp.concatenate([b_ref[rows, :]] * rep, axis=0)
        outs = []
        for gi in range(A_KV):
            gsl = slice(gi * A_DH, (gi + 1) * A_DH)
            qg = jnp.concatenate([q[:, (gi * rep + r) * A_DH:(gi * rep + r + 1) * A_DH] for r in range(rep)], axis=0)
            logits = jnp.concatenate([_dot(qg, kt[gsl, :]), _dot_nt(qg, k_new[:, gsl])], axis=-1) + bias
            p, l = _softmax(logits)
            p = p.astype(BF16)
            og = (_dot_nt(p[:, :past], vt[gsl, :]) + _dot(p[:, past:], v_new[:, gsl])) / l
            outs += [og[r * t:(r + 1) * t] for r in range(rep)]
        for hp in range(A_HEADS // 2):
            o_ref[rows, hp * LANES:(hp + 1) * LANES] = jnp.concatenate(outs[2 * hp:2 * hp + 2], axis=-1)


def _dsa_s_attn(Bs, T, e, page_table, q, k_new, v_new, bias, k_t, v_t):
    n_pages = page_table.shape[1]
    tok = lambda w: pl.BlockSpec((SEQ_PER_STEP * T, w), lambda b, pt: (b, 0))
    pool = pl.BlockSpec(memory_space=pl.ANY)
    buf = pltpu.VMEM((2, SEQ_PER_STEP, LANES, n_pages * PAGE), F32)
    return pl.pallas_call(
        functools.partial(_dsa_s_attn_body, e=e, n_pages=n_pages, t=T),
        out_shape=jax.ShapeDtypeStruct((Bs * T, A_HEADS * A_DH), F32),
        grid_spec=pltpu.PrefetchScalarGridSpec(
            num_scalar_prefetch=1, grid=(Bs // SEQ_PER_STEP,),
            in_specs=[tok(512), tok(LANES), tok(LANES), tok(bias.shape[1]), pool, pool],
            out_specs=tok(512),
            scratch_shapes=[buf, buf, pltpu.SemaphoreType.DMA((2,)), pltpu.SemaphoreType.DMA((2,))]),
        compiler_params=_cparams(1),
        name="dsa_s_attn",
    )(page_table, q, k_new, v_new, bias, k_t, v_t)


def _ret_log_gamma(h):
    return math.log1p(-(2.0 ** (-5.0 - h)))


def _ret_chunk(q, k, v, rg, s, h, c):
    lg = _ret_log_gamma(h)
    ri = lax.broadcasted_iota(jnp.int32, (c, c), 0)
    ci = lax.broadcasted_iota(jnp.int32, (c, c), 1)
    diff = (ri - ci).astype(F32)
    decay = jnp.where(diff >= 0, jnp.exp(jnp.maximum(diff, 0.0) * lg), 0.0)
    idx = lax.broadcasted_iota(jnp.int32, (c, 1), 0).astype(F32)
    q_dec = jnp.exp((idx + 1.0) * lg)
    k_dec = jnp.exp((c - 1.0 - idx) * lg)
    c_dec = math.exp(c * lg)
    qb = q.astype(BF16)
    att = _dot_nt(qb, k.astype(BF16)) * decay
    inner = _dot(att.astype(BF16), v.astype(BF16))
    cross = _dot(qb, s.astype(BF16)) * q_dec
    s_new = s * c_dec + _dot_tn((k * k_dec).astype(BF16), v.astype(BF16))
    return _rms(inner + cross) * _silu(rg), s_new


def _ret_s_body(rq_ref, rk_ref, rv_ref, rg_ref, s0_ref, o_ref, so_ref, *, t):
    for s in range(SEQ_PER_STEP):
        rows = slice(s * t, (s + 1) * t)
        for h in range(RET_HEADS):
            sl = slice(h * RET_DH, (h + 1) * RET_DH)
            o, s_new = _ret_chunk(rq_ref[rows, sl], rk_ref[rows, sl], rv_ref[rows, sl], rg_ref[rows, sl],
                                  s0_ref[s, h], h, t)
            o_ref[rows, sl] = o
            so_ref[s, h] = s_new


def _ret_sample(Bs, T, e, rq, rk, rv, rg, state_ret):
    tok = pl.BlockSpec((SEQ_PER_STEP * T, 512), lambda b: (b, 0))
    return pl.pallas_call(
        functools.partial(_ret_s_body, t=T),
        out_shape=[jax.ShapeDtypeStruct((Bs * T, 512), F32),
                   jax.ShapeDtypeStruct((Bs, RET_HEADS, RET_DH, RET_DH), F32)],
        grid=(Bs // SEQ_PER_STEP,),
        in_specs=[tok, tok, tok, tok,
                  pl.BlockSpec((None, SEQ_PER_STEP, RET_HEADS, RET_DH, RET_DH), lambda b: (e, b, 0, 0, 0))],
        out_specs=[tok, pl.BlockSpec((SEQ_PER_STEP, RET_HEADS, RET_DH, RET_DH), lambda b: (b, 0, 0, 0))],
        compiler_params=_cparams(1),
        name="ret_sample",
    )(rq, rk, rv, rg, state_ret)


def _even_out_body(x_ref, oa_ref, or_ref, gt_ref, gpost_ref, wa_ref, wr_ref, o_ref, *, g, rt):
    y = _dot(oa_ref[...].astype(BF16), wa_ref[...]) + _dot(or_ref[...].astype(BF16), wr_ref[...])
    o_ref[...] = x_ref[...] + _gated(_rms(y, gpost_ref[...]), gt_ref[...], g, rt)


def _even_out(t, x, o_a, o_r, mod, mcol, l, e, p):
    half = A_HEADS * A_DH
    return pl.pallas_call(
        functools.partial(_even_out_body, g=t.g, rt=t.rt),
        out_shape=jax.ShapeDtypeStruct((t.n_tok, D), F32),
        grid=t.grid,
        in_specs=[t.tok(D), t.tok(half), t.tok(half), t.mod(mcol), t.sel((None, None, 1, D), (l, 1, 0, 0)),
                  t.sel((None, half, D), (e, 0, 0)), t.sel((None, half, D), (e, 1, 0))],
        out_specs=t.tok(D),
        compiler_params=_cparams(2),
        name="even_out",
    )(x, o_a, o_r, mod, p["g_post"], p["w_out_even"], p["w_out_even"])


RET_CHUNK = 256


def _ffn_staged(x, mod_refs, gpre_ref, gpost_ref, wg_ref, wu_ref, wd_ref, between=()):
    sh_ref, sc_ref, gt_ref = mod_refs
    hf = _prenorm(x, gpre_ref[...], sh_ref[...], sc_ref[...], 1, x.shape[0]).astype(BF16)
    a, u = _dot(hf, wg_ref[...]), _dot(hf, wu_ref[...])
    for thunk in between:
        thunk()
    f = _dot((_silu(a) * u).astype(BF16), wd_ref[...])
    return x + 0.5 * (_rms(f, gpost_ref[...]) * gt_ref[...][0])


def _even_tail_body(rq_ref, rk_ref, rv_ref, rg_ref, x_ref, oa_ref, gt_ref, sh2_ref, sc2_ref, gt2_ref,
                    gpost_ref, gpre2_ref, gpost2_ref, wa_ref, wr_ref, wg_ref, wu_ref, wd_ref,
                    o_ref, so_ref, or_ref, s_ref, *, rows, nt, n_tiles):
    t = pl.program_id(0)
    j = lax.rem(t, nt)
    slot = lax.rem(t, 2)

    @pl.when(j == 0)
    def _():
        s_ref[...] = jnp.zeros_like(s_ref)

    def retention():
        for c in range(rows // RET_CHUNK):
            rs = slice(c * RET_CHUNK, (c + 1) * RET_CHUNK)
            for h in range(RET_HEADS):
                sl = slice(h * RET_DH, (h + 1) * RET_DH)
                o, s_new = _ret_chunk(rq_ref[rs, sl], rk_ref[rs, sl], rv_ref[rs, sl], rg_ref[rs, sl],
                                      s_ref[h], h, RET_CHUNK)
                or_ref[slot, rs, sl] = o
                s_ref[h] = s_new
        so_ref[0] = s_ref[...]

    def tail(between):
        o_r = or_ref[1 - slot]
        y = _dot(oa_ref[...].astype(BF16), wa_ref[...]) + _dot(o_r.astype(BF16), wr_ref[...])
        x1 = x_ref[...] + _rms(y, gpost_ref[...]) * gt_ref[...][0]
        o_ref[...] = _ffn_staged(x1, (sh2_ref, sc2_ref, gt2_ref), gpre2_ref, gpost2_ref, wg_ref, wu_ref, wd_ref,
                                 between=between)

    pl.when(t == 0)(retention)
    pl.when(jnp.logical_and(t > 0, t < n_tiles))(functools.partial(tail, (retention,)))
    pl.when(t == n_tiles)(functools.partial(tail, ()))


def _even_tail_prompt(B, L, x, o_a, rq, rk, rv, rg, mod, l, e, p, rows=512):
    assert rows == 2 * RET_CHUNK
    nt = L // rows
    n_tiles = B * nt
    half = A_HEADS * A_DH
    cur = lambda t: jnp.minimum(t, n_tiles - 1)
    prv = lambda t: jnp.maximum(t - 1, 0)
    sel = lambda block, idx: pl.BlockSpec(block, lambda t: idx, pipeline_mode=pl.Buffered(1))
    modc = lambda c: pl.BlockSpec((1, 1, D), lambda t: (prv(t) // nt, 0, c))
    gain = lambda n: sel((None, None, 1, D), (l, n, 0, 0))
    ret_in = pl.BlockSpec((rows, RET_HEADS * RET_DH), lambda t: (cur(t), 0))
    return pl.pallas_call(
        functools.partial(_even_tail_body, rows=rows, nt=nt, n_tiles=n_tiles),
        out_shape=[jax.ShapeDtypeStruct((B * L, D), F32),
                   jax.ShapeDtypeStruct((B, RET_HEADS, RET_DH, RET_DH), F32)],
        grid=(n_tiles + 1,),
        in_specs=[ret_in, ret_in, ret_in, ret_in,
                  pl.BlockSpec((rows, D), lambda t: (prv(t), 0)), pl.BlockSpec((rows, half), lambda t: (prv(t), 0)),
                  modc(5), modc(6), modc(7), modc(8), gain(1), gain(2), gain(2),
                  sel((None, half, D), (e, 0, 0)), sel((None, half, D), (e, 1, 0)),
                  sel((None, None, D, D_FF), (l, 1, 0, 0)), sel((None, None, D, D_FF), (l, 1, 0, 0)),
                  sel((None, None, D_FF, D), (l, 1, 0, 0))],
        out_specs=[pl.BlockSpec((rows, D), lambda t: (prv(t), 0)),
                   pl.BlockSpec((1, RET_HEADS, RET_DH, RET_DH), lambda t: (cur(t) // nt, 0, 0, 0))],
        scratch_shapes=[pltpu.VMEM((2, rows, RET_HEADS * RET_DH), F32), pltpu.VMEM((RET_HEADS, RET_DH, RET_DH), F32)],
        compiler_params=_cparams(1),
        name="even_tail_prompt",
    )(rq, rk, rv, rg, x, o_a, mod, mod, mod, mod, p["g_post"], p["g_pre"], p["g_post"],
      p["w_out_even"], p["w_out_even"], p["wg"], p["wu"], p["wd"])


def _softplus(x):
    return jnp.maximum(x, 0.0) + jnp.log1p(jnp.exp(-jnp.abs(x)))


def _scan_sublanes(a, b):
    sub = lax.broadcasted_iota(jnp.int32, (1, SUBLANES, a.shape[-1]), 1)
    d = 1
    while d < SUBLANES:
        keep = sub >= d
        a_sh = jnp.where(keep, pltpu.roll(a, d, 1), 1.0)
        b_sh = jnp.where(keep, pltpu.roll(b, d, 1), 0.0)
        b = b + a * b_sh
        a = a * a_sh
        d *= 2
    return a, b


def _rglru_gates(conv, wa_ref, wx_ref):
    convb = conv.astype(BF16)
    ra = jnp.concatenate([_dot(convb[:, n * RG_BW:(n + 1) * RG_BW], wa_ref[n]) for n in range(RG_BLOCKS)], axis=-1)
    rx = jnp.concatenate([_dot(convb[:, n * RG_BW:(n + 1) * RG_BW], wx_ref[n]) for n in range(RG_BLOCKS)], axis=-1)
    return ra, rx


def _odd_s_body(x_ref, sh_ref, sc_ref, gt_ref, gpre_ref, gpost_ref, win_ref, cw_ref, cb_ref,
                wa_ref, ba_ref, wx_ref, bx_ref, lam_ref, wout_ref, prev_ref, h0_ref,
                o_ref, xb_out_ref, hs_out_ref, *, g, rt):
    rows = g * rt
    x = x_ref[...]
    h = _prenorm(x, gpre_ref[...], sh_ref[...], sc_ref[...], g, rt).astype(BF16)
    proj = _dot(h, win_ref[...])
    gate_br = proj[:, :D]
    xb = proj[:, D:]
    pos = jnp.bitwise_and(lax.broadcasted_iota(jnp.int32, (rows, 1), 0), rt - 1)
    cw = cw_ref[...]
    prev = prev_ref[...]
    conv = xb * cw[CONV_W - 1:CONV_W] + cb_ref[...]
    for jj in range(CONV_W - 1):
        d = CONV_W - 1 - jj
        tap = jnp.where(pos >= d, pltpu.roll(xb, d, 0), pltpu.roll(prev, rows - (SUBLANES - d), 0))
        conv = conv + tap * cw[jj:jj + 1]
    xb_out_ref[...] = xb
    ra, rx = _rglru_gates(conv, wa_ref, wx_ref)
    r = _sigmoid(ra + ba_ref[...])
    ig = _sigmoid(rx + bx_ref[...])
    a = jnp.exp((-RG_C) * r * _softplus(-lam_ref[...]))
    b = jnp.sqrt(1.0 - a * a) * (ig * conv)
    b = b + a * h0_ref[...]
    n_grp = rows // SUBLANES
    hs = _scan_sublanes(a.reshape(n_grp, SUBLANES, D), b.reshape(n_grp, SUBLANES, D))[1].reshape(rows, D)
    hs_out_ref[...] = hs
    y = (hs * _gelu_tanh(gate_br)).astype(BF16)
    out = _dot(y, wout_ref[...])
    o_ref[...] = x + _gated(_rms(out, gpost_ref[...]), gt_ref[...], g, rt)


def _odd_sample(t, x, mod, mcol, l, e, p, prev, h0):
    vec = t.sel((None, 1, D), (e, 0, 0))
    rgw = t.sel((None, RG_BLOCKS, RG_BW, RG_BW), (e, 0, 0, 0))
    return pl.pallas_call(
        functools.partial(_odd_s_body, g=t.g, rt=t.rt),
        out_shape=[jax.ShapeDtypeStruct((t.n_tok, D), F32)] * 3,
        grid=t.grid,
        in_specs=[t.tok(D), t.mod(mcol), t.mod(mcol + 1), t.mod(mcol + 2),
                  t.sel((None, None, 1, D), (l, 1, 0, 0)), t.sel((None, None, 1, D), (l, 1, 0, 0)),
                  t.sel((None, D, 2 * D), (e, 0, 0)), t.sel((None, CONV_W, D), (e, 0, 0)), vec,
                  rgw, vec, rgw, vec, vec, t.sel((None, D, D), (e, 0, 0)), t.tok(D), t.tok(D)],
        out_specs=[t.tok(D)] * 3,
        compiler_params=_cparams(2),
        name="odd_sample",
    )(x, mod, mod, mod, p["g_pre"], p["g_post"], p["w_in_odd"], p["conv_w"], p["conv_b"],
      p["w_rg_a"], p["b_rg_a"], p["w_rg_x"], p["b_rg_x"], p["rg_lambda"], p["w_out_odd"], prev, h0)


def _odd_ffn_body(x_ref, sh_ref, sc_ref, gt_ref, sh2_ref, sc2_ref, gt2_ref,
                  gpre_ref, gpost_ref, gpre2_ref, gpost2_ref, win_ref, cw_ref, cb_ref,
                  wa_ref, ba_ref, wx_ref, bx_ref, lam_ref, wout_ref, wg_ref, wu_ref, wd_ref,
                  o_ref, tail_out_ref, h_out_ref, x1_ref, tail_ref, hcar_ref, *, rows, nt, n_tiles):
    t = pl.program_id(0)
    j = lax.rem(t, nt)
    slot = lax.rem(t, 2)
    n_grp = rows // SUBLANES

    @pl.when(t == 0)
    def _():
        x1_ref[...] = jnp.zeros_like(x1_ref)

    @pl.when(j == 0)
    def _():
        tail_ref[...] = jnp.zeros_like(tail_ref)
        hcar_ref[...] = jnp.zeros_like(hcar_ref)

    x = x_ref[...]
    h = _prenorm(x, gpre_ref[...], sh_ref[...], sc_ref[...], 1, rows).astype(BF16)
    proj = _dot(h, win_ref[...])
    gate_br = proj[:, :D]
    xb = proj[:, D:]
    row = lax.broadcasted_iota(jnp.int32, (rows, 1), 0)
    cw = cw_ref[...]
    xext = jnp.concatenate([tail_ref[...], xb], axis=0)
    conv = xb * cw[CONV_W - 1:CONV_W] + cb_ref[...]
    for jj in range(CONV_W - 1):
        conv = conv + pltpu.roll(xext, CONV_W - 1 - jj, 0)[SUBLANES:] * cw[jj:jj + 1]
    tail_ref[...] = xb[rows - SUBLANES:]
    ra, rx = _rglru_gates(conv, wa_ref, wx_ref)

    x1p = x1_ref[1 - slot]
    hf = _prenorm(x1p, gpre2_ref[...], sh2_ref[...], sc2_ref[...], 1, rows).astype(BF16)
    ff_a, ff_u = _dot(hf, wg_ref[...]), _dot(hf, wu_ref[...])

    r = _sigmoid(ra + ba_ref[...])
    ig = _sigmoid(rx + bx_ref[...])
    log_a = (-RG_C) * r * _softplus(-lam_ref[...])
    a = jnp.exp(log_a)
    mult = jnp.sqrt(1.0 - a * a)
    mult = jnp.where(jnp.logical_and(j == 0, row == 0), 1.0, mult)
    b = mult * (ig * conv)
    a, b = _scan_sublanes(a.reshape(n_grp, SUBLANES, D), b.reshape(n_grp, SUBLANES, D))
    carry = hcar_ref[0:1]
    parts = []
    for g8 in range(n_grp):
        part = b[g8] + a[g8] * carry
        carry = part[SUBLANES - 1:SUBLANES]
        parts.append(part)
    hs = jnp.concatenate(parts, axis=0)
    hcar_ref[0:1] = carry
    y = (hs * _gelu_tanh(gate_br)).astype(BF16)

    act = (_silu(ff_a) * ff_u).astype(BF16)
    x1 = x + _rms(_dot(y, wout_ref[...]), gpost_ref[...]) * gt_ref[...][0]
    x1_ref[slot] = x1
    f = _dot(act, wd_ref[...])
    o_ref[...] = x1p + 0.5 * (_rms(f, gpost2_ref[...]) * gt2_ref[...][0])

    @pl.when(t < n_tiles)
    def _():
        tail_out_ref[0] = xb[rows - SUBLANES:]
        h_out_ref[0] = parts[-1]


def _odd_ffn_prompt(B, L, x, mod, l, e, p, rows=512):
    nt = L // rows
    n_tiles = B * nt
    cur = lambda t: jnp.minimum(t, n_tiles - 1)
    prv = lambda t: jnp.maximum(t - 1, 0)
    sel = lambda block, idx: pl.BlockSpec(block, lambda t: idx, pipeline_mode=pl.Buffered(1))
    modc = lambda c, which: pl.BlockSpec((1, 1, D), lambda t: (which(t) // nt, 0, c))
    gain = lambda n: sel((None, None, 1, D), (l, n, 0, 0))
    vec = sel((None, 1, D), (e, 0, 0))
    rgw = sel((None, RG_BLOCKS, RG_BW, RG_BW), (e, 0, 0, 0))
    last = pl.BlockSpec((1, SUBLANES, D), lambda t: (cur(t) // nt, 0, 0))
    return pl.pallas_call(
        functools.partial(_odd_ffn_body, rows=rows, nt=nt, n_tiles=n_tiles),
        out_shape=[jax.ShapeDtypeStruct((B * L, D), F32),
                   jax.ShapeDtypeStruct((B, SUBLANES, D), F32),
                   jax.ShapeDtypeStruct((B, SUBLANES, D), F32)],
        grid=(n_tiles + 1,),
        in_specs=[pl.BlockSpec((rows, D), lambda t: (cur(t), 0)),
                  modc(3, cur), modc(4, cur), modc(5, cur), modc(6, prv), modc(7, prv), modc(8, prv),
                  gain(1), gain(1), gain(2), gain(2),
                  sel((None, D, 2 * D), (e, 0, 0)), sel((None, CONV_W, D), (e, 0, 0)), vec,
                  rgw, vec, rgw, vec, vec, sel((None, D, D), (e, 0, 0)),
                  sel((None, None, D, D_FF), (l, 1, 0, 0)), sel((None, None, D, D_FF), (l, 1, 0, 0)),
                  sel((None, None, D_FF, D), (l, 1, 0, 0))],
        out_specs=[pl.BlockSpec((rows, D), lambda t: (prv(t), 0)), last, last],
        scratch_shapes=[pltpu.VMEM((2, rows, D), F32), pltpu.VMEM((SUBLANES, D), F32), pltpu.VMEM((SUBLANES, D), F32)],
        compiler_params=_cparams(1),
        name="odd_ffn_prompt",
    )(x, mod, mod, mod, mod, mod, mod, p["g_pre"], p["g_post"], p["g_pre"], p["g_post"],
      p["w_in_odd"], p["conv_w"], p["conv_b"], p["w_rg_a"], p["b_rg_a"], p["w_rg_x"], p["b_rg_x"], p["rg_lambda"],
      p["w_out_odd"], p["wg"], p["wu"], p["wd"])


def _rope_tables(pos):
    posf = pos.astype(F32)[:, None]
    inv_a = jnp.power(jnp.float32(ROPE_THETA), -jnp.arange(0, ROT_DIM, 2, dtype=F32) / ROT_DIM)
    ang = posf * inv_a[None, :]
    cos, sin = jnp.cos(ang), jnp.sin(ang)
    n = pos.shape[0]
    half = ROT_DIM // 2
    rest = A_DH - ROT_DIM
    one, zero = jnp.ones((n, rest), F32), jnp.zeros((n, rest), F32)
    zh = jnp.zeros((n, half), F32)
    cos_h = jnp.concatenate([cos, cos, one], axis=1)
    s1_h = jnp.concatenate([-sin, zh, zero], axis=1)
    s2_h = jnp.concatenate([zh, sin, zero], axis=1)
    both = lambda a: jnp.concatenate([a, a], axis=1)
    wi_scale = jnp.full((n, A_DH), IDX_HEADS ** -0.5, F32)
    z64 = jnp.zeros((n, A_DH), F32)
    rope_a = jnp.stack([both(cos_h), both(s1_h), both(s2_h),
                        jnp.concatenate([cos_h, wi_scale], axis=1),
                        jnp.concatenate([s1_h, z64], axis=1),
                        jnp.concatenate([s2_h, z64], axis=1)])
    inv_r = jnp.power(jnp.float32(RET_THETA), -jnp.linspace(0.0, 1.0, RET_DH // 2, dtype=F32))
    ang_r = posf * inv_r[None, :]
    cr, sr = jnp.cos(ang_r), jnp.sin(ang_r)
    rope_r = jnp.stack([jnp.concatenate([cr, cr], axis=1), jnp.concatenate([-sr, sr], axis=1)])
    return rope_a, rope_r


def _pack_w_in_even(w):
    o = np.cumsum((0, 512, 128, 128, 256, IDX_HEADS, IDX_DIM, 512, 512, 512, 512))
    pad = jnp.zeros(w.shape[:2] + (LANES - IDX_DIM - IDX_HEADS,), w.dtype)
    return jnp.concatenate([w[..., o[0]:o[4]], w[..., o[5]:o[6]], w[..., o[4]:o[5]], pad, w[..., o[6]:o[10]]],
                           axis=-1).astype(BF16)


def kernel(x_prompt, x_sample, cache_k, cache_v, cache_kidx, state_ret, state_conv, state_rglru, page_table,
           c_prompt, c_sample, w_ada, b_ada, g_pre, g_post, w_ffn_gate, w_ffn_up, w_ffn_down,
           w_in_even, w_out_even, w_in_odd, w_out_odd, conv_w, conv_b, w_rg_a, b_rg_a, w_rg_x, b_rg_x, rg_lambda):
    B, L, _ = x_prompt.shape
    Bs, T, _ = x_sample.shape
    n_pages = page_table.shape[1]
    past = n_pages * cache_k.shape[2]
    n_phys = cache_k.shape[1]
    assert cache_k.shape[2] == PAGE and T == SUBLANES

    tp = _Tiling(B, L, 1, 512)
    ts = _Tiling(Bs, T, 64, T)
    ts_odd = _Tiling(Bs, T, 32, T)

    vec3 = lambda a: a.reshape(a.shape[0], 1, a.shape[1])
    p = {
        "g_pre": g_pre.reshape(DEPTH, 3, 1, D), "g_post": g_post.reshape(DEPTH, 3, 1, D),
        "wg": w_ffn_gate.astype(BF16), "wu": w_ffn_up.astype(BF16), "wd": w_ffn_down.astype(BF16),
        "w_in_even": _pack_w_in_even(w_in_even), "w_out_even": w_out_even.astype(BF16),
        "w_in_odd": w_in_odd.astype(BF16), "w_out_odd": w_out_odd.astype(BF16),
        "conv_w": conv_w, "conv_b": vec3(conv_b),
        "w_rg_a": w_rg_a.astype(BF16), "b_rg_a": vec3(b_rg_a),
        "w_rg_x": w_rg_x.astype(BF16), "b_rg_x": vec3(b_rg_x), "rg_lambda": vec3(rg_lambda),
    }
    k_t = jnp.transpose(cache_k, (0, 1, 3, 4, 2)).reshape(cache_k.shape[0], n_phys, LANES, PAGE)
    v_t = jnp.transpose(cache_v, (0, 1, 3, 4, 2)).reshape(cache_v.shape[0], n_phys, LANES, PAGE)
    kidx_t = jnp.transpose(cache_kidx, (0, 1, 3, 2))

    mod_all = _ada(jnp.concatenate([c_prompt, c_sample], axis=0), w_ada, b_ada)
    mod_p = mod_all[:, :B].reshape(DEPTH, B, 1, N_MOD * D)
    mod_s = mod_all[:, B:].reshape(DEPTH, Bs, 1, N_MOD * D)

    rope_a_p, rope_r_p = _rope_tables(jnp.arange(L, dtype=jnp.int32))
    rope_a_s, rope_r_s = _rope_tables(jnp.tile(past + jnp.arange(T, dtype=jnp.int32), ts.g))
    rope_map_p = lambda i, j: (0, j, 0)
    rope_map_s = lambda i, j: (0, 0, 0)

    xp = x_prompt.reshape(B * L, D)
    xs = x_sample.reshape(Bs * T, D)
    ks, vs, kis, rets, convs, hs = ([[], []] for _ in range(6))

    for l in range(DEPTH):
        e = l // 2
        groups = ((0, tp, xp, mod_p[l]), (1, ts, xs, mod_s[l]))
        new_x = []
        for gi, t, x, mod in groups:
            x = _ffn(t, x, mod, 0, l, 0, 0, p)
            if l % 2 == 0:
                if gi == 0:
                    q, kt, v, qi, kw, kit, rq, rk, rv, rg = _even_in(
                        t, True, x, mod, 3, l, e, p, rope_a_p, rope_r_p, rope_map_p)
                    o_a = _dsa_prompt(B, L, q, qi, kw, kt, v)
                    ks[0].append(jnp.transpose(kt.reshape(B, A_KV, A_DH, L), (0, 3, 1, 2)))
                    vs[0].append(v.reshape(B, L, A_KV, A_DH))
                    kis[0].append(jnp.transpose(kit, (0, 2, 1)))
                    x, s_new = _even_tail_prompt(B, L, x, o_a, rq, rk, rv, rg, mod, l, e, p)
                    rets[0].append(s_new)
                    new_x.append(x)
                    continue
                else:
                    q, k, v, qi, kw, rq, rk, rv, rg = _even_in(
                        t, False, x, mod, 3, l, e, p, rope_a_s, rope_r_s, rope_map_s)
                    scores = _dsa_s_scores(Bs, T, e, page_table, qi, kw, kidx_t)
                    bias = _dsa_s_select(Bs * T, T, past, scores)
                    o_a = _dsa_s_attn(Bs, T, e, page_table, q, k, v, bias, k_t, v_t)
                    o_r, s_new = _ret_sample(Bs, T, e, rq, rk, rv, rg, state_ret)
                    ks[1].append(k.reshape(Bs, T, A_KV, A_DH))
                    vs[1].append(v.reshape(Bs, T, A_KV, A_DH))
                    kis[1].append(kw[:, :IDX_DIM].reshape(Bs, T, IDX_DIM))
                rets[gi].append(s_new)
                x = _even_out(t, x, o_a, o_r, mod, 5, l, e, p)
            else:
                if gi == 0:
                    x, tail, hlast = _odd_ffn_prompt(B, L, x, mod, l, e, p)
                    convs[0].append(tail[:, SUBLANES - (CONV_W - 1):])
                    hs[0].append(hlast[:, SUBLANES - 1])
                    new_x.append(x)
                    continue
                else:
                    prev = jnp.pad(state_conv[e], ((0, 0), (SUBLANES - (CONV_W - 1), 0), (0, 0))).reshape(Bs * T, D)
                    h0 = jnp.pad(state_rglru[e][:, None, :], ((0, 0), (0, T - 1), (0, 0))).reshape(Bs * T, D)
                    x, xb, hseq = _odd_sample(ts_odd, x, mod, 3, l, e, p, prev, h0)
                    convs[1].append(xb.reshape(Bs, T, D)[:, T - (CONV_W - 1):])
                    hs[1].append(hseq.reshape(Bs, T, D)[:, T - 1])
            x = _ffn(t, x, mod, 6, l, 1, 2, p)
            new_x.append(x)
        xp, xs = new_x

    st = lambda lists, gi: jnp.stack(lists[gi])
    return (xp.reshape(B, L, D), xs.reshape(Bs, T, D),
            st(ks, 0), st(vs, 0), st(kis, 0), st(rets, 0), st(convs, 0), st(hs, 0),
            st(ks, 1), st(vs, 1), st(kis, 1), st(rets, 1), st(convs, 1), st(hs, 1))
```

```python
import functools
import math

import jax
import jax.numpy as jnp
import numpy as np
from jax import lax
from jax.experimental import pallas as pl
from jax.experimental.pallas import tpu as pltpu

F32 = jnp.float32
BF16 = jnp.bfloat16

D = 1024
DEPTH = 4
N_MOD = 9
A_HEADS = 8
A_KV = 2
A_DH = 64
ROT_DIM = 16
ROPE_THETA = 500000.0
IDX_HEADS = 4
IDX_DIM = 64
TOPK_MAX = 256
RET_HEADS = 4
RET_DH = 128
RET_THETA = 10000.0
RG_BLOCKS = 8
RG_BW = 128
CONV_W = 4
RG_C = 8.0
D_FF = 2816
EPS = 1e-6
P_IN = 3200
PAGE = 128

LANES = 128
SUBLANES = 8
VMEM_LIMIT = 56 * 1024 * 1024

NEG_INF = float("-inf")


def _cparams(n_axes, vmem=VMEM_LIMIT):
    return pltpu.CompilerParams(dimension_semantics=("arbitrary",) * n_axes, vmem_limit_bytes=vmem)


def _dot(a, b):
    return jnp.dot(a, b, preferred_element_type=F32)


def _dot_nt(a, b):
    return lax.dot_general(a, b, (((1,), (1,)), ((), ())), preferred_element_type=F32)


def _dot_tn(a, b):
    return lax.dot_general(a, b, (((0,), (0,)), ((), ())), preferred_element_type=F32)


def _sigmoid(x):
    return 1.0 / (1.0 + jnp.exp(-x))


def _silu(x):
    return x * _sigmoid(x)


def _gelu_tanh(x):
    return 0.5 * x * (1.0 + jnp.tanh(math.sqrt(2.0 / math.pi) * (x + 0.044715 * (x * x * x))))


def _rms(x, g=None):
    y = x * lax.rsqrt(jnp.mean(x * x, axis=-1, keepdims=True) + EPS)
    return y if g is None else y * g


def _prenorm(x, g_pre, shift, scale, g, rt):
    h = _rms(x, g_pre)
    if g == 1:
        return h * (1.0 + scale[0]) + shift[0]
    h3 = h.reshape(g, rt, D)
    return (h3 * (1.0 + scale) + shift).reshape(g * rt, D)


def _gated(y, gate, g, rt):
    if g == 1:
        return y * gate[0]
    return (y.reshape(g, rt, D) * gate).reshape(g * rt, D)


class _Tiling:
    def __init__(self, G, R, g, rt):
        assert G % g == 0 and R % rt == 0 and (g == 1 or rt == R)
        self.G, self.R, self.g, self.rt = G, R, g, rt
        self.nr = R // rt
        self.rows = g * rt
        self.grid = (G // g, self.nr)
        self.n_tok = G * R

    def tok(self, width):
        nr = self.nr
        return pl.BlockSpec((self.rows, width), lambda i, j: (i * nr + j, 0))

    def mod(self, c):
        return pl.BlockSpec((self.g, 1, D), lambda i, j: (i, 0, c))

    def sel(self, block, idx):
        return pl.BlockSpec(block, lambda i, j: idx, pipeline_mode=pl.Buffered(1))


def _ada_body(c_ref, w_ref, b_ref, o_ref):
    c = _silu(c_ref[...]).astype(BF16)
    o_ref[0] = _dot(c, w_ref[0].astype(BF16)) + b_ref[0]


def _ada(c_all, w_ada, b_ada):
    n = c_all.shape[0]
    return pl.pallas_call(
        _ada_body,
        out_shape=jax.ShapeDtypeStruct((DEPTH, n, N_MOD * D), F32),
        grid=(DEPTH, N_MOD),
        in_specs=[pl.BlockSpec((n, D), lambda l, j: (0, 0)),
                  pl.BlockSpec((1, D, D), lambda l, j: (l, 0, j)),
                  pl.BlockSpec((1, 1, D), lambda l, j: (l, 0, j))],
        out_specs=pl.BlockSpec((1, n, D), lambda l, j: (l, 0, j)),
        compiler_params=_cparams(2),
        name="ada",
    )(c_all, w_ada, b_ada.reshape(DEPTH, 1, N_MOD * D))


def _ffn_body(x_ref, sh_ref, sc_ref, gt_ref, gpre_ref, gpost_ref, wg_ref, wu_ref, wd_ref, o_ref, *, g, rt):
    x = x_ref[...]
    h = _prenorm(x, gpre_ref[...], sh_ref[...], sc_ref[...], g, rt).astype(BF16)
    a = _dot(h, wg_ref[...])
    u = _dot(h, wu_ref[...])
    act = (_silu(a) * u).astype(BF16)
    f = _dot(act, wd_ref[...])
    y = _rms(f, gpost_ref[...])
    o_ref[...] = x + 0.5 * _gated(y, gt_ref[...], g, rt)


def _ffn(t, x, mod, mcol, l, s, n, p):
    return pl.pallas_call(
        functools.partial(_ffn_body, g=t.g, rt=t.rt),
        out_shape=jax.ShapeDtypeStruct((t.n_tok, D), F32),
        grid=t.grid,
        in_specs=[t.tok(D), t.mod(mcol), t.mod(mcol + 1), t.mod(mcol + 2),
                  t.sel((None, None, 1, D), (l, n, 0, 0)), t.sel((None, None, 1, D), (l, n, 0, 0)),
                  t.sel((None, None, D, D_FF), (l, s, 0, 0)), t.sel((None, None, D, D_FF), (l, s, 0, 0)),
                  t.sel((None, None, D_FF, D), (l, s, 0, 0))],
        out_specs=t.tok(D),
        compiler_params=_cparams(2),
        name="ffn",
    )(x, mod, mod, mod, p["g_pre"], p["g_post"], p["wg"], p["wu"], p["wd"])


def _even_in_body(*refs, g, rt, prompt):
    x_ref, sh_ref, sc_ref, gpre_ref, w_ref, ra_ref, rr_ref = refs[:7]
    q_ref, k_ref, v_ref, qi_ref, kw_ref = refs[7:12]
    if prompt:
        kit_ref, rq_ref, rk_ref, rv_ref, rg_ref = refs[12:]
    else:
        rq_ref, rk_ref, rv_ref, rg_ref = refs[12:]
    h = _prenorm(x_ref[...], gpre_ref[...], sh_ref[...], sc_ref[...], g, rt).astype(BF16)
    proj = _dot(h, w_ref[...])

    def sec(c):
        return proj[:, c * LANES:(c + 1) * LANES]

    def rope_a(t, v):
        return (t * ra_ref[3 * v] + pltpu.roll(t, LANES - ROT_DIM // 2, 1) * ra_ref[3 * v + 1]
                + pltpu.roll(t, ROT_DIM // 2, 1) * ra_ref[3 * v + 2])

    def rope_r(t):
        return t * rr_ref[0] + pltpu.roll(t, RET_DH // 2, 1) * rr_ref[1]

    for c in range(4):
        q_ref[:, c * LANES:(c + 1) * LANES] = rope_a(sec(c), 0) * (A_DH ** -0.5)
    k = rope_a(sec(4), 0)
    v_ref[...] = sec(5)
    for c in range(2):
        qi_ref[:, c * LANES:(c + 1) * LANES] = rope_a(sec(6 + c), 0) * (IDX_DIM ** -0.5)
    kw = rope_a(sec(8), 1)
    kw_ref[...] = kw
    if prompt:
        k_ref[...] = k.T
        kit_ref[...] = kw.T[0:IDX_DIM]
    else:
        k_ref[...] = k
    for c in range(4):
        rq_ref[:, c * LANES:(c + 1) * LANES] = rope_r(sec(9 + c))
        rk_ref[:, c * LANES:(c + 1) * LANES] = rope_r(sec(13 + c) * (RET_DH ** -0.5))
        rv_ref[:, c * LANES:(c + 1) * LANES] = sec(17 + c)
        rg_ref[:, c * LANES:(c + 1) * LANES] = sec(21 + c)


def _even_in(t, prompt, x, mod, mcol, l, e, p, rope_a, rope_r, rope_map):
    tokw = lambda w: (jax.ShapeDtypeStruct((t.n_tok, w), F32), t.tok(w))
    seq_t = lambda w: (jax.ShapeDtypeStruct((t.G, w, t.R), F32),
                       pl.BlockSpec((None, w, t.rt), lambda i, j: (i, 0, j)))
    outs = [tokw(512), seq_t(LANES) if prompt else tokw(LANES), tokw(LANES), tokw(256), tokw(LANES)]
    if prompt:
        outs.append(seq_t(IDX_DIM))
    outs += [tokw(512)] * 4
    return pl.pallas_call(
        functools.partial(_even_in_body, g=t.g, rt=t.rt, prompt=prompt),
        out_shape=[o[0] for o in outs],
        grid=t.grid,
        in_specs=[t.tok(D), t.mod(mcol), t.mod(mcol + 1), t.sel((None, None, 1, D), (l, 1, 0, 0)),
                  t.sel((None, D, P_IN), (e, 0, 0)),
                  pl.BlockSpec((6, t.rows, LANES), rope_map),
                  pl.BlockSpec((2, t.rows, LANES), rope_map)],
        out_specs=[o[1] for o in outs],
        compiler_params=_cparams(2),
        name="even_in_p" if prompt else "even_in_s",
    )(x, mod, mod, p["g_pre"], p["w_in_even"], rope_a, rope_r)


def _sortable(s):
    s = jnp.where(s == 0.0, 0.0, s)
    b = lax.bitcast_convert_type(s, jnp.int32)
    return b ^ ((b >> 31) & 0x7FFFFFFF)


def _topk_bias_t(scores_t, visible_t, k):
    s_len, n_q = scores_t.shape
    key = _sortable(scores_t)
    fold = 8 * SUBLANES
    assert s_len % fold == 0 and s_len % LANES == 0
    i16 = jnp.int16

    def count(mask):
        part = jnp.sum(jnp.where(mask, 1.0, 0.0).reshape(s_len // fold, fold, n_q), axis=0)
        return jnp.sum(part, axis=0, keepdims=True)

    def count16(mask):
        ind = jnp.where(mask, i16(1), i16(0))
        part = ind[0:LANES]
        for r in range(1, s_len // LANES):
            part = part + ind[r * LANES:(r + 1) * LANES]
        return jnp.sum(part.astype(jnp.int32).astype(F32), axis=0, keepdims=True)

    def kth16(vals, kk):
        def step(it, t):
            cand = t + jnp.left_shift(jnp.int32(1), 15 - it)
            return jnp.where(count16(vals >= cand.astype(i16)) >= kk, cand, t)
        return lax.fori_loop(0, 16, step, jnp.full((1, n_q), -(2 ** 15), jnp.int32))

    hi = (key >> 16).astype(i16)
    t_hi = kth16(hi, k)
    t_hi16 = t_hi.astype(i16)
    lo = jnp.where(hi == t_hi16, ((key & 0xFFFF) - 2 ** 15).astype(i16), i16(-(2 ** 15)))
    t_lo = kth16(lo, k - count16(hi > t_hi16))
    thr = jnp.left_shift(t_hi, 16) + (t_lo + 2 ** 15)
    ge = key >= thr
    cnt_ge = count(ge)

    def no_ties():
        return jnp.where(visible_t, jnp.where(ge, 0.0, NEG_INF), NEG_INF)

    def ties():
        need = k - count(key > thr)
        tri = jnp.where(lax.broadcasted_iota(jnp.int32, (LANES, LANES), 0)
                        >= lax.broadcasted_iota(jnp.int32, (LANES, LANES), 1), 1.0, 0.0).astype(BF16)
        off = jnp.zeros((1, n_q), F32)
        pieces = []
        for c in range(s_len // LANES):
            sl = slice(c * LANES, (c + 1) * LANES)
            kc = key[sl, :]
            eq = kc == thr
            pc = _dot(tri, jnp.where(eq, 1.0, 0.0).astype(BF16)) + off
            off = pc[LANES - 1:LANES, :]
            tie_ok = jnp.where(eq, jnp.where(pc <= need, 0.0, NEG_INF), NEG_INF)
            b = jnp.where(kc > thr, 0.0, tie_ok)
            pieces.append(jnp.where(visible_t[sl, :], b, NEG_INF))
        return jnp.concatenate(pieces, axis=0)

    return lax.cond(jnp.max(cnt_ge) > k, ties, no_ties)


def _softmax(logits):
    m = jnp.max(logits, axis=-1, keepdims=True)
    p = jnp.exp(logits - m)
    return p, jnp.sum(p, axis=-1, keepdims=True)


def _dsa_p_body(q_ref, qi_ref, kw_ref, kt_ref, v_ref, o_ref, *, tq, topk):
    s_len = q_ref.shape[0]
    rep = A_HEADS // A_KV
    kt = kt_ref[...].astype(BF16)
    ki = kw_ref[:, 0:IDX_DIM].astype(BF16)
    v = v_ref[...].astype(BF16)
    for i in range(s_len // tq):
        rows = slice(i * tq, (i + 1) * tq)
        s_vis = (i + 1) * tq
        qit = qi_ref[rows, :].T.astype(BF16)
        wit = kw_ref[rows, :].T
        sc = jnp.zeros((s_vis, tq), F32)
        for h in range(IDX_HEADS):
            raw = _dot(ki[:s_vis], qit[h * IDX_DIM:(h + 1) * IDX_DIM, :])
            sc = sc + jnp.maximum(raw, 0.0) * wit[IDX_DIM + h:IDX_DIM + h + 1, :]
        kpos = lax.broadcasted_iota(jnp.int32, (s_vis, 1), 0)
        qpos = i * tq + lax.broadcasted_iota(jnp.int32, (1, tq), 1)
        visible = kpos <= qpos
        bias = _topk_bias_t(jnp.where(visible, sc, NEG_INF), visible, topk).T
        q = q_ref[rows, :].astype(BF16)
        for hp in range(A_HEADS // 2):
            outs = []
            for h in (2 * hp, 2 * hp + 1):
                gsl = slice((h // rep) * A_DH, (h // rep + 1) * A_DH)
                p, l = _softmax(_dot(q[:, h * A_DH:(h + 1) * A_DH], kt[gsl, :s_vis]) + bias)
                outs.append(_dot(p.astype(BF16), v[:s_vis, gsl]) / l)
            o_ref[rows, hp * LANES:(hp + 1) * LANES] = jnp.concatenate(outs, axis=-1)


def _dsa_prompt(B, L, q, qi, kw, kt, v, tq=256):
    tok = lambda w: pl.BlockSpec((L, w), lambda b: (b, 0))
    seq_t = lambda w: pl.BlockSpec((None, w, L), lambda b: (b, 0, 0))
    return pl.pallas_call(
        functools.partial(_dsa_p_body, tq=tq, topk=min(TOPK_MAX, L // 4)),
        out_shape=jax.ShapeDtypeStruct((B * L, A_HEADS * A_DH), F32),
        grid=(B,),
        in_specs=[tok(512), tok(256), tok(LANES), seq_t(LANES), tok(LANES)],
        out_specs=tok(512),
        compiler_params=_cparams(1),
        name="dsa_prompt",
    )(q, qi, kw, kt, v)


SEQ_PER_STEP = 8


def _page_gather(pt_ref, pool_ref, buf_ref, sem_ref, e, n_pages):
    b = pl.program_id(0)

    def copies(step, slot):
        return [pltpu.make_async_copy(pool_ref.at[e, pt_ref[step * SEQ_PER_STEP + s, p]],
                                      buf_ref.at[slot, s, :, pl.ds(p * PAGE, PAGE)],
                                      sem_ref.at[slot])
                for s in range(SEQ_PER_STEP) for p in range(n_pages)]

    @pl.when(b == 0)
    def _():
        for c in copies(0, 0):
            c.start()

    @pl.when(b + 1 < pl.num_programs(0))
    def _():
        for c in copies(b + 1, (b + 1) % 2):
            c.start()

    slot = b % 2
    for c in copies(b, slot):
        c.wait()
    return slot


def _dsa_s_scores_body(pt_ref, qi_ref, kw_ref, pool_ref, o_ref, buf_ref, sem_ref, *, e, n_pages, t):
    slot = _page_gather(pt_ref, pool_ref, buf_ref, sem_ref, e, n_pages)
    past = n_pages * PAGE
    lane = lax.broadcasted_iota(jnp.int32, (t, PAGE), 1)
    row = lax.broadcasted_iota(jnp.int32, (t, PAGE), 0)
    for s in range(SEQ_PER_STEP):
        rows = slice(s * t, (s + 1) * t)
        qi = qi_ref[rows, :]
        qs = jnp.concatenate([qi[:, h * IDX_DIM:(h + 1) * IDX_DIM] for h in range(IDX_HEADS)], axis=0).astype(BF16)
        kw = kw_ref[rows, :]
        wi = kw[:, IDX_DIM:LANES]
        ki_new = jnp.concatenate([kw[:, 0:IDX_DIM], jnp.zeros((PAGE - t, IDX_DIM), F32)], axis=0).astype(BF16)
        kit = buf_ref[slot, s].astype(BF16)

        def weigh(raw):
            raw = jnp.maximum(raw, 0.0)
            sc = raw[0:t] * wi[:, 0:1]
            for h in range(1, IDX_HEADS):
                sc = sc + raw[h * t:(h + 1) * t] * wi[:, h:h + 1]
            return sc

        o_ref[rows, :past] = weigh(_dot(qs, kit))
        o_ref[rows, past:] = jnp.where(lane <= row, weigh(_dot_nt(qs, ki_new)), NEG_INF)


def _dsa_s_scores(Bs, T, e, page_table, qi, kw, kidx_t):
    n_pages = page_table.shape[1]
    tok = lambda w: pl.BlockSpec((SEQ_PER_STEP * T, w), lambda b, pt: (b, 0))
    s_pad = (n_pages + 1) * PAGE
    return pl.pallas_call(
        functools.partial(_dsa_s_scores_body, e=e, n_pages=n_pages, t=T),
        out_shape=jax.ShapeDtypeStruct((Bs * T, s_pad), F32),
        grid_spec=pltpu.PrefetchScalarGridSpec(
            num_scalar_prefetch=1, grid=(Bs // SEQ_PER_STEP,),
            in_specs=[tok(256), tok(LANES), pl.BlockSpec(memory_space=pl.ANY)],
            out_specs=tok(s_pad),
            scratch_shapes=[pltpu.VMEM((2, SEQ_PER_STEP, IDX_DIM, n_pages * PAGE), F32),
                            pltpu.SemaphoreType.DMA((2,))]),
        compiler_params=_cparams(1),
        name="dsa_s_scores",
    )(page_table, qi, kw, kidx_t)


def _dsa_s_select_body(sc_ref, o_ref, *, t, past, topk):
    rows, s_pad = sc_ref.shape
    tpos = jnp.bitwise_and(lax.broadcasted_iota(jnp.int32, (1, rows), 1), t - 1)
    kpos = lax.broadcasted_iota(jnp.int32, (s_pad, 1), 0)
    visible = kpos <= past + tpos
    o_ref[...] = _topk_bias_t(sc_ref[...].T, visible, topk).T


def _dsa_s_select(n_rows, T, past, scores, rows=256):
    s_pad = scores.shape[1]
    spec = pl.BlockSpec((rows, s_pad), lambda i: (i, 0))
    return pl.pallas_call(
        functools.partial(_dsa_s_select_body, t=T, past=past, topk=min(TOPK_MAX, (past + T) // 4)),
        out_shape=jax.ShapeDtypeStruct((n_rows, s_pad), F32),
        grid=(n_rows // rows,),
        in_specs=[spec], out_specs=spec,
        compiler_params=_cparams(1),
        name="dsa_s_select",
    )(scores)


def _dsa_s_attn_body(pt_ref, q_ref, kn_ref, vn_ref, b_ref, kpool_ref, vpool_ref, o_ref,
                     kbuf_ref, vbuf_ref, ksem_ref, vsem_ref, *, e, n_pages, t):
    slot = _page_gather(pt_ref, kpool_ref, kbuf_ref, ksem_ref, e, n_pages)
    _page_gather(pt_ref, vpool_ref, vbuf_ref, vsem_ref, e, n_pages)
    past = n_pages * PAGE
    rep = A_HEADS // A_KV
    pad = jnp.zeros((PAGE - t, LANES), F32)
    for s in range(SEQ_PER_STEP):
        rows = slice(s * t, (s + 1) * t)
        kt = kbuf_ref[slot, s].astype(BF16)
        vt = vbuf_ref[slot, s].astype(BF16)
        k_new = jnp.concatenate([kn_ref[rows, :], pad], axis=0).astype(BF16)
        v_new = jnp.concatenate([vn_ref[rows, :], pad], axis=0).astype(BF16)
        q = q_ref[rows, :].astype(BF16)
        bias = jnp.concatenate([b_ref[rows, :]] * rep, axis=0)
        outs = []
        for gi in range(A_KV):
            gsl = slice(gi * A_DH, (gi + 1) * A_DH)
            qg = jnp.concatenate([q[:, (gi * rep + r) * A_DH:(gi * rep + r + 1) * A_DH] for r in range(rep)], axis=0)
            logits = jnp.concatenate([_dot(qg, kt[gsl, :]), _dot_nt(qg, k_new[:, gsl])], axis=-1) + bias
            p, l = _softmax(logits)
            p = p.astype(BF16)
            og = (_dot_nt(p[:, :past], vt[gsl, :]) + _dot(p[:, past:], v_new[:, gsl])) / l
            outs += [og[r * t:(r + 1) * t] for r in range(rep)]
        for hp in range(A_HEADS // 2):
            o_ref[rows, hp * LANES:(hp + 1) * LANES] = jnp.concatenate(outs[2 * hp:2 * hp + 2], axis=-1)


def _dsa_s_attn(Bs, T, e, page_table, q, k_new, v_new, bias, k_t, v_t):
    n_pages = page_table.shape[1]
    tok = lambda w: pl.BlockSpec((SEQ_PER_STEP * T, w), lambda b, pt: (b, 0))
    pool = pl.BlockSpec(memory_space=pl.ANY)
    buf = pltpu.VMEM((2, SEQ_PER_STEP, LANES, n_pages * PAGE), F32)
    return pl.pallas_call(
        functools.partial(_dsa_s_attn_body, e=e, n_pages=n_pages, t=T),
        out_shape=jax.ShapeDtypeStruct((Bs * T, A_HEADS * A_DH), F32),
        grid_spec=pltpu.PrefetchScalarGridSpec(
            num_scalar_prefetch=1, grid=(Bs // SEQ_PER_STEP,),
            in_specs=[tok(512), tok(LANES), tok(LANES), tok(bias.shape[1]), pool, pool],
            out_specs=tok(512),
            scratch_shapes=[buf, buf, pltpu.SemaphoreType.DMA((2,)), pltpu.SemaphoreType.DMA((2,))]),
        compiler_params=_cparams(1),
        name="dsa_s_attn",
    )(page_table, q, k_new, v_new, bias, k_t, v_t)


def _ret_log_gamma(h):
    return math.log1p(-(2.0 ** (-5.0 - h)))


def _ret_chunk(q, k, v, rg, s, h, c):
    lg = _ret_log_gamma(h)
    ri = lax.broadcasted_iota(jnp.int32, (c, c), 0)
    ci = lax.broadcasted_iota(jnp.int32, (c, c), 1)
    diff = (ri - ci).astype(F32)
    decay = jnp.where(diff >= 0, jnp.exp(jnp.maximum(diff, 0.0) * lg), 0.0)
    idx = lax.broadcasted_iota(jnp.int32, (c, 1), 0).astype(F32)
    q_dec = jnp.exp((idx + 1.0) * lg)
    k_dec = jnp.exp((c - 1.0 - idx) * lg)
    c_dec = math.exp(c * lg)
    qb = q.astype(BF16)
    att = _dot_nt(qb, k.astype(BF16)) * decay
    inner = _dot(att.astype(BF16), v.astype(BF16))
    cross = _dot(qb, s.astype(BF16)) * q_dec
    s_new = s * c_dec + _dot_tn((k * k_dec).astype(BF16), v.astype(BF16))
    return _rms(inner + cross) * _silu(rg), s_new


def _ret_s_body(rq_ref, rk_ref, rv_ref, rg_ref, s0_ref, o_ref, so_ref, *, t):
    for s in range(SEQ_PER_STEP):
        rows = slice(s * t, (s + 1) * t)
        for h in range(RET_HEADS):
            sl = slice(h * RET_DH, (h + 1) * RET_DH)
            o, s_new = _ret_chunk(rq_ref[rows, sl], rk_ref[rows, sl], rv_ref[rows, sl], rg_ref[rows, sl],
                                  s0_ref[s, h], h, t)
            o_ref[rows, sl] = o
            so_ref[s, h] = s_new


def _ret_sample(Bs, T, e, rq, rk, rv, rg, state_ret):
    tok = pl.BlockSpec((SEQ_PER_STEP * T, 512), lambda b: (b, 0))
    return pl.pallas_call(
        functools.partial(_ret_s_body, t=T),
        out_shape=[jax.ShapeDtypeStruct((Bs * T, 512), F32),
                   jax.ShapeDtypeStruct((Bs, RET_HEADS, RET_DH, RET_DH), F32)],
        grid=(Bs // SEQ_PER_STEP,),
        in_specs=[tok, tok, tok, tok,
                  pl.BlockSpec((None, SEQ_PER_STEP, RET_HEADS, RET_DH, RET_DH), lambda b: (e, b, 0, 0, 0))],
        out_specs=[tok, pl.BlockSpec((SEQ_PER_STEP, RET_HEADS, RET_DH, RET_DH), lambda b: (b, 0, 0, 0))],
        compiler_params=_cparams(1),
        name="ret_sample",
    )(rq, rk, rv, rg, state_ret)


def _even_out_body(x_ref, oa_ref, or_ref, gt_ref, gpost_ref, wa_ref, wr_ref, o_ref, *, g, rt):
    y = _dot(oa_ref[...].astype(BF16), wa_ref[...]) + _dot(or_ref[...].astype(BF16), wr_ref[...])
    o_ref[...] = x_ref[...] + _gated(_rms(y, gpost_ref[...]), gt_ref[...], g, rt)


def _even_out(t, x, o_a, o_r, mod, mcol, l, e, p):
    half = A_HEADS * A_DH
    return pl.pallas_call(
        functools.partial(_even_out_body, g=t.g, rt=t.rt),
        out_shape=jax.ShapeDtypeStruct((t.n_tok, D), F32),
        grid=t.grid,
        in_specs=[t.tok(D), t.tok(half), t.tok(half), t.mod(mcol), t.sel((None, None, 1, D), (l, 1, 0, 0)),
                  t.sel((None, half, D), (e, 0, 0)), t.sel((None, half, D), (e, 1, 0))],
        out_specs=t.tok(D),
        compiler_params=_cparams(2),
        name="even_out",
    )(x, o_a, o_r, mod, p["g_post"], p["w_out_even"], p["w_out_even"])


RET_CHUNK = 256


def _ffn_staged(x, mod_refs, gpre_ref, gpost_ref, wg_ref, wu_ref, wd_ref, between=()):
    sh_ref, sc_ref, gt_ref = mod_refs
    hf = _prenorm(x, gpre_ref[...], sh_ref[...], sc_ref[...], 1, x.shape[0]).astype(BF16)
    a, u = _dot(hf, wg_ref[...]), _dot(hf, wu_ref[...])
    for thunk in between:
        thunk()
    f = _dot((_silu(a) * u).astype(BF16), wd_ref[...])
    return x + 0.5 * (_rms(f, gpost_ref[...]) * gt_ref[...][0])


def _even_tail_body(rq_ref, rk_ref, rv_ref, rg_ref, x_ref, oa_ref, gt_ref, sh2_ref, sc2_ref, gt2_ref,
                    gpost_ref, gpre2_ref, gpost2_ref, wa_ref, wr_ref, wg_ref, wu_ref, wd_ref,
                    o_ref, so_ref, or_ref, s_ref, *, rows, nt, n_tiles):
    t = pl.program_id(0)
    j = lax.rem(t, nt)
    slot = lax.rem(t, 2)

    @pl.when(j == 0)
    def _():
        s_ref[...] = jnp.zeros_like(s_ref)

    def retention():
        for c in range(rows // RET_CHUNK):
            rs = slice(c * RET_CHUNK, (c + 1) * RET_CHUNK)
            for h in range(RET_HEADS):
                sl = slice(h * RET_DH, (h + 1) * RET_DH)
                o, s_new = _ret_chunk(rq_ref[rs, sl], rk_ref[rs, sl], rv_ref[rs, sl], rg_ref[rs, sl],
                                      s_ref[h], h, RET_CHUNK)
                or_ref[slot, rs, sl] = o
                s_ref[h] = s_new
        so_ref[0] = s_ref[...]

    def tail(between):
        o_r = or_ref[1 - slot]
        y = _dot(oa_ref[...].astype(BF16), wa_ref[...]) + _dot(o_r.astype(BF16), wr_ref[...])
        x1 = x_ref[...] + _rms(y, gpost_ref[...]) * gt_ref[...][0]
        o_ref[...] = _ffn_staged(x1, (sh2_ref, sc2_ref, gt2_ref), gpre2_ref, gpost2_ref, wg_ref, wu_ref, wd_ref,
                                 between=between)

    pl.when(t == 0)(retention)
    pl.when(jnp.logical_and(t > 0, t < n_tiles))(functools.partial(tail, (retention,)))
    pl.when(t == n_tiles)(functools.partial(tail, ()))


def _even_tail_prompt(B, L, x, o_a, rq, rk, rv, rg, mod, l, e, p, rows=512):
    assert rows == 2 * RET_CHUNK
    nt = L // rows
    n_tiles = B * nt
    half = A_HEADS * A_DH
    cur = lambda t: jnp.minimum(t, n_tiles - 1)
    prv = lambda t: jnp.maximum(t - 1, 0)
    sel = lambda block, idx: pl.BlockSpec(block, lambda t: idx, pipeline_mode=pl.Buffered(1))
    modc = lambda c: pl.BlockSpec((1, 1, D), lambda t: (prv(t) // nt, 0, c))
    gain = lambda n: sel((None, None, 1, D), (l, n, 0, 0))
    ret_in = pl.BlockSpec((rows, RET_HEADS * RET_DH), lambda t: (cur(t), 0))
    return pl.pallas_call(
        functools.partial(_even_tail_body, rows=rows, nt=nt, n_tiles=n_tiles),
        out_shape=[jax.ShapeDtypeStruct((B * L, D), F32),
                   jax.ShapeDtypeStruct((B, RET_HEADS, RET_DH, RET_DH), F32)],
        grid=(n_tiles + 1,),
        in_specs=[ret_in, ret_in, ret_in, ret_in,
                  pl.BlockSpec((rows, D), lambda t: (prv(t), 0)), pl.BlockSpec((rows, half), lambda t: (prv(t), 0)),
                  modc(5), modc(6), modc(7), modc(8), gain(1), gain(2), gain(2),
                  sel((None, half, D), (e, 0, 0)), sel((None, half, D), (e, 1, 0)),
                  sel((None, None, D, D_FF), (l, 1, 0, 0)), sel((None, None, D, D_FF), (l, 1, 0, 0)),
                  sel((None, None, D_FF, D), (l, 1, 0, 0))],
        out_specs=[pl.BlockSpec((rows, D), lambda t: (prv(t), 0)),
                   pl.BlockSpec((1, RET_HEADS, RET_DH, RET_DH), lambda t: (cur(t) // nt, 0, 0, 0))],
        scratch_shapes=[pltpu.VMEM((2, rows, RET_HEADS * RET_DH), F32), pltpu.VMEM((RET_HEADS, RET_DH, RET_DH), F32)],
        compiler_params=_cparams(1),
        name="even_tail_prompt",
    )(rq, rk, rv, rg, x, o_a, mod, mod, mod, mod, p["g_post"], p["g_pre"], p["g_post"],
      p["w_out_even"], p["w_out_even"], p["wg"], p["wu"], p["wd"])


def _softplus(x):
    return jnp.maximum(x, 0.0) + jnp.log1p(jnp.exp(-jnp.abs(x)))


def _scan_sublanes(a, b):
    sub = lax.broadcasted_iota(jnp.int32, (1, SUBLANES, a.shape[-1]), 1)
    d = 1
    while d < SUBLANES:
        keep = sub >= d
        a_sh = jnp.where(keep, pltpu.roll(a, d, 1), 1.0)
        b_sh = jnp.where(keep, pltpu.roll(b, d, 1), 0.0)
        b = b + a * b_sh
        a = a * a_sh
        d *= 2
    return a, b


def _rglru_gates(conv, wa_ref, wx_ref):
    convb = conv.astype(BF16)
    ra = jnp.concatenate([_dot(convb[:, n * RG_BW:(n + 1) * RG_BW], wa_ref[n]) for n in range(RG_BLOCKS)], axis=-1)
    rx = jnp.concatenate([_dot(convb[:, n * RG_BW:(n + 1) * RG_BW], wx_ref[n]) for n in range(RG_BLOCKS)], axis=-1)
    return ra, rx


def _odd_s_body(x_ref, sh_ref, sc_ref, gt_ref, gpre_ref, gpost_ref, win_ref, cw_ref, cb_ref,
                wa_ref, ba_ref, wx_ref, bx_ref, lam_ref, wout_ref, prev_ref, h0_ref,
                o_ref, xb_out_ref, hs_out_ref, *, g, rt):
    rows = g * rt
    x = x_ref[...]
    h = _prenorm(x, gpre_ref[...], sh_ref[...], sc_ref[...], g, rt).astype(BF16)
    proj = _dot(h, win_ref[...])
    gate_br = proj[:, :D]
    xb = proj[:, D:]
    pos = jnp.bitwise_and(lax.broadcasted_iota(jnp.int32, (rows, 1), 0), rt - 1)
    cw = cw_ref[...]
    prev = prev_ref[...]
    conv = xb * cw[CONV_W - 1:CONV_W] + cb_ref[...]
    for jj in range(CONV_W - 1):
        d = CONV_W - 1 - jj
        tap = jnp.where(pos >= d, pltpu.roll(xb, d, 0), pltpu.roll(prev, rows - (SUBLANES - d), 0))
        conv = conv + tap * cw[jj:jj + 1]
    xb_out_ref[...] = xb
    ra, rx = _rglru_gates(conv, wa_ref, wx_ref)
    r = _sigmoid(ra + ba_ref[...])
    ig = _sigmoid(rx + bx_ref[...])
    a = jnp.exp((-RG_C) * r * _softplus(-lam_ref[...]))
    b = jnp.sqrt(1.0 - a * a) * (ig * conv)
    b = b + a * h0_ref[...]
    n_grp = rows // SUBLANES
    hs = _scan_sublanes(a.reshape(n_grp, SUBLANES, D), b.reshape(n_grp, SUBLANES, D))[1].reshape(rows, D)
    hs_out_ref[...] = hs
    y = (hs * _gelu_tanh(gate_br)).astype(BF16)
    out = _dot(y, wout_ref[...])
    o_ref[...] = x + _gated(_rms(out, gpost_ref[...]), gt_ref[...], g, rt)


def _odd_sample(t, x, mod, mcol, l, e, p, prev, h0):
    vec = t.sel((None, 1, D), (e, 0, 0))
    rgw = t.sel((None, RG_BLOCKS, RG_BW, RG_BW), (e, 0, 0, 0))
    return pl.pallas_call(
        functools.partial(_odd_s_body, g=t.g, rt=t.rt),
        out_shape=[jax.ShapeDtypeStruct((t.n_tok, D), F32)] * 3,
        grid=t.grid,
        in_specs=[t.tok(D), t.mod(mcol), t.mod(mcol + 1), t.mod(mcol + 2),
                  t.sel((None, None, 1, D), (l, 1, 0, 0)), t.sel((None, None, 1, D), (l, 1, 0, 0)),
                  t.sel((None, D, 2 * D), (e, 0, 0)), t.sel((None, CONV_W, D), (e, 0, 0)), vec,
                  rgw, vec, rgw, vec, vec, t.sel((None, D, D), (e, 0, 0)), t.tok(D), t.tok(D)],
        out_specs=[t.tok(D)] * 3,
        compiler_params=_cparams(2),
        name="odd_sample",
    )(x, mod, mod, mod, p["g_pre"], p["g_post"], p["w_in_odd"], p["conv_w"], p["conv_b"],
      p["w_rg_a"], p["b_rg_a"], p["w_rg_x"], p["b_rg_x"], p["rg_lambda"], p["w_out_odd"], prev, h0)


def _odd_ffn_body(x_ref, sh_ref, sc_ref, gt_ref, sh2_ref, sc2_ref, gt2_ref,
                  gpre_ref, gpost_ref, gpre2_ref, gpost2_ref, win_ref, cw_ref, cb_ref,
                  wa_ref, ba_ref, wx_ref, bx_ref, lam_ref, wout_ref, wg_ref, wu_ref, wd_ref,
                  o_ref, tail_out_ref, h_out_ref, x1_ref, tail_ref, hcar_ref, *, rows, nt, n_tiles):
    t = pl.program_id(0)
    j = lax.rem(t, nt)
    slot = lax.rem(t, 2)
    n_grp = rows // SUBLANES

    @pl.when(t == 0)
    def _():
        x1_ref[...] = jnp.zeros_like(x1_ref)

    @pl.when(j == 0)
    def _():
        tail_ref[...] = jnp.zeros_like(tail_ref)
        hcar_ref[...] = jnp.zeros_like(hcar_ref)

    x = x_ref[...]
    h = _prenorm(x, gpre_ref[...], sh_ref[...], sc_ref[...], 1, rows).astype(BF16)
    proj = _dot(h, win_ref[...])
    gate_br = proj[:, :D]
    xb = proj[:, D:]
    row = lax.broadcasted_iota(jnp.int32, (rows, 1), 0)
    cw = cw_ref[...]
    xext = jnp.concatenate([tail_ref[...], xb], axis=0)
    conv = xb * cw[CONV_W - 1:CONV_W] + cb_ref[...]
    for jj in range(CONV_W - 1):
        conv = conv + pltpu.roll(xext, CONV_W - 1 - jj, 0)[SUBLANES:] * cw[jj:jj + 1]
    tail_ref[...] = xb[rows - SUBLANES:]
    ra, rx = _rglru_gates(conv, wa_ref, wx_ref)

    x1p = x1_ref[1 - slot]
    hf = _prenorm(x1p, gpre2_ref[...], sh2_ref[...], sc2_ref[...], 1, rows).astype(BF16)
    ff_a, ff_u = _dot(hf, wg_ref[...]), _dot(hf, wu_ref[...])

    r = _sigmoid(ra + ba_ref[...])
    ig = _sigmoid(rx + bx_ref[...])
    log_a = (-RG_C) * r * _softplus(-lam_ref[...])
    a = jnp.exp(log_a)
    mult = jnp.sqrt(1.0 - a * a)
    mult = jnp.where(jnp.logical_and(j == 0, row == 0), 1.0, mult)
    b = mult * (ig * conv)
    a, b = _scan_sublanes(a.reshape(n_grp, SUBLANES, D), b.reshape(n_grp, SUBLANES, D))
    carry = hcar_ref[0:1]
    parts = []
    for g8 in range(n_grp):
        part = b[g8] + a[g8] * carry
        carry = part[SUBLANES - 1:SUBLANES]
        parts.append(part)
    hs = jnp.concatenate(parts, axis=0)
    hcar_ref[0:1] = carry
    y = (hs * _gelu_tanh(gate_br)).astype(BF16)

    act = (_silu(ff_a) * ff_u).astype(BF16)
    x1 = x + _rms(_dot(y, wout_ref[...]), gpost_ref[...]) * gt_ref[...][0]
    x1_ref[slot] = x1
    f = _dot(act, wd_ref[...])
    o_ref[...] = x1p + 0.5 * (_rms(f, gpost2_ref[...]) * gt2_ref[...][0])

    @pl.when(t < n_tiles)
    def _():
        tail_out_ref[0] = xb[rows - SUBLANES:]
        h_out_ref[0] = parts[-1]


def _odd_ffn_prompt(B, L, x, mod, l, e, p, rows=512):
    nt = L // rows
    n_tiles = B * nt
    cur = lambda t: jnp.minimum(t, n_tiles - 1)
    prv = lambda t: jnp.maximum(t - 1, 0)
    sel = lambda block, idx: pl.BlockSpec(block, lambda t: idx, pipeline_mode=pl.Buffered(1))
    modc = lambda c, which: pl.BlockSpec((1, 1, D), lambda t: (which(t) // nt, 0, c))
    gain = lambda n: sel((None, None, 1, D), (l, n, 0, 0))
    vec = sel((None, 1, D), (e, 0, 0))
    rgw = sel((None, RG_BLOCKS, RG_BW, RG_BW), (e, 0, 0, 0))
    last = pl.BlockSpec((1, SUBLANES, D), lambda t: (cur(t) // nt, 0, 0))
    return pl.pallas_call(
        functools.partial(_odd_ffn_body, rows=rows, nt=nt, n_tiles=n_tiles),
        out_shape=[jax.ShapeDtypeStruct((B * L, D), F32),
                   jax.ShapeDtypeStruct((B, SUBLANES, D), F32),
                   jax.ShapeDtypeStruct((B, SUBLANES, D), F32)],
        grid=(n_tiles + 1,),
        in_specs=[pl.BlockSpec((rows, D), lambda t: (cur(t), 0)),
                  modc(3, cur), modc(4, cur), modc(5, cur), modc(6, prv), modc(7, prv), modc(8, prv),
                  gain(1), gain(1), gain(2), gain(2),
                  sel((None, D, 2 * D), (e, 0, 0)), sel((None, CONV_W, D), (e, 0, 0)), vec,
                  rgw, vec, rgw, vec, vec, sel((None, D, D), (e, 0, 0)),
                  sel((None, None, D, D_FF), (l, 1, 0, 0)), sel((None, None, D, D_FF), (l, 1, 0, 0)),
                  sel((None, None, D_FF, D), (l, 1, 0, 0))],
        out_specs=[pl.BlockSpec((rows, D), lambda t: (prv(t), 0)), last, last],
        scratch_shapes=[pltpu.VMEM((2, rows, D), F32), pltpu.VMEM((SUBLANES, D), F32), pltpu.VMEM((SUBLANES, D), F32)],
        compiler_params=_cparams(1),
        name="odd_ffn_prompt",
    )(x, mod, mod, mod, mod, mod, mod, p["g_pre"], p["g_post"], p["g_pre"], p["g_post"],
      p["w_in_odd"], p["conv_w"], p["conv_b"], p["w_rg_a"], p["b_rg_a"], p["w_rg_x"], p["b_rg_x"], p["rg_lambda"],
      p["w_out_odd"], p["wg"], p["wu"], p["wd"])


def _rope_tables(pos):
    posf = pos.astype(F32)[:, None]
    inv_a = jnp.power(jnp.float32(ROPE_THETA), -jnp.arange(0, ROT_DIM, 2, dtype=F32) / ROT_DIM)
    ang = posf * inv_a[None, :]
    cos, sin = jnp.cos(ang), jnp.sin(ang)
    n = pos.shape[0]
    half = ROT_DIM // 2
    rest = A_DH - ROT_DIM
    one, zero = jnp.ones((n, rest), F32), jnp.zeros((n, rest), F32)
    zh = jnp.zeros((n, half), F32)
    cos_h = jnp.concatenate([cos, cos, one], axis=1)
    s1_h = jnp.concatenate([-sin, zh, zero], axis=1)
    s2_h = jnp.concatenate([zh, sin, zero], axis=1)
    both = lambda a: jnp.concatenate([a, a], axis=1)
    wi_scale = jnp.full((n, A_DH), IDX_HEADS ** -0.5, F32)
    z64 = jnp.zeros((n, A_DH), F32)
    rope_a = jnp.stack([both(cos_h), both(s1_h), both(s2_h),
                        jnp.concatenate([cos_h, wi_scale], axis=1),
                        jnp.concatenate([s1_h, z64], axis=1),
                        jnp.concatenate([s2_h, z64], axis=1)])
    inv_r = jnp.power(jnp.float32(RET_THETA), -jnp.linspace(0.0, 1.0, RET_DH // 2, dtype=F32))
    ang_r = posf * inv_r[None, :]
    cr, sr = jnp.cos(ang_r), jnp.sin(ang_r)
    rope_r = jnp.stack([jnp.concatenate([cr, cr], axis=1), jnp.concatenate([-sr, sr], axis=1)])
    return rope_a, rope_r


def _pack_w_in_even(w):
    o = np.cumsum((0, 512, 128, 128, 256, IDX_HEADS, IDX_DIM, 512, 512, 512, 512))
    pad = jnp.zeros(w.shape[:2] + (LANES - IDX_DIM - IDX_HEADS,), w.dtype)
    return jnp.concatenate([w[..., o[0]:o[4]], w[..., o[5]:o[6]], w[..., o[4]:o[5]], pad, w[..., o[6]:o[10]]],
                           axis=-1).astype(BF16)


def kernel(x_prompt, x_sample, cache_k, cache_v, cache_kidx, state_ret, state_conv, state_rglru, page_table,
           c_prompt, c_sample, w_ada, b_ada, g_pre, g_post, w_ffn_gate, w_ffn_up, w_ffn_down,
           w_in_even, w_out_even, w_in_odd, w_out_odd, conv_w, conv_b, w_rg_a, b_rg_a, w_rg_x, b_rg_x, rg_lambda):
    B, L, _ = x_prompt.shape
    Bs, T, _ = x_sample.shape
    n_pages = page_table.shape[1]
    past = n_pages * cache_k.shape[2]
    n_phys = cache_k.shape[1]
    assert cache_k.shape[2] == PAGE and T == SUBLANES

    tp = _Tiling(B, L, 1, 512)
    ts = _Tiling(Bs, T, 64, T)
    ts_odd = _Tiling(Bs, T, 32, T)

    vec3 = lambda a: a.reshape(a.shape[0], 1, a.shape[1])
    p = {
        "g_pre": g_pre.reshape(DEPTH, 3, 1, D), "g_post": g_post.reshape(DEPTH, 3, 1, D),
        "wg": w_ffn_gate.astype(BF16), "wu": w_ffn_up.astype(BF16), "wd": w_ffn_down.astype(BF16),
        "w_in_even": _pack_w_in_even(w_in_even), "w_out_even": w_out_even.astype(BF16),
        "w_in_odd": w_in_odd.astype(BF16), "w_out_odd": w_out_odd.astype(BF16),
        "conv_w": conv_w, "conv_b": vec3(conv_b),
        "w_rg_a": w_rg_a.astype(BF16), "b_rg_a": vec3(b_rg_a),
        "w_rg_x": w_rg_x.astype(BF16), "b_rg_x": vec3(b_rg_x), "rg_lambda": vec3(rg_lambda),
    }
    k_t = jnp.transpose(cache_k, (0, 1, 3, 4, 2)).reshape(cache_k.shape[0], n_phys, LANES, PAGE)
    v_t = jnp.transpose(cache_v, (0, 1, 3, 4, 2)).reshape(cache_v.shape[0], n_phys, LANES, PAGE)
    kidx_t = jnp.transpose(cache_kidx, (0, 1, 3, 2))

    mod_all = _ada(jnp.concatenate([c_prompt, c_sample], axis=0), w_ada, b_ada)
    mod_p = mod_all[:, :B].reshape(DEPTH, B, 1, N_MOD * D)
    mod_s = mod_all[:, B:].reshape(DEPTH, Bs, 1, N_MOD * D)

    rope_a_p, rope_r_p = _rope_tables(jnp.arange(L, dtype=jnp.int32))
    rope_a_s, rope_r_s = _rope_tables(jnp.tile(past + jnp.arange(T, dtype=jnp.int32), ts.g))
    rope_map_p = lambda i, j: (0, j, 0)
    rope_map_s = lambda i, j: (0, 0, 0)

    xp = x_prompt.reshape(B * L, D)
    xs = x_sample.reshape(Bs * T, D)
    ks, vs, kis, rets, convs, hs = ([[], []] for _ in range(6))

    for l in range(DEPTH):
        e = l // 2
        groups = ((0, tp, xp, mod_p[l]), (1, ts, xs, mod_s[l]))
        new_x = []
        for gi, t, x, mod in groups:
            x = _ffn(t, x, mod, 0, l, 0, 0, p)
            if l % 2 == 0:
                if gi == 0:
                    q, kt, v, qi, kw, kit, rq, rk, rv, rg = _even_in(
                        t, True, x, mod, 3, l, e, p, rope_a_p, rope_r_p, rope_map_p)
                    o_a = _dsa_prompt(B, L, q, qi, kw, kt, v)
                    ks[0].append(jnp.transpose(kt.reshape(B, A_KV, A_DH, L), (0, 3, 1, 2)))
                    vs[0].append(v.reshape(B, L, A_KV, A_DH))
                    kis[0].append(jnp.transpose(kit, (0, 2, 1)))
                    x, s_new = _even_tail_prompt(B, L, x, o_a, rq, rk, rv, rg, mod, l, e, p)
                    rets[0].append(s_new)
                    new_x.append(x)
                    continue
                else:
                    q, k, v, qi, kw, rq, rk, rv, rg = _even_in(
                        t, False, x, mod, 3, l, e, p, rope_a_s, rope_r_s, rope_map_s)
                    scores = _dsa_s_scores(Bs, T, e, page_table, qi, kw, kidx_t)
                    bias = _dsa_s_select(Bs * T, T, past, scores)
                    o_a = _dsa_s_attn(Bs, T, e, page_table, q, k, v, bias, k_t, v_t)
                    o_r, s_new = _ret_sample(Bs, T, e, rq, rk, rv, rg, state_ret)
                    ks[1].append(k.reshape(Bs, T, A_KV, A_DH))
                    vs[1].append(v.reshape(Bs, T, A_KV, A_DH))
                    kis[1].append(kw[:, :IDX_DIM].reshape(Bs, T, IDX_DIM))
                rets[gi].append(s_new)
                x = _even_out(t, x, o_a, o_r, mod, 5, l, e, p)
            else:
                if gi == 0:
                    x, tail, hlast = _odd_ffn_prompt(B, L, x, mod, l, e, p)
                    convs[0].append(tail[:, SUBLANES - (CONV_W - 1):])
                    hs[0].append(hlast[:, SUBLANES - 1])
                    new_x.append(x)
                    continue
                else:
                    prev = jnp.pad(state_conv[e], ((0, 0), (SUBLANES - (CONV_W - 1), 0), (0, 0))).reshape(Bs * T, D)
                    h0 = jnp.pad(state_rglru[e][:, None, :], ((0, 0), (0, T - 1), (0, 0))).reshape(Bs * T, D)
                    x, xb, hseq = _odd_sample(ts_odd, x, mod, 3, l, e, p, prev, h0)
                    convs[1].append(xb.reshape(Bs, T, D)[:, T - (CONV_W - 1):])
                    hs[1].append(hseq.reshape(Bs, T, D)[:, T - 1])
            x = _ffn(t, x, mod, 6, l, 1, 2, p)
            new_x.append(x)
        xp, xs = new_x

    st = lambda lists, gi: jnp.stack(lists[gi])
    return (xp.reshape(B, L, D), xs.reshape(Bs, T, D),
            st(ks, 0), st(vs, 0), st(kis, 0), st(rets, 0), st(convs, 0), st(hs, 0),
            st(ks, 1), st(vs, 1), st(kis, 1), st(rets, 1), st(convs, 1), st(hs, 1))
```
